```python
import jax, jax.numpy as jnp
from jax import lax
import numpy as np

D_MODEL = 1024
BATCH = 4
SEQ = 4096
DEPTH = 1

D_MIX = 2 * D_MODEL
D_SSD = D_MIX // 2
SSD_HEAD_DIM = 64
SSD_HEADS = D_SSD // SSD_HEAD_DIM
SSD_GROUPS = 4
SSD_STATE = 128
SSD_CONV = 4
SSD_CHUNK = 128
D_POOL = D_MIX - D_SSD
POOL_WINDOWS = (2, 4, 8, 16)
POOL_GROUPS = len(POOL_WINDOWS)
POOL_GROUP_DIM = D_POOL // POOL_GROUPS
D_XBC = D_SSD + 2 * SSD_GROUPS * SSD_STATE
D_IN_PROJ = D_SSD + D_XBC + SSD_HEADS + D_POOL
D_FF = 2816
N_MOD = 9
FFN_RES = 0.5
EPS = 1e-6

kernel_name = "hybrid_ssd_pool_macaron_adaln"


def rms_norm(x, w):
    xf = x.astype(jnp.float32)
    y = xf * lax.rsqrt(jnp.mean(xf * xf, axis=-1, keepdims=True) + EPS)
    return (y * w.astype(jnp.float32)).astype(x.dtype)


def modulate(h, shift, scale):
    return h * (1 + scale[:, None, :]) + shift[:, None, :]


def swiglu(h, w_gate, w_up, w_down):
    return (jax.nn.silu(h @ w_gate) * (h @ w_up)) @ w_down


def causal_depthwise_conv(u, w, b):
    ch = u.shape[-1]
    y = lax.conv_general_dilated(
        u, w[:, None, :].astype(u.dtype), window_strides=(1,),
        padding=[(SSD_CONV - 1, 0)], dimension_numbers=('NWC', 'WIO', 'NWC'),
        feature_group_count=ch)
    return y + b.astype(u.dtype)


def ssd_chunked(xh, dt, a, bm, cm):
    bsz, L, H, P = xh.shape
    G, N = bm.shape[2], bm.shape[3]
    R = H // G
    Q = SSD_CHUNK
    NC = L // Q
    xdt = (xh * dt[..., None]).reshape(bsz, NC, Q, G, R, P)
    adt = jnp.transpose((dt * a).reshape(bsz, NC, Q, G, R), (0, 3, 4, 1, 2))
    a_cs = jnp.cumsum(adt, axis=-1)
    bc = bm.reshape(bsz, NC, Q, G, N)
    cc = cm.reshape(bsz, NC, Q, G, N)
    causal = jnp.tril(jnp.ones((Q, Q), dtype=bool))
    seg = a_cs[..., :, None] - a_cs[..., None, :]
    decay = jnp.exp(jnp.where(causal, seg, -jnp.inf))
    cb = jnp.einsum('bclgn,bcsgn->bgcls', cc, bc)
    scores = cb[:, :, None] * decay
    y_diag = jnp.einsum('bgrcls,bcsgrp->bclgrp', scores, xdt)
    ds = jnp.transpose(jnp.exp(a_cs[..., -1:] - a_cs), (0, 3, 4, 1, 2))
    states = jnp.einsum('bcsgn,bcsgrp->bcgrpn', bc, xdt * ds[..., None])
    chunk_decay = jnp.moveaxis(jnp.exp(a_cs[..., -1]), -1, 0)

    def step(h, inp):
        s, d = inp
        return h * d[..., None, None] + s, h

    h0 = jnp.zeros((bsz, G, R, P, N), states.dtype)
    _, prev = lax.scan(step, h0, (jnp.moveaxis(states, 1, 0), chunk_decay))
    prev = jnp.moveaxis(prev, 0, 1)
    sd = jnp.transpose(jnp.exp(a_cs), (0, 3, 4, 1, 2))
    y_off = jnp.einsum('bclgn,bcgrpn->bclgrp', cc, prev) * sd[..., None]
    return (y_diag + y_off).reshape(bsz, L, H, P)


def pool_mixer(u, pool_w, pool_b, pool_scale):
    bsz, L, _ = u.shape
    uf = u.astype(jnp.float32).reshape(bsz, L, POOL_GROUPS, POOL_GROUP_DIM)
    cs = jnp.cumsum(uf, axis=1)
    pos = jnp.arange(1, L + 1, dtype=jnp.float32)
    pooled = []
    for gi, w in enumerate(POOL_WINDOWS):
        shifted = jnp.pad(cs[:, :L - w, gi], ((0, 0), (w, 0), (0, 0)))
        cnt = jnp.minimum(pos, float(w))[None, :, None]
        pooled.append((cs[:, :, gi] - shifted) / cnt)
    diff = jnp.stack(pooled, axis=2) - uf
    out = jnp.einsum('blgc,gcd->blgd', diff, pool_w.astype(jnp.float32)) + pool_b.astype(jnp.float32)
    out = out.reshape(bsz, L, D_POOL) * pool_scale.astype(jnp.float32)
    return out.astype(u.dtype)


def token_mixer(h, w_in, conv_w, conv_b, dt_bias, a_log, d_skip, ssd_norm_w,
                pool_w, pool_b, pool_scale, w_out):
    bsz, L, _ = h.shape
    f32 = jnp.float32
    proj = h @ w_in
    z, xbc, dt_raw, u = jnp.split(
        proj, [D_SSD, D_SSD + D_XBC, D_SSD + D_XBC + SSD_HEADS], axis=-1)
    xbc = jax.nn.silu(causal_depthwise_conv(xbc, conv_w, conv_b))
    xs, bm, cm = jnp.split(xbc, [D_SSD, D_SSD + SSD_GROUPS * SSD_STATE], axis=-1)
    dt = jax.nn.softplus(dt_raw.astype(f32) + dt_bias.astype(f32))
    a = -jnp.exp(a_log.astype(f32))
    xh = xs.astype(f32).reshape(bsz, L, SSD_HEADS, SSD_HEAD_DIM)
    y = ssd_chunked(xh, dt, a,
                    bm.astype(f32).reshape(bsz, L, SSD_GROUPS, SSD_STATE),
                    cm.astype(f32).reshape(bsz, L, SSD_GROUPS, SSD_STATE))
    y = (y + d_skip.astype(f32)[:, None] * xh).reshape(bsz, L, D_SSD)
    yg = (y * jax.nn.silu(z.astype(f32))).reshape(bsz, L, SSD_GROUPS, D_SSD // SSD_GROUPS)
    yg = yg * lax.rsqrt(jnp.mean(yg * yg, axis=-1, keepdims=True) + EPS)
    y_ssd = (yg.reshape(bsz, L, D_SSD) * ssd_norm_w.astype(f32)).astype(h.dtype)
    y_pool = pool_mixer(u, pool_w, pool_b, pool_scale)
    return jnp.concatenate([y_ssd, y_pool], axis=-1) @ w_out


def setup_inputs(seed: int = 0) -> dict:
    key = jax.random.key(seed)
    ks = jax.random.split(key, 32)
    f32 = jnp.float32

    def normal(k, shape, scale):
        return jax.random.normal(k, shape, f32) * scale

    dt0 = jnp.exp(jax.random.uniform(ks[9], (DEPTH, SSD_HEADS), f32,
                                     minval=np.log(1e-3), maxval=np.log(1e-1)))
    return {
        "x": normal(ks[0], (BATCH, SEQ, D_MODEL), 1.0),
        "c": normal(ks[1], (BATCH, D_MODEL), 1.0),
        "w_ada": normal(ks[2], (DEPTH, D_MODEL, N_MOD * D_MODEL), 0.5 * D_MODEL ** -0.5),
        "b_ada": normal(ks[3], (DEPTH, N_MOD * D_MODEL), 0.02),
        "ffn1_norm": 1.0 + normal(ks[4], (DEPTH, D_MODEL), 0.02),
        "ffn1_w_gate": normal(ks[5], (DEPTH, D_MODEL, D_FF), D_MODEL ** -0.5),
        "ffn1_w_up": normal(ks[6], (DEPTH, D_MODEL, D_FF), D_MODEL ** -0.5),
        "ffn1_w_down": normal(ks[7], (DEPTH, D_FF, D_MODEL), D_FF ** -0.5),
        "mix_norm": 1.0 + normal(ks[8], (DEPTH, D_MODEL), 0.02),
        "w_in": normal(ks[10], (DEPTH, D_MODEL, D_IN_PROJ), D_MODEL ** -0.5),
        "conv_w": normal(ks[11], (DEPTH, SSD_CONV, D_XBC), SSD_CONV ** -0.5),
        "conv_b": normal(ks[12], (DEPTH, D_XBC), 0.02),
        "dt_bias": dt0 + jnp.log(-jnp.expm1(-dt0)),
        "a_log": jnp.log(jax.random.uniform(ks[13], (DEPTH, SSD_HEADS), f32, minval=1.0, maxval=16.0)),
        "d_skip": 1.0 + normal(ks[14], (DEPTH, SSD_HEADS), 0.02),
        "ssd_norm_w": 1.0 + normal(ks[15], (DEPTH, D_SSD), 0.02),
        "pool_w": normal(ks[16], (DEPTH, POOL_GROUPS, POOL_GROUP_DIM, POOL_GROUP_DIM), POOL_GROUP_DIM ** -0.5),
        "pool_b": normal(ks[17], (DEPTH, POOL_GROUPS, POOL_GROUP_DIM), 0.02),
        "pool_scale": 1.0 + normal(ks[18], (DEPTH, D_POOL), 0.02),
        "w_out": normal(ks[19], (DEPTH, D_MIX, D_MODEL), D_MIX ** -0.5),
        "ffn2_norm": 1.0 + normal(ks[20], (DEPTH, D_MODEL), 0.02),
        "ffn2_w_gate": normal(ks[21], (DEPTH, D_MODEL, D_FF), D_MODEL ** -0.5),
        "ffn2_w_up": normal(ks[22], (DEPTH, D_MODEL, D_FF), D_MODEL ** -0.5),
        "ffn2_w_down": normal(ks[23], (DEPTH, D_FF, D_MODEL), D_FF ** -0.5),
        "final_norm": 1.0 + normal(ks[24], (D_MODEL,), 0.02),
    }


def reference(x, c, w_ada, b_ada, ffn1_norm, ffn1_w_gate, ffn1_w_up, ffn1_w_down,
              mix_norm, w_in, conv_w, conv_b, dt_bias, a_log, d_skip, ssd_norm_w,
              pool_w, pool_b, pool_scale, w_out, ffn2_norm, ffn2_w_gate, ffn2_w_up,
              ffn2_w_down, final_norm):
    c_act = jax.nn.silu(c)
    for i in range(DEPTH):
        mod = c_act @ w_ada[i] + b_ada[i]
        sh1, sc1, g1, sh2, sc2, g2, sh3, sc3, g3 = jnp.split(mod, N_MOD, axis=-1)
        h = modulate(rms_norm(x, ffn1_norm[i]), sh1, sc1)
        x = x + FFN_RES * g1[:, None, :] * swiglu(h, ffn1_w_gate[i], ffn1_w_up[i], ffn1_w_down[i])
        h = modulate(rms_norm(x, mix_norm[i]), sh2, sc2)
        x = x + g2[:, None, :] * token_mixer(
            h, w_in[i], conv_w[i], conv_b[i], dt_bias[i], a_log[i], d_skip[i], ssd_norm_w[i],
            pool_w[i], pool_b[i], pool_scale[i], w_out[i])
        h = modulate(rms_norm(x, ffn2_norm[i]), sh3, sc3)
        x = x + FFN_RES * g3[:, None, :] * swiglu(h, ffn2_w_gate[i], ffn2_w_up[i], ffn2_w_down[i])
    return rms_norm(x, final_norm)
```

```python
import functools

import jax
import jax.numpy as jnp
from jax import lax
from jax.experimental import pallas as pl
from jax.experimental.pallas import tpu as pltpu

F32 = jnp.float32
BF16 = jnp.bfloat16

D_MODEL = 1024
D_FF = 2816
N_MOD = 9
FFN_RES = 0.5
EPS = 1e-6
D_SSD = 1024
HEAD_DIM = 64
N_HEADS = 16
N_GROUPS = 4
HEADS_PER_GROUP = N_HEADS // N_GROUPS
D_STATE = 128
D_CONV = 4
CHUNK = 128
D_POOL = 1024
POOL_WINDOWS = (2, 4, 8, 16)
POOL_GROUP_DIM = 256
D_XBC = D_SSD + 2 * N_GROUPS * D_STATE

LANES = 128
SUBLANES = 8
VMEM_LIMIT_BYTES = 56 * 1024 * 1024

FFN_ROWS = 512
MIX_ROWS = 512
ADA_COLS = 1024
CONV_PAD = SUBLANES
POOL_PAD = 16

NT_DIMS = (((1,), (1,)), ((), ()))


def _dot(a, b):
    return jnp.dot(a, b, preferred_element_type=F32)


def _dot_nt(a, b):
    return lax.dot_general(a, b, NT_DIMS, preferred_element_type=F32)


def _split3(v):
    hi = v.astype(BF16)
    r1 = v - hi.astype(F32)
    mid = r1.astype(BF16)
    lo = (r1 - mid.astype(F32)).astype(BF16)
    return hi, mid, lo


def _silu(v):
    return v * jax.nn.sigmoid(v)


def _rms(v, w):
    ms = jnp.mean(v * v, axis=-1, keepdims=True)
    return v * lax.rsqrt(ms + EPS) * w


def _const_spec(shape):
    zeros = (0,) * len(shape)
    return pl.BlockSpec(shape, lambda *_: zeros, pipeline_mode=pl.Buffered(1))


def _adaln_kernel(c_ref, w_ref, b_ref, o_ref):
    c = c_ref[...]
    a_hi, a_mid, _ = _split3(_silu(c))
    w_hi, w_mid, _ = _split3(w_ref[...])
    acc = _dot(a_hi, w_hi) + _dot(a_mid, w_hi) + _dot(a_hi, w_mid)
    o_ref[...] = acc + b_ref[...]


def _adaln(c_pad, w_ada, b_ada):
    rows = c_pad.shape[0]
    n = w_ada.shape[1]
    return pl.pallas_call(
        _adaln_kernel,
        grid=(n // ADA_COLS,),
        in_specs=[
            pl.BlockSpec((rows, D_MODEL), lambda j: (0, 0)),
            pl.BlockSpec((D_MODEL, ADA_COLS), lambda j: (0, j)),
            pl.BlockSpec((1, ADA_COLS), lambda j: (0, j)),
        ],
        out_specs=pl.BlockSpec((rows, ADA_COLS), lambda j: (0, j)),
        out_shape=jax.ShapeDtypeStruct((rows, n), F32),
        compiler_params=pltpu.CompilerParams(
            dimension_semantics=("arbitrary",), vmem_limit_bytes=VMEM_LIMIT_BYTES),
        name="adaln",
    )(c_pad, w_ada, b_ada)


def _ffn_kernel(x_ref, mod_ref, nw_ref, wg_ref, wu_ref, wd_ref, fw_ref, o_ref, *, mod_idx, final):
    x = x_ref[...]
    sh = mod_ref[0, pl.ds(mod_idx, 1), :]
    sc = mod_ref[0, pl.ds(mod_idx + 1, 1), :]
    gt = mod_ref[0, pl.ds(mod_idx + 2, 1), :]
    h = (_rms(x, nw_ref[...]) * (1.0 + sc) + sh).astype(BF16)
    g = _dot(h, wg_ref[...])
    u = _dot(h, wu_ref[...])
    a = (_silu(g) * u).astype(BF16)
    o = _dot(a, wd_ref[...])
    out = x + (FFN_RES * gt) * o
    if final:
        out = _rms(out, fw_ref[...])
    o_ref[...] = out


def _ffn(x, mod3, norm_w, wg, wu, wd, final_w, *, mod_idx, final):
    bsz, seq, _ = x.shape
    nt = seq // FFN_ROWS
    row_spec = pl.BlockSpec((None, FFN_ROWS, D_MODEL), lambda b, i: (b, i, 0))
    return pl.pallas_call(
        functools.partial(_ffn_kernel, mod_idx=mod_idx, final=final),
        grid=(bsz, nt),
        in_specs=[
            row_spec,
            pl.BlockSpec((1, N_MOD, D_MODEL), lambda b, i: (b, 0, 0)),
            _const_spec((1, D_MODEL)),
            _const_spec((D_MODEL, D_FF)),
            _const_spec((D_MODEL, D_FF)),
            _const_spec((D_FF, D_MODEL)),
            _const_spec((1, D_MODEL)),
        ],
        out_specs=row_spec,
        out_shape=jax.ShapeDtypeStruct(x.shape, F32),
        compiler_params=pltpu.CompilerParams(
            dimension_semantics=("arbitrary", "arbitrary"), vmem_limit_bytes=VMEM_LIMIT_BYTES),
        name="ffn_final" if final else "ffn",
    )(x, mod3, norm_w, wg, wu, wd, final_w)


def _mixer_kernel(x_ref, mod_ref, nw_ref, wz_ref, wxbc_ref, wdt_ref, wu_ref,
                  convw_ref, convb_ref, dtb_ref, alog_ref, dskip_ref, ssdnw_ref,
                  poolw_ref, poolb_ref, pools_ref, wout_ref,
                  o_ref,
                  cbuf, xbc_s, dt_s, y_s, ubuf, st_s):
    rows = MIX_ROWS
    j = pl.program_id(1)

    @pl.when(j == 0)
    def _():
        cbuf[0:CONV_PAD, :] = jnp.zeros((CONV_PAD, D_XBC), F32)
        ubuf[0:POOL_PAD, :] = jnp.zeros((POOL_PAD, D_POOL), F32)
        st_s[...] = jnp.zeros(st_s.shape, F32)

    x = x_ref[...]
    sh = mod_ref[0, pl.ds(3, 1), :]
    sc = mod_ref[0, pl.ds(4, 1), :]
    gt = mod_ref[0, pl.ds(5, 1), :]
    h = (_rms(x, nw_ref[...]) * (1.0 + sc) + sh).astype(BF16)

    z = _dot(h, wz_ref[...])
    cbuf[CONV_PAD:CONV_PAD + rows, :] = _dot(h, wxbc_ref[...])
    dt_raw = _dot(h, wdt_ref[...])
    ubuf[POOL_PAD:POOL_PAD + rows, :] = _dot(h, wu_ref[...])

    cw = convw_ref[...]
    xc = convb_ref[...]
    for k in range(D_CONV):
        off = CONV_PAD - (D_CONV - 1) + k
        xc = xc + cw[k:k + 1, :] * cbuf[off:off + rows, :]
    cbuf[0:CONV_PAD, :] = cbuf[rows:rows + CONV_PAD, :]
    xbc_s[...] = _silu(xc)

    dt_s[...] = jax.nn.softplus(dt_raw + dtb_ref[...])
    a_neg = -jnp.exp(alog_ref[...])

    ri = lax.broadcasted_iota(jnp.int32, (CHUNK, CHUNK), 0)
    ci = lax.broadcasted_iota(jnp.int32, (CHUNK, CHUNK), 1)
    causal = ri >= ci
    tril = jnp.where(causal, 1.0, 0.0).astype(BF16)

    def chunk_body(c, carry):
        r0 = pl.multiple_of(c * CHUNK, CHUNK)
        dt_c = dt_s[pl.ds(r0, CHUNK), :]
        p_hi, p_mid, p_lo = _split3(dt_c * a_neg)
        cs = _dot(tril, p_hi) + _dot(tril, p_mid) + _dot(tril, p_lo)
        cs_t = cs.T
        dt_t = dt_c.T
        sd = jnp.exp(cs)
        cs_end_t = cs_t[:, CHUNK - 1:CHUNK]
        w_t = dt_t * jnp.exp(cs_end_t - cs_t)
        cd_t = jnp.exp(cs_end_t)
        xs_t = xbc_s[pl.ds(r0, CHUNK), 0:D_SSD].T
        for g in range(N_GROUPS):
            b0 = D_SSD + g * D_STATE
            c0 = D_SSD + N_GROUPS * D_STATE + g * D_STATE
            b_g = xbc_s[pl.ds(r0, CHUNK), b0:b0 + D_STATE]
            c_g = xbc_s[pl.ds(r0, CHUNK), c0:c0 + D_STATE]
            b_bf = b_g.astype(BF16)
            cb = _dot_nt(c_g.astype(BF16), b_bf)
            for r in range(HEADS_PER_GROUP):
                hd = g * HEADS_PER_GROUP + r
                seg = cs[:, hd:hd + 1] - cs_t[hd:hd + 1, :]
                decay = jnp.exp(jnp.where(causal, seg, -jnp.inf))
                scores = decay * (cb * dt_t[hd:hd + 1, :])
                c_sd = c_g * sd[:, hd:hd + 1]
                lhs = jnp.concatenate([scores, c_sd], axis=1).astype(BF16)
                x_t = xs_t[hd * HEAD_DIM:(hd + 1) * HEAD_DIM, :]
                st = st_s[hd]
                rhs_t = jnp.concatenate([x_t, st], axis=1).astype(BF16)
                y_s[pl.ds(r0, CHUNK), hd * HEAD_DIM:(hd + 1) * HEAD_DIM] = _dot_nt(lhs, rhs_t)
                xw = (x_t * w_t[hd:hd + 1, :]).astype(BF16)
                st_s[hd] = st * cd_t[hd:hd + 1, :] + _dot(xw, b_bf)
        return carry

    lax.fori_loop(0, rows // CHUNK, chunk_body, 0)

    y = y_s[...] + dskip_ref[...] * xbc_s[:, 0:D_SSD]
    yz = y * _silu(z)
    gw = D_SSD // N_GROUPS
    parts = []
    for g in range(N_GROUPS):
        blk = yz[:, g * gw:(g + 1) * gw]
        ms = jnp.mean(blk * blk, axis=-1, keepdims=True)
        parts.append(blk * lax.rsqrt(ms + EPS))
    y_ssd = (jnp.concatenate(parts, axis=1) * ssdnw_ref[...]).astype(BF16)

    pos = (j * rows + 1 + lax.broadcasted_iota(jnp.int32, (rows, 1), 0)).astype(F32)
    outs = []
    for gi, w in enumerate(POOL_WINDOWS):
        lo, hi = gi * POOL_GROUP_DIM, (gi + 1) * POOL_GROUP_DIM
        u_g = ubuf[POOL_PAD:POOL_PAD + rows, lo:hi]
        s = u_g
        for k in range(1, w):
            s = s + ubuf[POOL_PAD - k:POOL_PAD - k + rows, lo:hi]
        diff = s / jnp.minimum(pos, float(w)) - u_g
        o = _dot(diff.astype(BF16), poolw_ref[gi]) + poolb_ref[:, lo:hi]
        outs.append(o * pools_ref[:, lo:hi])
    ubuf[0:POOL_PAD, :] = ubuf[rows:rows + POOL_PAD, :]
    y_pool = jnp.concatenate(outs, axis=1).astype(BF16)

    out = _dot(y_ssd, wout_ref[0:D_SSD, :]) + _dot(y_pool, wout_ref[D_SSD:D_SSD + D_POOL, :])
    o_ref[...] = x + gt * out


def _mixer(x, mod3, norm_w, wz, wxbc, wdt, wu, convw, convb, dtb, alog, dskip, ssdnw,
           poolw, poolb, pools, wout):
    bsz, seq, _ = x.shape
    nt = seq // MIX_ROWS
    row_spec = pl.BlockSpec((None, MIX_ROWS, D_MODEL), lambda b, i: (b, i, 0))
    consts = [norm_w, wz, wxbc, wdt, wu, convw, convb, dtb, alog, dskip, ssdnw,
              poolw, poolb, pools, wout]
    return pl.pallas_call(
        _mixer_kernel,
        grid=(bsz, nt),
        in_specs=[row_spec, pl.BlockSpec((1, N_MOD, D_MODEL), lambda b, i: (b, 0, 0))]
        + [_const_spec(a.shape) for a in consts],
        out_specs=row_spec,
        out_shape=jax.ShapeDtypeStruct(x.shape, F32),
        scratch_shapes=[
            pltpu.VMEM((CONV_PAD + MIX_ROWS, D_XBC), F32),
            pltpu.VMEM((MIX_ROWS, D_XBC), F32),
            pltpu.VMEM((MIX_ROWS, LANES), F32),
            pltpu.VMEM((MIX_ROWS, D_SSD), F32),
            pltpu.VMEM((POOL_PAD + MIX_ROWS, D_POOL), F32),
            pltpu.VMEM((N_HEADS, HEAD_DIM, D_STATE), F32),
        ],
        compiler_params=pltpu.CompilerParams(
            dimension_semantics=("arbitrary", "arbitrary"), vmem_limit_bytes=VMEM_LIMIT_BYTES),
        name="mixer",
    )(x, mod3, *consts)


def _pad_lanes(v):
    return jnp.pad(v.reshape(1, -1), ((0, 0), (0, LANES - v.shape[0])))


def kernel(x, c, w_ada, b_ada, ffn1_norm, ffn1_w_gate, ffn1_w_up, ffn1_w_down, mix_norm, w_in, conv_w, conv_b, dt_bias, a_log, d_skip, ssd_norm_w, pool_w, pool_b, pool_scale, w_out, ffn2_norm, ffn2_w_gate, ffn2_w_up, ffn2_w_down, final_norm):
    bsz = x.shape[0]
    depth = w_ada.shape[0]
    c_pad = jnp.pad(c, ((0, SUBLANES - bsz), (0, 0)))
    row = lambda v: v.reshape(1, -1)
    for i in range(depth):
        mod = _adaln(c_pad, w_ada[i], row(b_ada[i]))
        mod3 = mod[:bsz].reshape(bsz, N_MOD, D_MODEL)
        last = i == depth - 1

        x = _ffn(x, mod3, row(ffn1_norm[i]), ffn1_w_gate[i].astype(BF16), ffn1_w_up[i].astype(BF16),
                 ffn1_w_down[i].astype(BF16), row(final_norm), mod_idx=0, final=False)

        wi = w_in[i]
        o_xbc = D_SSD
        o_dt = o_xbc + D_XBC
        o_u = o_dt + N_HEADS
        wdt = jnp.pad(wi[:, o_dt:o_u], ((0, 0), (0, LANES - N_HEADS)))
        x = _mixer(
            x, mod3, row(mix_norm[i]),
            wi[:, :o_xbc].astype(BF16), wi[:, o_xbc:o_dt].astype(BF16), wdt.astype(BF16),
            wi[:, o_u:].astype(BF16),
            conv_w[i], row(conv_b[i]), _pad_lanes(dt_bias[i]), _pad_lanes(a_log[i]),
            row(jnp.repeat(d_skip[i], HEAD_DIM)), row(ssd_norm_w[i]),
            pool_w[i].astype(BF16), row(pool_b[i]), row(pool_scale[i]), w_out[i].astype(BF16))

        x = _ffn(x, mod3, row(ffn2_norm[i]), ffn2_w_gate[i].astype(BF16), ffn2_w_up[i].astype(BF16),
                 ffn2_w_down[i].astype(BF16), row(final_norm), mod_idx=6, final=last)
    if depth == 0:
        raise ValueError("depth must be >= 1")
    return x
```

```python
import functools

import jax
import jax.numpy as jnp
from jax import lax
from jax.experimental import pallas as pl
from jax.experimental.pallas import tpu as pltpu

F32 = jnp.float32
BF16 = jnp.bfloat16

D_MODEL = 1024
D_FF = 2816
N_MOD = 9
FFN_RES = 0.5
EPS = 1e-6
D_SSD = 1024
HEAD_DIM = 64
N_HEADS = 16
N_GROUPS = 4
HEADS_PER_GROUP = N_HEADS // N_GROUPS
D_STATE = 128
D_CONV = 4
CHUNK = 128
D_POOL = 1024
POOL_WINDOWS = (2, 4, 8, 16)
POOL_GROUP_DIM = 256
D_XBC = D_SSD + 2 * N_GROUPS * D_STATE

LANES = 128
SUBLANES = 8
VMEM_LIMIT_BYTES = 56 * 1024 * 1024

FFN_ROWS = 512
MIX_ROWS = 512
ADA_COLS = 1024
CONV_PAD = SUBLANES
POOL_PAD = 16

assert HEAD_DIM * 2 == LANES and D_STATE == LANES and CHUNK == LANES


def _dot(a, b):
    return jnp.dot(a, b, preferred_element_type=F32)


def _pack_rows(w):
    wb = w.astype(BF16)
    *lead, k, n = wb.shape
    pairs = jnp.swapaxes(wb.reshape(*lead, k // 2, 2, n), -1, -2)
    return lax.bitcast_convert_type(pairs, jnp.uint32)


def _wdot(a, w_packed):
    return _dot(a, pltpu.bitcast(w_packed, BF16))


def _split3(v):
    hi = v.astype(BF16)
    r1 = v - hi.astype(F32)
    mid = r1.astype(BF16)
    lo = (r1 - mid.astype(F32)).astype(BF16)
    return hi, mid, lo


def _silu(v):
    return v * jax.nn.sigmoid(v)


def _rms(v, w):
    ms = jnp.mean(v * v, axis=-1, keepdims=True)
    return v * lax.rsqrt(ms + EPS) * w


def _const_spec(shape):
    zeros = (0,) * len(shape)
    return pl.BlockSpec(shape, lambda *_: zeros, pipeline_mode=pl.Buffered(1))


def _adaln_kernel(c_ref, w_ref, b_ref, o_ref):
    c = c_ref[...]
    a_hi, a_mid, _ = _split3(_silu(c))
    w_hi, w_mid, _ = _split3(w_ref[...])
    acc = _dot(a_hi, w_hi) + _dot(a_mid, w_hi) + _dot(a_hi, w_mid)
    o_ref[...] = acc + b_ref[...]


def _adaln(c_pad, w_ada, b_ada):
    rows = c_pad.shape[0]
    n = w_ada.shape[1]
    return pl.pallas_call(
        _adaln_kernel,
        grid=(n // ADA_COLS,),
        in_specs=[
            pl.BlockSpec((rows, D_MODEL), lambda j: (0, 0)),
            pl.BlockSpec((D_MODEL, ADA_COLS), lambda j: (0, j)),
            pl.BlockSpec((1, ADA_COLS), lambda j: (0, j)),
        ],
        out_specs=pl.BlockSpec((rows, ADA_COLS), lambda j: (0, j)),
        out_shape=jax.ShapeDtypeStruct((rows, n), F32),
        compiler_params=pltpu.CompilerParams(
            dimension_semantics=("arbitrary",), vmem_limit_bytes=VMEM_LIMIT_BYTES),
        name="adaln",
    )(c_pad, w_ada, b_ada)


def _ffn_kernel(x_ref, mod_ref, nw_ref, wg_ref, wu_ref, wd_ref, fw_ref, o_ref, *, mod_idx, final):
    x = x_ref[...]
    sh = mod_ref[0, pl.ds(mod_idx, 1), :]
    sc = mod_ref[0, pl.ds(mod_idx + 1, 1), :]
    gt = mod_ref[0, pl.ds(mod_idx + 2, 1), :]
    h = (_rms(x, nw_ref[...]) * (1.0 + sc) + sh).astype(BF16)
    g = _wdot(h, wg_ref[...])
    u = _wdot(h, wu_ref[...])
    a = (_silu(g) * u).astype(BF16)
    o = _wdot(a, wd_ref[...])
    out = x + (FFN_RES * gt) * o
    if final:
        out = _rms(out, fw_ref[...])
    o_ref[...] = out


def _ffn(x, mod3, norm_w, wg, wu, wd, final_w, *, mod_idx, final):
    bsz, seq, _ = x.shape
    nt = seq // FFN_ROWS
    row_spec = pl.BlockSpec((None, FFN_ROWS, D_MODEL), lambda b, i: (b, i, 0))
    return pl.pallas_call(
        functools.partial(_ffn_kernel, mod_idx=mod_idx, final=final),
        grid=(bsz, nt),
        in_specs=[
            row_spec,
            pl.BlockSpec((1, N_MOD, D_MODEL), lambda b, i: (b, 0, 0)),
            _const_spec((1, D_MODEL)),
            _const_spec((D_MODEL // 2, D_FF)),
            _const_spec((D_MODEL // 2, D_FF)),
            _const_spec((D_FF // 2, D_MODEL)),
            _const_spec((1, D_MODEL)),
        ],
        out_specs=row_spec,
        out_shape=jax.ShapeDtypeStruct(x.shape, F32),
        compiler_params=pltpu.CompilerParams(
            dimension_semantics=("arbitrary", "arbitrary"), vmem_limit_bytes=VMEM_LIMIT_BYTES),
        name="ffn_final" if final else "ffn",
    )(x, mod3, norm_w, wg, wu, wd, final_w)


def _mixer_kernel(xc_ref, xn_ref, modc_ref, modn_ref, nw_ref, wz_ref, wxbc_ref, wdt_ref, wu_ref,
                  convw_ref, convb_ref, dtb_ref, alog_ref, dskip_ref, ssdnw_ref,
                  poolw_ref, poolb_ref, pools_ref, wout_ref,
                  o_ref,
                  z_s, raw_s, dtr_s, u_s, xbc_s, hraw_s, hu_s, st_s, *, tiles_per_seq):
    step = pl.program_id(0)
    tile_in_seq = step % tiles_per_seq
    next_starts_seq = tile_in_seq == tiles_per_seq - 1
    n_chunks = MIX_ROWS // CHUNK

    def in_norm(x, mod_ref):
        sh = mod_ref[0, 3:4, :]
        sc = mod_ref[0, 4:5, :]
        return (_rms(x, nw_ref[...]) * (1.0 + sc) + sh).astype(BF16)

    @pl.when(step == 0)
    def _():
        h0 = in_norm(xc_ref[...], modc_ref)
        z_s[...] = _wdot(h0, wz_ref[...])
        raw_s[...] = _wdot(h0, wxbc_ref[...])
        dtr_s[...] = _wdot(h0, wdt_ref[...])
        u_s[...] = _wdot(h0, wu_ref[...])
        hraw_s[...] = jnp.zeros(hraw_s.shape, F32)
        hu_s[...] = jnp.zeros(hu_s.shape, F32)

    @pl.when(tile_in_seq == 0)
    def _():
        st_s[...] = jnp.zeros(st_s.shape, F32)

    gt = modc_ref[0, 5:6, :]
    a_neg = -jnp.exp(alog_ref[...])
    cw = convw_ref[...]
    ri = lax.broadcasted_iota(jnp.int32, (CHUNK, CHUNK), 0)
    ci = lax.broadcasted_iota(jnp.int32, (CHUNK, CHUNK), 1)
    causal = ri >= ci
    tril = jnp.where(causal, 1.0, 0.0).astype(BF16)
    low_half = ci < HEAD_DIM
    low_half_row = lax.broadcasted_iota(jnp.int32, (1, LANES), 1) < HEAD_DIM

    def chunk(c):
        last = c == n_chunks - 1
        rs = slice(c * CHUNK, (c + 1) * CHUNK)
        h_n = in_norm(xn_ref[rs, :], modn_ref)

        cur = raw_s[rs, :]
        full = jnp.concatenate([hraw_s[...], cur], axis=0)
        xc = convb_ref[...] + cw[D_CONV - 1:D_CONV, :] * cur
        for k in range(D_CONV - 1):
            shifted = pltpu.roll(full, D_CONV - 1 - k, 0)[CONV_PAD:, :]
            xc = xc + cw[k:k + 1, :] * shifted
        hist = cur[CHUNK - CONV_PAD:, :]
        if last:
            hist = jnp.where(next_starts_seq, jnp.zeros_like(hist), hist)
        hraw_s[...] = hist
        xbc_s[rs, :] = _silu(xc)
        raw_s[rs, :] = _wdot(h_n, wxbc_ref[...])

        u_c = u_s[rs, :]
        ext = jnp.concatenate([hu_s[...], u_c], axis=0)
        sums = []
        acc = ext
        for gi, w in enumerate(POOL_WINDOWS):
            acc = acc + pltpu.roll(acc, w // 2, 0)
            sums.append(acc[POOL_PAD:, 0:POOL_GROUP_DIM])
            if gi + 1 < len(POOL_WINDOWS):
                acc = acc[:, POOL_GROUP_DIM:]
        hist = u_c[CHUNK - POOL_PAD:, :]
        if last:
            hist = jnp.where(next_starts_seq, jnp.zeros_like(hist), hist)
        hu_s[...] = hist
        diffs = []
        for gi, w in enumerate(POOL_WINDOWS):
            u_g = u_c[:, gi * POOL_GROUP_DIM:(gi + 1) * POOL_GROUP_DIM]
            if c == 0:
                pos = tile_in_seq * MIX_ROWS + 1 + lax.broadcasted_iota(
                    jnp.int32, (CHUNK, POOL_GROUP_DIM), 0)
                pooled = sums[gi] / jnp.minimum(pos, w).astype(F32)
            else:
                pooled = sums[gi] * (1.0 / w)
            diffs.append((pooled - u_g).astype(BF16))
        u_s[rs, :] = _wdot(h_n, wu_ref[...])

        dt_c = jax.nn.softplus(dtr_s[rs, :] + dtb_ref[...])
        p_hi, p_mid, p_lo = _split3(dt_c * a_neg)
        cs = _dot(tril, p_hi) + _dot(tril, p_mid) + _dot(tril, p_lo)
        cs_t = cs.T
        dt_t = dt_c.T
        w_t = dt_t * jnp.exp(cs_t[:, CHUNK - 1:CHUNK] - cs_t)
        xs_c = xbc_s[rs, 0:D_SSD]
        y_pairs = []
        for g in range(N_GROUPS):
            b0 = D_SSD + g * D_STATE
            c0 = D_SSD + N_GROUPS * D_STATE + g * D_STATE
            b_t = xbc_s[rs, b0:b0 + D_STATE].T
            c_g = xbc_s[rs, c0:c0 + D_STATE]
            cb = _dot(c_g.astype(BF16), b_t.astype(BF16))
            lhs, bw_t, cd = [], [], []
            for r in range(HEADS_PER_GROUP):
                hd = g * HEADS_PER_GROUP + r
                cs_bc = jnp.broadcast_to(cs[:, hd:hd + 1], (CHUNK, CHUNK))
                decay = jnp.exp(jnp.where(causal, cs_bc - cs_t[hd:hd + 1, :], -jnp.inf))
                scores = decay * (cb * dt_t[hd:hd + 1, :])
                sd = jnp.exp(cs_bc)
                lhs.append(jnp.concatenate([scores, c_g * sd], axis=1).astype(BF16))
                bw_t.append((b_t * w_t[hd:hd + 1, :]).astype(BF16))
                cd.append(sd[CHUNK - 1:CHUNK, :])
            for pr in range(HEADS_PER_GROUP // 2):
                h0 = 2 * pr
                col = (g * HEADS_PER_GROUP + h0) * HEAD_DIM
                x_p = xs_c[:, col:col + LANES]
                st = st_s[:, col:col + LANES]
                rhs = jnp.concatenate([x_p, st], axis=0).astype(BF16)
                y_pairs.append(jnp.where(low_half, _dot(lhs[h0], rhs), _dot(lhs[h0 + 1], rhs)))
                zero = jnp.zeros_like(x_p)
                x_split = jnp.concatenate(
                    [jnp.where(low_half, x_p, zero), jnp.where(low_half, zero, x_p)], axis=0)
                upd = _dot(jnp.concatenate([bw_t[h0], bw_t[h0 + 1]], axis=1), x_split.astype(BF16))
                st_s[:, col:col + LANES] = st * jnp.where(low_half_row, cd[h0], cd[h0 + 1]) + upd

        y = jnp.concatenate(y_pairs, axis=1) + dskip_ref[...] * xs_c
        yz = y * _silu(z_s[rs, :])
        gw = D_SSD // N_GROUPS
        parts = []
        for g in range(N_GROUPS):
            blk = yz[:, g * gw:(g + 1) * gw]
            ms = jnp.mean(blk * blk, axis=-1, keepdims=True)
            parts.append(blk * lax.rsqrt(ms + EPS))
        y_ssd = (jnp.concatenate(parts, axis=1) * ssdnw_ref[...]).astype(BF16)
        z_s[rs, :] = _wdot(h_n, wz_ref[...])
        dtr_s[rs, :] = _wdot(h_n, wdt_ref[...])

        outs = []
        for gi in range(len(POOL_WINDOWS)):
            lo, hi = gi * POOL_GROUP_DIM, (gi + 1) * POOL_GROUP_DIM
            outs.append((_wdot(diffs[gi], poolw_ref[gi]) + poolb_ref[:, lo:hi]) * pools_ref[:, lo:hi])
        y_pool = jnp.concatenate(outs, axis=1).astype(BF16)
        out = (_wdot(y_ssd, wout_ref[0:D_SSD // 2, :])
               + _wdot(y_pool, wout_ref[D_SSD // 2:(D_SSD + D_POOL) // 2, :]))
        o_ref[rs, :] = xc_ref[rs, :] + gt * out

    for c in range(n_chunks):
        chunk(c)


def _mixer(x, mod3, norm_w, wz, wxbc, wdt, wu, convw, convb, dtb, alog, dskip, ssdnw,
           poolw, poolb, pools, wout):
    bsz, seq, _ = x.shape
    tiles_per_seq = seq // MIX_ROWS
    steps = bsz * tiles_per_seq
    nxt = lambda s: jnp.minimum(s + 1, steps - 1)
    consts = [norm_w, wz, wxbc, wdt, wu, convw, convb, dtb, alog, dskip, ssdnw,
              poolw, poolb, pools, wout]
    out = pl.pallas_call(
        functools.partial(_mixer_kernel, tiles_per_seq=tiles_per_seq),
        grid=(steps,),
        in_specs=[
            pl.BlockSpec((MIX_ROWS, D_MODEL), lambda s: (s, 0)),
            pl.BlockSpec((MIX_ROWS, D_MODEL), lambda s: (nxt(s), 0)),
            pl.BlockSpec((1, N_MOD, D_MODEL), lambda s: (s // tiles_per_seq, 0, 0)),
            pl.BlockSpec((1, N_MOD, D_MODEL), lambda s: (nxt(s) // tiles_per_seq, 0, 0)),
        ] + [_const_spec(a.shape) for a in consts],
        out_specs=pl.BlockSpec((MIX_ROWS, D_MODEL), lambda s: (s, 0)),
        out_shape=jax.ShapeDtypeStruct((bsz * seq, D_MODEL), F32),
        scratch_shapes=[
            pltpu.VMEM((MIX_ROWS, D_SSD), F32),
            pltpu.VMEM((MIX_ROWS, D_XBC), F32),
            pltpu.VMEM((MIX_ROWS, LANES), F32),
            pltpu.VMEM((MIX_ROWS, D_POOL), F32),
            pltpu.VMEM((MIX_ROWS, D_XBC), F32),
            pltpu.VMEM((CONV_PAD, D_XBC), F32),
            pltpu.VMEM((POOL_PAD, D_POOL), F32),
            pltpu.VMEM((D_STATE, D_SSD), F32),
        ],
        compiler_params=pltpu.CompilerParams(
            dimension_semantics=("arbitrary",), vmem_limit_bytes=VMEM_LIMIT_BYTES),
        name="mixer",
    )(x.reshape(bsz * seq, D_MODEL), x.reshape(bsz * seq, D_MODEL), mod3, mod3, *consts)
    return out.reshape(bsz, seq, D_MODEL)


def _pad_lanes(v):
    return jnp.pad(v.reshape(1, -1), ((0, 0), (0, LANES - v.shape[0])))


def kernel(x, c, w_ada, b_ada, ffn1_norm, ffn1_w_gate, ffn1_w_up, ffn1_w_down, mix_norm, w_in, conv_w, conv_b, dt_bias, a_log, d_skip, ssd_norm_w, pool_w, pool_b, pool_scale, w_out, ffn2_norm, ffn2_w_gate, ffn2_w_up, ffn2_w_down, final_norm):
    bsz = x.shape[0]
    depth = w_ada.shape[0]
    c_pad = jnp.pad(c, ((0, SUBLANES - bsz), (0, 0)))
    row = lambda v: v.reshape(1, -1)
    for i in range(depth):
        mod = _adaln(c_pad, w_ada[i], row(b_ada[i]))
        mod3 = mod[:bsz].reshape(bsz, N_MOD, D_MODEL)
        last = i == depth - 1

        x = _ffn(x, mod3, row(ffn1_norm[i]), _pack_rows(ffn1_w_gate[i]), _pack_rows(ffn1_w_up[i]),
                 _pack_rows(ffn1_w_down[i]), row(final_norm), mod_idx=0, final=False)

        wi = w_in[i]
        o_xbc = D_SSD
        o_dt = o_xbc + D_XBC
        o_u = o_dt + N_HEADS
        wdt = jnp.pad(wi[:, o_dt:o_u], ((0, 0), (0, LANES - N_HEADS)))
        x = _mixer(
            x, mod3, row(mix_norm[i]),
            _pack_rows(wi[:, :o_xbc]), _pack_rows(wi[:, o_xbc:o_dt]), _pack_rows(wdt),
            _pack_rows(wi[:, o_u:]),
            conv_w[i], row(conv_b[i]), _pad_lanes(dt_bias[i]), _pad_lanes(a_log[i]),
            row(jnp.repeat(d_skip[i], HEAD_DIM)), row(ssd_norm_w[i]),
            _pack_rows(pool_w[i]), row(pool_b[i]), row(pool_scale[i]), _pack_rows(w_out[i]))

        x = _ffn(x, mod3, row(ffn2_norm[i]), _pack_rows(ffn2_w_gate[i]), _pack_rows(ffn2_w_up[i]),
                 _pack_rows(ffn2_w_down[i]), row(final_norm), mod_idx=6, final=last)
    return x
```

```python
import functools

import jax
import jax.numpy as jnp
from jax import lax
from jax.experimental import pallas as pl
from jax.experimental.pallas import tpu as pltpu

F32 = jnp.float32
BF16 = jnp.bfloat16

D_MODEL = 1024
D_FF = 2816
N_MOD = 9
FFN_RES = 0.5
EPS = 1e-6
D_SSD = 1024
HEAD_DIM = 64
N_HEADS = 16
N_GROUPS = 4
HEADS_PER_GROUP = N_HEADS // N_GROUPS
D_STATE = 128
D_CONV = 4
CHUNK = 128
D_POOL = 1024
POOL_WINDOWS = (2, 4, 8, 16)
POOL_GROUP_DIM = 256
D_XBC = D_SSD + 2 * N_GROUPS * D_STATE

LANES = 128
SUBLANES = 8
MXU_COLS = 256
VMEM_LIMIT_BYTES = 60 * 1024 * 1024

FFN_ROWS = 512
MIX_ROWS = 512
ADA_COLS = 1024
CONV_PAD = SUBLANES
POOL_PAD = 16
CONV_COLS = 512

assert HEAD_DIM * 2 == LANES and D_STATE == LANES and CHUNK == LANES


def _dot(a, b):
    return jnp.dot(a, b, preferred_element_type=F32)


def _split3(v):
    hi = v.astype(BF16)
    r1 = v - hi.astype(F32)
    mid = r1.astype(BF16)
    lo = (r1 - mid.astype(F32)).astype(BF16)
    return hi, mid, lo


def _silu(v):
    return v * jax.nn.sigmoid(v)


def _rms(v, w):
    ms = jnp.mean(v * v, axis=-1, keepdims=True)
    return v * lax.rsqrt(ms + EPS) * w


def _const_spec(shape):
    zeros = (0,) * len(shape)
    return pl.BlockSpec(shape, lambda *_: zeros, pipeline_mode=pl.Buffered(1))


def _adaln_kernel(c_ref, w_ref, b_ref, o_ref):
    c = c_ref[...]
    a_hi, a_mid, _ = _split3(_silu(c))
    w_hi, w_mid, _ = _split3(w_ref[...])
    acc = _dot(a_hi, w_hi) + _dot(a_mid, w_hi) + _dot(a_hi, w_mid)
    o_ref[...] = acc + b_ref[...]


def _adaln(c_pad, w_ada, b_ada):
    rows = c_pad.shape[0]
    n = w_ada.shape[1]
    return pl.pallas_call(
        _adaln_kernel,
        grid=(n // ADA_COLS,),
        in_specs=[
            pl.BlockSpec((rows, D_MODEL), lambda j: (0, 0)),
            pl.BlockSpec((D_MODEL, ADA_COLS), lambda j: (0, j)),
            pl.BlockSpec((1, ADA_COLS), lambda j: (0, j)),
        ],
        out_specs=pl.BlockSpec((rows, ADA_COLS), lambda j: (0, j)),
        out_shape=jax.ShapeDtypeStruct((rows, n), F32),
        compiler_params=pltpu.CompilerParams(
            dimension_semantics=("arbitrary",), vmem_limit_bytes=VMEM_LIMIT_BYTES),
        name="adaln",
    )(c_pad, w_ada, b_ada)


def _ffn_kernel(x_ref, mod_ref, nw_ref, wg_ref, wu_ref, wd_ref, fw_ref, o_ref, *, mod_idx, final):
    x = x_ref[...]
    sh = mod_ref[0, pl.ds(mod_idx, 1), :]
    sc = mod_ref[0, pl.ds(mod_idx + 1, 1), :]
    gt = mod_ref[0, pl.ds(mod_idx + 2, 1), :]
    h = (_rms(x, nw_ref[...]) * (1.0 + sc) + sh).astype(BF16)
    g = _dot(h, wg_ref[...])
    u = _dot(h, wu_ref[...])
    a = (_silu(g) * u).astype(BF16)
    o = _dot(a, wd_ref[...])
    out = x + (FFN_RES * gt) * o
    if final:
        out = _rms(out, fw_ref[...])
    o_ref[...] = out


def _ffn(x, mod3, norm_w, wg, wu, wd, final_w, *, mod_idx, final):
    bsz, seq, _ = x.shape
    nt = seq // FFN_ROWS
    row_spec = pl.BlockSpec((None, FFN_ROWS, D_MODEL), lambda b, i: (b, i, 0))
    return pl.pallas_call(
        functools.partial(_ffn_kernel, mod_idx=mod_idx, final=final),
        grid=(bsz, nt),
        in_specs=[
            row_spec,
            pl.BlockSpec((1, N_MOD, D_MODEL), lambda b, i: (b, 0, 0)),
            _const_spec((1, D_MODEL)),
            _const_spec((D_MODEL, D_FF)),
            _const_spec((D_MODEL, D_FF)),
            _const_spec((D_FF, D_MODEL)),
            _const_spec((1, D_MODEL)),
        ],
        out_specs=row_spec,
        out_shape=jax.ShapeDtypeStruct(x.shape, F32),
        compiler_params=pltpu.CompilerParams(
            dimension_semantics=("arbitrary", "arbitrary"), vmem_limit_bytes=VMEM_LIMIT_BYTES),
        name="ffn_final" if final else "ffn",
    )(x, mod3, norm_w, wg, wu, wd, final_w)


def _mixer_kernel(xc_ref, xn_ref, modc_ref, modn_ref, nw_ref, wz_in, wxbc_in, wdt_in, wu_in,
                  convw_ref, convb_ref, dtb_ref, alog_ref, dskip_ref, ssdnw_ref,
                  poolw_in, poolb_ref, pools_ref, wout_in,
                  o_ref,
                  wz_ref, wxbc_ref, wdt_ref, wu_ref, poolw_ref, wout_ref,
                  z_s, raw_s, dtr_s, u_s, xbc_s, hraw_s, hu_s, st_s, *, tiles_per_seq):
    step = pl.program_id(0)
    tile_in_seq = step % tiles_per_seq
    next_starts_seq = tile_in_seq == tiles_per_seq - 1
    n_chunks = MIX_ROWS // CHUNK

    def in_norm(x, mod_ref):
        sh = mod_ref[0, 3:4, :]
        sc = mod_ref[0, 4:5, :]
        return (_rms(x, nw_ref[...]) * (1.0 + sc) + sh).astype(BF16)

    @pl.when(step == 0)
    def _():
        wz_ref[...] = wz_in[...]
        wxbc_ref[...] = wxbc_in[...]
        wdt_ref[...] = wdt_in[...]
        wu_ref[...] = wu_in[...]
        poolw_ref[...] = poolw_in[...]
        wout_ref[...] = wout_in[...]
        h0 = in_norm(xc_ref[...], modc_ref)
        z_s[...] = _dot(h0, wz_ref[...])
        raw_s[...] = _dot(h0, wxbc_ref[...])
        dtr_s[...] = _dot(h0, wdt_ref[...])
        u_s[...] = _dot(h0, wu_ref[...])
        hraw_s[...] = jnp.zeros(hraw_s.shape, F32)
        hu_s[...] = jnp.zeros(hu_s.shape, F32)

    @pl.when(tile_in_seq == 0)
    def _():
        st_s[...] = jnp.zeros(st_s.shape, F32)

    gt = modc_ref[0, 5:6, :]
    a_neg = -jnp.exp(alog_ref[...])
    ri = lax.broadcasted_iota(jnp.int32, (CHUNK, CHUNK), 0)
    ci = lax.broadcasted_iota(jnp.int32, (CHUNK, CHUNK), 1)
    causal = ri >= ci
    tril = jnp.where(causal, 1.0, 0.0).astype(BF16)
    low_half = ci < HEAD_DIM
    low_half_row = lax.broadcasted_iota(jnp.int32, (1, LANES), 1) < HEAD_DIM

    def rows(c):
        return slice(c * CHUNK, (c + 1) * CHUNK)

    pending = []

    def emit(k=None):
        n = len(pending) if k is None else min(k, len(pending))
        for _ in range(n):
            pending.pop(0)()

    def proj_pieces(c, h_n):
        rs = rows(c)

        def piece(dst, w_ref, lo, hi):
            def run():
                dst[rs, lo:hi] = _dot(h_n, w_ref[:, lo:hi])
            return run

        out = [piece(raw_s, wxbc_ref, k, k + MXU_COLS) for k in range(0, D_XBC, MXU_COLS)]
        out += [piece(u_s, wu_ref, k, k + MXU_COLS) for k in range(0, D_POOL, MXU_COLS)]
        out += [piece(z_s, wz_ref, k, k + MXU_COLS) for k in range(0, D_SSD, MXU_COLS)]
        out.append(piece(dtr_s, wdt_ref, 0, LANES))
        return out

    def pre(c):
        rs = rows(c)
        last = c == n_chunks - 1
        for lo in range(0, D_XBC, CONV_COLS):
            hi = lo + CONV_COLS
            cw = convw_ref[:, lo:hi]
            cur = raw_s[rs, lo:hi]
            full = jnp.concatenate([hraw_s[:, lo:hi], cur], axis=0)
            xc = convb_ref[:, lo:hi] + cw[D_CONV - 1:D_CONV, :] * cur
            for k in range(D_CONV - 1):
                shifted = pltpu.roll(full, D_CONV - 1 - k, 0)[CONV_PAD:, :]
                xc = xc + cw[k:k + 1, :] * shifted
            hist = cur[CHUNK - CONV_PAD:, :]
            if last:
                hist = jnp.where(next_starts_seq, jnp.zeros_like(hist), hist)
            hraw_s[:, lo:hi] = hist
            xbc_s[rs, lo:hi] = _silu(xc)
            emit(2)
        u_c = u_s[rs, :]
        ext = jnp.concatenate([hu_s[...], u_c], axis=0)
        sums = []
        acc = ext
        for gi, w in enumerate(POOL_WINDOWS):
            acc = acc + pltpu.roll(acc, w // 2, 0)
            sums.append(acc[POOL_PAD:, 0:POOL_GROUP_DIM])
            if gi + 1 < len(POOL_WINDOWS):
                acc = acc[:, POOL_GROUP_DIM:]
        hist = u_c[CHUNK - POOL_PAD:, :]
        if last:
            hist = jnp.where(next_starts_seq, jnp.zeros_like(hist), hist)
        hu_s[...] = hist
        diffs = []
        for gi, w in enumerate(POOL_WINDOWS):
            u_g = u_c[:, gi * POOL_GROUP_DIM:(gi + 1) * POOL_GROUP_DIM]
            if c == 0:
                pos = tile_in_seq * MIX_ROWS + 1 + lax.broadcasted_iota(
                    jnp.int32, (CHUNK, POOL_GROUP_DIM), 0)
                pooled = sums[gi] / jnp.minimum(pos, w).astype(F32)
            else:
                pooled = sums[gi] * (1.0 / w)
            diffs.append((pooled - u_g).astype(BF16))
        emit(4)
        zs = _silu(z_s[rs, :])
        dt_c = jax.nn.softplus(dtr_s[rs, :] + dtb_ref[...])
        emit()
        return diffs, zs, dt_c

    def ssd(c, zs, dt_c):
        rs = rows(c)
        p_hi, p_mid, p_lo = _split3(dt_c * a_neg)
        cs = _dot(tril, p_hi) + _dot(tril, p_mid) + _dot(tril, p_lo)
        cs_t = cs.T
        dt_t = dt_c.T
        w_t = dt_t * jnp.exp(cs_t[:, CHUNK - 1:CHUNK] - cs_t)
        xs_c = xbc_s[rs, 0:D_SSD]
        emit(1)
        y_pairs = []
        for g in range(N_GROUPS):
            b0 = D_SSD + g * D_STATE
            c0 = D_SSD + N_GROUPS * D_STATE + g * D_STATE
            b_t = xbc_s[rs, b0:b0 + D_STATE].T
            c_g = xbc_s[rs, c0:c0 + D_STATE]
            cb = _dot(c_g.astype(BF16), b_t.astype(BF16))
            for pr in range(HEADS_PER_GROUP // 2):
                lhs, bw_t, cd = [], [], []
                for r in (2 * pr, 2 * pr + 1):
                    hd = g * HEADS_PER_GROUP + r
                    cs_bc = jnp.broadcast_to(cs[:, hd:hd + 1], (CHUNK, CHUNK))
                    decay = jnp.exp(jnp.where(causal, cs_bc - cs_t[hd:hd + 1, :], -jnp.inf))
                    scores = decay * (cb * dt_t[hd:hd + 1, :])
                    sd = jnp.exp(cs_bc)
                    lhs.append(jnp.concatenate([scores, c_g * sd], axis=1).astype(BF16))
                    bw_t.append((b_t * w_t[hd:hd + 1, :]).astype(BF16))
                    cd.append(sd[CHUNK - 1:CHUNK, :])
                col = (g * HEADS_PER_GROUP + 2 * pr) * HEAD_DIM
                x_p = xs_c[:, col:col + LANES]
                st = st_s[:, col:col + LANES]
                rhs = jnp.concatenate([x_p, st], axis=0).astype(BF16)
                y_pairs.append(jnp.where(low_half, _dot(lhs[0], rhs), _dot(lhs[1], rhs)))
                zero = jnp.zeros_like(x_p)
                x_split = jnp.concatenate(
                    [jnp.where(low_half, x_p, zero), jnp.where(low_half, zero, x_p)], axis=0)
                upd = _dot(jnp.concatenate(bw_t, axis=1), x_split.astype(BF16))
                st_s[:, col:col + LANES] = st * jnp.where(low_half_row, cd[0], cd[1]) + upd
                emit(1)
        y = jnp.concatenate(y_pairs, axis=1) + dskip_ref[...] * xs_c
        yz = y * zs
        gw = D_SSD // N_GROUPS
        parts = []
        for g in range(N_GROUPS):
            blk = yz[:, g * gw:(g + 1) * gw]
            ms = jnp.mean(blk * blk, axis=-1, keepdims=True)
            parts.append(blk * lax.rsqrt(ms + EPS))
        return (jnp.concatenate(parts, axis=1) * ssdnw_ref[...]).astype(BF16)

    def post_pieces(c, y_ssd, diffs):
        rs = rows(c)
        outs = [None] * len(POOL_WINDOWS)

        def pool_piece(gi):
            def run():
                lo, hi = gi * POOL_GROUP_DIM, (gi + 1) * POOL_GROUP_DIM
                o = (_dot(diffs[gi], poolw_ref[gi]) + poolb_ref[:, lo:hi]) * pools_ref[:, lo:hi]
                outs[gi] = o.astype(BF16)
            return run

        def out_piece(lo, hi):
            def run():
                y_pool = jnp.concatenate(outs, axis=1)
                out = (_dot(y_ssd, wout_ref[0:D_SSD, lo:hi])
                       + _dot(y_pool, wout_ref[D_SSD:D_SSD + D_POOL, lo:hi]))
                o_ref[rs, lo:hi] = xc_ref[rs, lo:hi] + gt[:, lo:hi] * out
            return run

        return ([pool_piece(gi) for gi in range(len(POOL_WINDOWS))]
                + [out_piece(k, k + MXU_COLS) for k in range(0, D_MODEL, MXU_COLS)])

    h_n = in_norm(xn_ref[rows(0), :], modn_ref)
    diffs, zs, dt_c = pre(0)
    pending.extend(proj_pieces(0, h_n))
    for c in range(n_chunks):
        y_ssd = ssd(c, zs, dt_c)
        pending.extend(post_pieces(c, y_ssd, diffs))
        if c + 1 < n_chunks:
            h_n = in_norm(xn_ref[rows(c + 1), :], modn_ref)
            emit(2)
            diffs, zs, dt_c = pre(c + 1)
            pending.extend(proj_pieces(c + 1, h_n))
        else:
            emit()


def _mixer(x, mod3, norm_w, wz, wxbc, wdt, wu, convw, convb, dtb, alog, dskip, ssdnw,
           poolw, poolb, pools, wout):
    bsz, seq, _ = x.shape
    tiles_per_seq = seq // MIX_ROWS
    steps = bsz * tiles_per_seq
    nxt = lambda s: jnp.minimum(s + 1, steps - 1)
    consts = [norm_w, wz, wxbc, wdt, wu, convw, convb, dtb, alog, dskip, ssdnw,
              poolw, poolb, pools, wout]
    x2 = x.reshape(bsz * seq, D_MODEL)
    out = pl.pallas_call(
        functools.partial(_mixer_kernel, tiles_per_seq=tiles_per_seq),
        grid=(steps,),
        in_specs=[
            pl.BlockSpec((MIX_ROWS, D_MODEL), lambda s: (s, 0)),
            pl.BlockSpec((MIX_ROWS, D_MODEL), lambda s: (nxt(s), 0)),
            pl.BlockSpec((1, N_MOD, D_MODEL), lambda s: (s // tiles_per_seq, 0, 0)),
            pl.BlockSpec((1, N_MOD, D_MODEL), lambda s: (nxt(s) // tiles_per_seq, 0, 0)),
        ] + [_const_spec(a.shape) for a in consts],
        out_specs=pl.BlockSpec((MIX_ROWS, D_MODEL), lambda s: (s, 0)),
        out_shape=jax.ShapeDtypeStruct((bsz * seq, D_MODEL), F32),
        scratch_shapes=[pltpu.VMEM(w.shape, BF16) for w in (wz, wxbc, wdt, wu, poolw, wout)] + [
            pltpu.VMEM((MIX_ROWS, D_SSD), F32),
            pltpu.VMEM((MIX_ROWS, D_XBC), F32),
            pltpu.VMEM((MIX_ROWS, LANES), F32),
            pltpu.VMEM((MIX_ROWS, D_POOL), F32),
            pltpu.VMEM((MIX_ROWS, D_XBC), F32),
            pltpu.VMEM((CONV_PAD, D_XBC), F32),
            pltpu.VMEM((POOL_PAD, D_POOL), F32),
            pltpu.VMEM((D_STATE, D_SSD), F32),
        ],
        compiler_params=pltpu.CompilerParams(
            dimension_semantics=("arbitrary",), vmem_limit_bytes=VMEM_LIMIT_BYTES),
        name="mixer",
    )(x2, x2, mod3, mod3, *consts)
    return out.reshape(bsz, seq, D_MODEL)


def _pad_lanes(v):
    return jnp.pad(v.reshape(1, -1), ((0, 0), (0, LANES - v.shape[0])))


def kernel(x, c, w_ada, b_ada, ffn1_norm, ffn1_w_gate, ffn1_w_up, ffn1_w_down, mix_norm, w_in, conv_w, conv_b, dt_bias, a_log, d_skip, ssd_norm_w, pool_w, pool_b, pool_scale, w_out, ffn2_norm, ffn2_w_gate, ffn2_w_up, ffn2_w_down, final_norm):
    bsz = x.shape[0]
    depth = w_ada.shape[0]
    c_pad = jnp.pad(c, ((0, SUBLANES - bsz), (0, 0)))
    row = lambda v: v.reshape(1, -1)
    for i in range(depth):
        mod = _adaln(c_pad, w_ada[i], row(b_ada[i]))
        mod3 = mod[:bsz].reshape(bsz, N_MOD, D_MODEL)
        last = i == depth - 1

        x = _ffn(x, mod3, row(ffn1_norm[i]), ffn1_w_gate[i].astype(BF16), ffn1_w_up[i].astype(BF16),
                 ffn1_w_down[i].astype(BF16), row(final_norm), mod_idx=0, final=False)

        wi = w_in[i]
        o_xbc = D_SSD
        o_dt = o_xbc + D_XBC
        o_u = o_dt + N_HEADS
        wdt = jnp.pad(wi[:, o_dt:o_u], ((0, 0), (0, LANES - N_HEADS)))
        x = _mixer(
            x, mod3, row(mix_norm[i]),
            wi[:, :o_xbc].astype(BF16), wi[:, o_xbc:o_dt].astype(BF16), wdt.astype(BF16),
            wi[:, o_u:].astype(BF16),
            conv_w[i], row(conv_b[i]), _pad_lanes(dt_bias[i]), _pad_lanes(a_log[i]),
            row(jnp.repeat(d_skip[i], HEAD_DIM)), row(ssd_norm_w[i]),
            pool_w[i].astype(BF16), row(pool_b[i]), row(pool_scale[i]), w_out[i].astype(BF16))

        x = _ffn(x, mod3, row(ffn2_norm[i]), ffn2_w_gate[i].astype(BF16), ffn2_w_up[i].astype(BF16),
                 ffn2_w_down[i].astype(BF16), row(final_norm), mod_idx=6, final=last)
    return x
```

```python
import functools

import jax
import jax.numpy as jnp
from jax import lax
from jax.experimental import pallas as pl
from jax.experimental.pallas import tpu as pltpu

F32 = jnp.float32
BF16 = jnp.bfloat16

D_MODEL = 1024
D_FF = 2816
N_MOD = 9
FFN_RES = 0.5
EPS = 1e-6
D_SSD = 1024
HEAD_DIM = 64
N_HEADS = 16
N_GROUPS = 4
HEADS_PER_GROUP = N_HEADS // N_GROUPS
D_STATE = 128
D_CONV = 4
CHUNK = 128
D_POOL = 1024
POOL_WINDOWS = (2, 4, 8, 16)
POOL_GROUP_DIM = 256
D_XBC = D_SSD + 2 * N_GROUPS * D_STATE

LANES = 128
SUBLANES = 8
MXU_COLS = 256
VMEM_LIMIT_BYTES = 60 * 1024 * 1024

FFN_ROWS = 512
MIX_ROWS = 512
ADA_COLS = 1024
CONV_PAD = SUBLANES
POOL_PAD = 16

assert HEAD_DIM * 2 == LANES and D_STATE == LANES and CHUNK == LANES


def _dot(a, b):
    return jnp.dot(a, b, preferred_element_type=F32)


def _split3(v):
    hi = v.astype(BF16)
    r1 = v - hi.astype(F32)
    mid = r1.astype(BF16)
    lo = (r1 - mid.astype(F32)).astype(BF16)
    return hi, mid, lo


def _silu(v):
    return v * jax.nn.sigmoid(v)


def _rms(v, w):
    ms = jnp.mean(v * v, axis=-1, keepdims=True)
    return v * lax.rsqrt(ms + EPS) * w


def _const_spec(shape):
    zeros = (0,) * len(shape)
    return pl.BlockSpec(shape, lambda *_: zeros, pipeline_mode=pl.Buffered(1))


def _adaln_kernel(c_ref, w_ref, b_ref, o_ref):
    c = c_ref[...]
    a_hi, a_mid, _ = _split3(_silu(c))
    w_hi, w_mid, _ = _split3(w_ref[...])
    acc = _dot(a_hi, w_hi) + _dot(a_mid, w_hi) + _dot(a_hi, w_mid)
    o_ref[...] = acc + b_ref[...]


def _adaln(c_pad, w_ada, b_ada):
    rows = c_pad.shape[0]
    n = w_ada.shape[1]
    return pl.pallas_call(
        _adaln_kernel,
        grid=(n // ADA_COLS,),
        in_specs=[
            pl.BlockSpec((rows, D_MODEL), lambda j: (0, 0)),
            pl.BlockSpec((D_MODEL, ADA_COLS), lambda j: (0, j)),
            pl.BlockSpec((1, ADA_COLS), lambda j: (0, j)),
        ],
        out_specs=pl.BlockSpec((rows, ADA_COLS), lambda j: (0, j)),
        out_shape=jax.ShapeDtypeStruct((rows, n), F32),
        compiler_params=pltpu.CompilerParams(
            dimension_semantics=("arbitrary",), vmem_limit_bytes=VMEM_LIMIT_BYTES),
        name="adaln",
    )(c_pad, w_ada, b_ada)


def _ffn_kernel(x_ref, mod_ref, nw_ref, wg_ref, wu_ref, wd_ref, fw_ref, o_ref, *, mod_idx, final):
    x = x_ref[...]
    sh = mod_ref[0, pl.ds(mod_idx, 1), :]
    sc = mod_ref[0, pl.ds(mod_idx + 1, 1), :]
    gt = mod_ref[0, pl.ds(mod_idx + 2, 1), :]
    h = (_rms(x, nw_ref[...]) * (1.0 + sc) + sh).astype(BF16)
    g = _dot(h, wg_ref[...])
    u = _dot(h, wu_ref[...])
    a = (_silu(g) * u).astype(BF16)
    o = _dot(a, wd_ref[...])
    out = x + (FFN_RES * gt) * o
    if final:
        out = _rms(out, fw_ref[...])
    o_ref[...] = out


def _ffn(x, mod3, norm_w, wg, wu, wd, final_w, *, mod_idx, final):
    bsz, seq, _ = x.shape
    nt = seq // FFN_ROWS
    row_spec = pl.BlockSpec((None, FFN_ROWS, D_MODEL), lambda b, i: (b, i, 0))
    return pl.pallas_call(
        functools.partial(_ffn_kernel, mod_idx=mod_idx, final=final),
        grid=(bsz, nt),
        in_specs=[
            row_spec,
            pl.BlockSpec((1, N_MOD, D_MODEL), lambda b, i: (b, 0, 0)),
            _const_spec((1, D_MODEL)),
            _const_spec((D_MODEL, D_FF)),
            _const_spec((D_MODEL, D_FF)),
            _const_spec((D_FF, D_MODEL)),
            _const_spec((1, D_MODEL)),
        ],
        out_specs=row_spec,
        out_shape=jax.ShapeDtypeStruct(x.shape, F32),
        compiler_params=pltpu.CompilerParams(
            dimension_semantics=("arbitrary", "arbitrary"), vmem_limit_bytes=VMEM_LIMIT_BYTES),
        name="ffn_final" if final else "ffn",
    )(x, mod3, norm_w, wg, wu, wd, final_w)


def _mixer_kernel(xc_ref, xn_ref, modc_ref, modn_ref, nw_ref, wz_in, wxbc_in, wdt_in, wu_in,
                  convw_ref, convb_ref, dtb_ref, alog_ref, dskip_ref, ssdnw_ref,
                  poolw_in, poolb_ref, pools_ref, wout_in,
                  o_ref,
                  wz_ref, wxbc_ref, wdt_ref, wu_ref, poolw_ref, wout_ref,
                  z_s, raw_s, dtr_s, u_s, xbc_s, st_s, *, tiles_per_seq):
    step = pl.program_id(0)
    tile_in_seq = step % tiles_per_seq
    next_starts_seq = tile_in_seq == tiles_per_seq - 1
    n_chunks = MIX_ROWS // CHUNK

    def in_norm(x, mod_ref):
        sh = mod_ref[0, 3:4, :]
        sc = mod_ref[0, 4:5, :]
        return (_rms(x, nw_ref[...]) * (1.0 + sc) + sh).astype(BF16)

    def rows(c):
        return slice(c * CHUNK, (c + 1) * CHUNK)

    def conv_base(c):
        return (c % n_chunks) * (CONV_PAD + CHUNK) + CONV_PAD

    def pool_base(c):
        return (c % n_chunks) * (POOL_PAD + CHUNK) + POOL_PAD

    pending = []

    def emit(k=None):
        n = len(pending) if k is None else min(k, len(pending))
        for _ in range(n):
            pending.pop(0)()

    def proj_pieces(c, h_n):
        rs = rows(c)

        def dense(dst, w_ref, lo, hi):
            def run():
                dst[rs, lo:hi] = _dot(h_n, w_ref[:, lo:hi])
            return run

        def slabs(dst, base, w_ref, lo, hi):
            def run():
                res = _dot(h_n, w_ref[:, lo:hi])
                for j in range(lo // LANES, hi // LANES):
                    dst[j, base:base + CHUNK, :] = res[:, j * LANES - lo:(j + 1) * LANES - lo]
            return run

        out = [slabs(raw_s, conv_base(c), wxbc_ref, k, k + MXU_COLS) for k in range(0, D_XBC, MXU_COLS)]
        out += [slabs(u_s, pool_base(c), wu_ref, k, k + MXU_COLS) for k in range(0, D_POOL, MXU_COLS)]
        out += [dense(z_s, wz_ref, k, k + MXU_COLS) for k in range(0, D_SSD, MXU_COLS)]
        out.append(dense(dtr_s, wdt_ref, 0, LANES))
        return out

    @pl.when(step == 0)
    def _():
        wz_ref[...] = wz_in[...]
        wxbc_ref[...] = wxbc_in[...]
        wdt_ref[...] = wdt_in[...]
        wu_ref[...] = wu_in[...]
        poolw_ref[...] = poolw_in[...]
        wout_ref[...] = wout_in[...]
        raw_s[:, 0:CONV_PAD, :] = jnp.zeros((D_XBC // LANES, CONV_PAD, LANES), F32)
        u_s[:, 0:POOL_PAD, :] = jnp.zeros((D_POOL // LANES, POOL_PAD, LANES), F32)
        for c in range(n_chunks):
            for run in proj_pieces(c, in_norm(xc_ref[rows(c), :], modc_ref)):
                run()

    @pl.when(tile_in_seq == 0)
    def _():
        st_s[...] = jnp.zeros(st_s.shape, F32)

    gt = modc_ref[0, 5:6, :]
    a_neg = -jnp.exp(alog_ref[...])
    ri = lax.broadcasted_iota(jnp.int32, (CHUNK, CHUNK), 0)
    ci = lax.broadcasted_iota(jnp.int32, (CHUNK, CHUNK), 1)
    causal = ri >= ci
    tril = jnp.where(causal, 1.0, 0.0).astype(BF16)
    low_half = ci < HEAD_DIM
    low_half_row = lax.broadcasted_iota(jnp.int32, (1, LANES), 1) < HEAD_DIM

    def pre(c):
        rs = rows(c)
        last = c == n_chunks - 1

        def carry(hist):
            return jnp.where(next_starts_seq, jnp.zeros_like(hist), hist) if last else hist

        base, nbase = conv_base(c), conv_base(c + 1)
        for j in range(D_XBC // LANES):
            cols = slice(j * LANES, (j + 1) * LANES)
            xc = convb_ref[:, cols]
            for k in range(D_CONV):
                xc = xc + convw_ref[k:k + 1, cols] * raw_s[j, pl.ds(base - (D_CONV - 1) + k, CHUNK), :]
            raw_s[j, nbase - CONV_PAD:nbase, :] = carry(raw_s[j, base + CHUNK - CONV_PAD:base + CHUNK, :])
            xbc_s[rs, cols] = _silu(xc)
            if j % 2 == 1:
                emit(1)
        base, nbase = pool_base(c), pool_base(c + 1)
        diffs = []
        for gi, w in enumerate(POOL_WINDOWS):
            halves = []
            for j in range(gi * POOL_GROUP_DIM // LANES, (gi + 1) * POOL_GROUP_DIM // LANES):
                u_g = u_s[j, base:base + CHUNK, :]
                s = u_g
                for k in range(1, w):
                    s = s + u_s[j, pl.ds(base - k, CHUNK), :]
                if c == 0:
                    pos = tile_in_seq * MIX_ROWS + 1 + lax.broadcasted_iota(jnp.int32, (CHUNK, LANES), 0)
                    pooled = s / jnp.minimum(pos, w).astype(F32)
                else:
                    pooled = s * (1.0 / w)
                halves.append(pooled - u_g)
                u_s[j, nbase - POOL_PAD:nbase, :] = carry(u_s[j, base + CHUNK - POOL_PAD:base + CHUNK, :])
            diffs.append(jnp.concatenate(halves, axis=1).astype(BF16))
            emit(1)
        zs = _silu(z_s[rs, :])
        dt_c = jax.nn.softplus(dtr_s[rs, :] + dtb_ref[...])
        emit()
        return diffs, zs, dt_c

    def ssd(c, zs, dt_c):
        rs = rows(c)
        p_hi, p_mid, p_lo = _split3(dt_c * a_neg)
        cs = _dot(tril, p_hi) + _dot(tril, p_mid) + _dot(tril, p_lo)
        cs_t = cs.T
        dt_t = dt_c.T
        w_t = dt_t * jnp.exp(cs_t[:, CHUNK - 1:CHUNK] - cs_t)
        xs_c = xbc_s[rs, 0:D_SSD]
        emit(1)
        y_pairs = []
        for g in range(N_GROUPS):
            b0 = D_SSD + g * D_STATE
            c0 = D_SSD + N_GROUPS * D_STATE + g * D_STATE
            b_t = xbc_s[rs, b0:b0 + D_STATE].T
            c_g = xbc_s[rs, c0:c0 + D_STATE]
            cb = _dot(c_g.astype(BF16), b_t.astype(BF16))
            for pr in range(HEADS_PER_GROUP // 2):
                lhs, bw_t, cd = [], [], []
                for r in (2 * pr, 2 * pr + 1):
                    hd = g * HEADS_PER_GROUP + r
                    cs_bc = jnp.broadcast_to(cs[:, hd:hd + 1], (CHUNK, CHUNK))
                    decay = jnp.exp(jnp.where(causal, cs_bc - cs_t[hd:hd + 1, :], -jnp.inf))
                    scores = decay * (cb * dt_t[hd:hd + 1, :])
                    sd = jnp.exp(cs_bc)
                    lhs.append(jnp.concatenate([scores, c_g * sd], axis=1).astype(BF16))
                    bw_t.append((b_t * w_t[hd:hd + 1, :]).astype(BF16))
                    cd.append(sd[CHUNK - 1:CHUNK, :])
                col = (g * HEADS_PER_GROUP + 2 * pr) * HEAD_DIM
                x_p = xs_c[:, col:col + LANES]
                st = st_s[:, col:col + LANES]
                rhs = jnp.concatenate([x_p, st], axis=0).astype(BF16)
                y_pairs.append(jnp.where(low_half, _dot(lhs[0], rhs), _dot(lhs[1], rhs)))
                zero = jnp.zeros_like(x_p)
                x_split = jnp.concatenate(
                    [jnp.where(low_half, x_p, zero), jnp.where(low_half, zero, x_p)], axis=0)
                upd = _dot(jnp.concatenate(bw_t, axis=1), x_split.astype(BF16))
                st_s[:, col:col + LANES] = st * jnp.where(low_half_row, cd[0], cd[1]) + upd
                emit(1)
        y = jnp.concatenate(y_pairs, axis=1) + dskip_ref[...] * xs_c
        yz = y * zs
        gw = D_SSD // N_GROUPS
        parts = []
        for g in range(N_GROUPS):
            blk = yz[:, g * gw:(g + 1) * gw]
            ms = jnp.mean(blk * blk, axis=-1, keepdims=True)
            parts.append(blk * lax.rsqrt(ms + EPS))
        return (jnp.concatenate(parts, axis=1) * ssdnw_ref[...]).astype(BF16)

    def post_pieces(c, y_ssd, diffs):
        rs = rows(c)
        outs = [None] * len(POOL_WINDOWS)

        def pool_piece(gi):
            def run():
                lo, hi = gi * POOL_GROUP_DIM, (gi + 1) * POOL_GROUP_DIM
                o = (_dot(diffs[gi], poolw_ref[gi]) + poolb_ref[:, lo:hi]) * pools_ref[:, lo:hi]
                outs[gi] = o.astype(BF16)
            return run

        def out_piece(lo, hi):
            def run():
                y_pool = jnp.concatenate(outs, axis=1)
                out = (_dot(y_ssd, wout_ref[0:D_SSD, lo:hi])
                       + _dot(y_pool, wout_ref[D_SSD:D_SSD + D_POOL, lo:hi]))
                o_ref[rs, lo:hi] = xc_ref[rs, lo:hi] + gt[:, lo:hi] * out
            return run

        return ([pool_piece(gi) for gi in range(len(POOL_WINDOWS))]
                + [out_piece(k, k + MXU_COLS) for k in range(0, D_MODEL, MXU_COLS)])

    h_n = in_norm(xn_ref[rows(0), :], modn_ref)
    diffs, zs, dt_c = pre(0)
    pending.extend(proj_pieces(0, h_n))
    for c in range(n_chunks):
        y_ssd = ssd(c, zs, dt_c)
        pending.extend(post_pieces(c, y_ssd, diffs))
        if c + 1 < n_chunks:
            h_n = in_norm(xn_ref[rows(c + 1), :], modn_ref)
            emit(2)
            diffs, zs, dt_c = pre(c + 1)
            pending.extend(proj_pieces(c + 1, h_n))
        else:
            emit()


def _mixer(x, mod3, norm_w, wz, wxbc, wdt, wu, convw, convb, dtb, alog, dskip, ssdnw,
           poolw, poolb, pools, wout):
    bsz, seq, _ = x.shape
    tiles_per_seq = seq // MIX_ROWS
    n_chunks = MIX_ROWS // CHUNK
    steps = bsz * tiles_per_seq
    nxt = lambda s: jnp.minimum(s + 1, steps - 1)
    consts = [norm_w, wz, wxbc, wdt, wu, convw, convb, dtb, alog, dskip, ssdnw,
              poolw, poolb, pools, wout]
    x2 = x.reshape(bsz * seq, D_MODEL)
    out = pl.pallas_call(
        functools.partial(_mixer_kernel, tiles_per_seq=tiles_per_seq),
        grid=(steps,),
        in_specs=[
            pl.BlockSpec((MIX_ROWS, D_MODEL), lambda s: (s, 0)),
            pl.BlockSpec((MIX_ROWS, D_MODEL), lambda s: (nxt(s), 0)),
            pl.BlockSpec((1, N_MOD, D_MODEL), lambda s: (s // tiles_per_seq, 0, 0)),
            pl.BlockSpec((1, N_MOD, D_MODEL), lambda s: (nxt(s) // tiles_per_seq, 0, 0)),
        ] + [_const_spec(a.shape) for a in consts],
        out_specs=pl.BlockSpec((MIX_ROWS, D_MODEL), lambda s: (s, 0)),
        out_shape=jax.ShapeDtypeStruct((bsz * seq, D_MODEL), F32),
        scratch_shapes=[pltpu.VMEM(w.shape, BF16) for w in (wz, wxbc, wdt, wu, poolw, wout)] + [
            pltpu.VMEM((MIX_ROWS, D_SSD), F32),
            pltpu.VMEM((D_XBC // LANES, n_chunks * (CONV_PAD + CHUNK), LANES), F32),
            pltpu.VMEM((MIX_ROWS, LANES), F32),
            pltpu.VMEM((D_POOL // LANES, n_chunks * (POOL_PAD + CHUNK), LANES), F32),
            pltpu.VMEM((MIX_ROWS, D_XBC), F32),
            pltpu.VMEM((D_STATE, D_SSD), F32),
        ],
        compiler_params=pltpu.CompilerParams(
            dimension_semantics=("arbitrary",), vmem_limit_bytes=VMEM_LIMIT_BYTES),
        name="mixer",
    )(x2, x2, mod3, mod3, *consts)
    return out.reshape(bsz, seq, D_MODEL)


def _pad_lanes(v):
    return jnp.pad(v.reshape(1, -1), ((0, 0), (0, LANES - v.shape[0])))


def kernel(x, c, w_ada, b_ada, ffn1_norm, ffn1_w_gate, ffn1_w_up, ffn1_w_down, mix_norm, w_in, conv_w, conv_b, dt_bias, a_log, d_skip, ssd_norm_w, pool_w, pool_b, pool_scale, w_out, ffn2_norm, ffn2_w_gate, ffn2_w_up, ffn2_w_down, final_norm):
    bsz = x.shape[0]
    depth = w_ada.shape[0]
    c_pad = jnp.pad(c, ((0, SUBLANES - bsz), (0, 0)))
    row = lambda v: v.reshape(1, -1)
    for i in range(depth):
        mod = _adaln(c_pad, w_ada[i], row(b_ada[i]))
        mod3 = mod[:bsz].reshape(bsz, N_MOD, D_MODEL)
        last = i == depth - 1

        x = _ffn(x, mod3, row(ffn1_norm[i]), ffn1_w_gate[i].astype(BF16), ffn1_w_up[i].astype(BF16),
                 ffn1_w_down[i].astype(BF16), row(final_norm), mod_idx=0, final=False)

        wi = w_in[i]
        o_xbc = D_SSD
        o_dt = o_xbc + D_XBC
        o_u = o_dt + N_HEADS
        wdt = jnp.pad(wi[:, o_dt:o_u], ((0, 0), (0, LANES - N_HEADS)))
        x = _mixer(
            x, mod3, row(mix_norm[i]),
            wi[:, :o_xbc].astype(BF16), wi[:, o_xbc:o_dt].astype(BF16), wdt.astype(BF16),
            wi[:, o_u:].astype(BF16),
            conv_w[i], row(conv_b[i]), _pad_lanes(dt_bias[i]), _pad_lanes(a_log[i]),
            row(jnp.repeat(d_skip[i], HEAD_DIM)), row(ssd_norm_w[i]),
            pool_w[i].astype(BF16), row(pool_b[i]), row(pool_scale[i]), w_out[i].astype(BF16))

        x = _ffn(x, mod3, row(ffn2_norm[i]), ffn2_w_gate[i].astype(BF16), ffn2_w_up[i].astype(BF16),
                 ffn2_w_down[i].astype(BF16), row(final_norm), mod_idx=6, final=last)
    return x
```

```python
import functools

import jax
import jax.numpy as jnp
from jax import lax
from jax.experimental import pallas as pl
from jax.experimental.pallas import tpu as pltpu

F32 = jnp.float32
BF16 = jnp.bfloat16

D_MODEL = 1024
D_FF = 2816
N_MOD = 9
FFN_RES = 0.5
EPS = 1e-6
D_SSD = 1024
HEAD_DIM = 64
N_HEADS = 16
N_GROUPS = 4
HEADS_PER_GROUP = N_HEADS // N_GROUPS
D_STATE = 128
D_CONV = 4
CHUNK = 128
D_POOL = 1024
POOL_WINDOWS = (2, 4, 8, 16)
POOL_GROUP_DIM = 256
D_XBC = D_SSD + 2 * N_GROUPS * D_STATE

LANES = 128
SUBLANES = 8
MXU_COLS = 256
VMEM_LIMIT_BYTES = 60 * 1024 * 1024

FFN_ROWS = 512
NORM_SLICES = 8
MIX_ROWS = 512
ADA_COLS = 1024
CONV_PAD = SUBLANES
POOL_PAD = 16

assert HEAD_DIM * 2 == LANES and D_STATE == LANES and CHUNK == LANES


def _dot(a, b):
    return jnp.dot(a, b, preferred_element_type=F32)


def _split3(v):
    hi = v.astype(BF16)
    r1 = v - hi.astype(F32)
    mid = r1.astype(BF16)
    lo = (r1 - mid.astype(F32)).astype(BF16)
    return hi, mid, lo


def _silu(v):
    return v * jax.nn.sigmoid(v)


def _rms(v, w):
    ms = jnp.mean(v * v, axis=-1, keepdims=True)
    return v * lax.rsqrt(ms + EPS) * w


def _const_spec(shape):
    zeros = (0,) * len(shape)
    return pl.BlockSpec(shape, lambda *_: zeros, pipeline_mode=pl.Buffered(1))


def _adaln_kernel(c_ref, w_ref, b_ref, o_ref):
    c = c_ref[...]
    a_hi, a_mid, _ = _split3(_silu(c))
    w_hi, w_mid, _ = _split3(w_ref[...])
    acc = _dot(a_hi, w_hi) + _dot(a_mid, w_hi) + _dot(a_hi, w_mid)
    o_ref[...] = acc + b_ref[...]


def _adaln(c_pad, w_ada, b_ada):
    rows = c_pad.shape[0]
    n = w_ada.shape[1]
    return pl.pallas_call(
        _adaln_kernel,
        grid=(n // ADA_COLS,),
        in_specs=[
            pl.BlockSpec((rows, D_MODEL), lambda j: (0, 0)),
            pl.BlockSpec((D_MODEL, ADA_COLS), lambda j: (0, j)),
            pl.BlockSpec((1, ADA_COLS), lambda j: (0, j)),
        ],
        out_specs=pl.BlockSpec((rows, ADA_COLS), lambda j: (0, j)),
        out_shape=jax.ShapeDtypeStruct((rows, n), F32),
        compiler_params=pltpu.CompilerParams(
            dimension_semantics=("arbitrary",), vmem_limit_bytes=VMEM_LIMIT_BYTES),
        name="adaln",
    )(c_pad, w_ada, b_ada)


def _ffn_kernel(xc_ref, xn_ref, modc_ref, modn_ref, nw_ref, wg_ref, wu_ref, wd_ref, fw_ref, o_ref,
                h_s, *, mod_idx, final):
    def in_norm(x, mod_ref):
        sh = mod_ref[0, pl.ds(mod_idx, 1), :]
        sc = mod_ref[0, pl.ds(mod_idx + 1, 1), :]
        return (_rms(x, nw_ref[...]) * (1.0 + sc) + sh).astype(BF16)

    step = pl.program_id(0)
    cur = h_s.at[step % 2]
    nxt = h_s.at[(step + 1) % 2]

    @pl.when(step == 0)
    def _():
        cur[...] = in_norm(xc_ref[...], modc_ref)

    n_chunks = D_FF // MXU_COLS
    slice_rows = FFN_ROWS // NORM_SLICES
    acts = []
    for k in range(n_chunks):
        if k < NORM_SLICES:
            rs = slice(k * slice_rows, (k + 1) * slice_rows)
            nxt[rs, :] = in_norm(xn_ref[rs, :], modn_ref)
        cols = slice(k * MXU_COLS, (k + 1) * MXU_COLS)
        h = cur[...]
        g = _dot(h, wg_ref[:, cols])
        u = _dot(h, wu_ref[:, cols])
        acts.append((_silu(g) * u).astype(BF16))
    a = jnp.concatenate(acts, axis=1)
    o = _dot(a, wd_ref[...])
    gt = modc_ref[0, pl.ds(mod_idx + 2, 1), :]
    out = xc_ref[...] + (FFN_RES * gt) * o
    if final:
        out = _rms(out, fw_ref[...])
    o_ref[...] = out


def _ffn(x, mod3, norm_w, wg, wu, wd, final_w, *, mod_idx, final):
    bsz, seq, _ = x.shape
    tiles_per_seq = seq // FFN_ROWS
    steps = bsz * tiles_per_seq
    nxt = lambda s: jnp.minimum(s + 1, steps - 1)
    x2 = x.reshape(bsz * seq, D_MODEL)
    out = pl.pallas_call(
        functools.partial(_ffn_kernel, mod_idx=mod_idx, final=final),
        grid=(steps,),
        in_specs=[
            pl.BlockSpec((FFN_ROWS, D_MODEL), lambda s: (s, 0)),
            pl.BlockSpec((FFN_ROWS, D_MODEL), lambda s: (nxt(s), 0)),
            pl.BlockSpec((1, N_MOD, D_MODEL), lambda s: (s // tiles_per_seq, 0, 0)),
            pl.BlockSpec((1, N_MOD, D_MODEL), lambda s: (nxt(s) // tiles_per_seq, 0, 0)),
            _const_spec((1, D_MODEL)),
            _const_spec((D_MODEL, D_FF)),
            _const_spec((D_MODEL, D_FF)),
            _const_spec((D_FF, D_MODEL)),
            _const_spec((1, D_MODEL)),
        ],
        out_specs=pl.BlockSpec((FFN_ROWS, D_MODEL), lambda s: (s, 0)),
        out_shape=jax.ShapeDtypeStruct((bsz * seq, D_MODEL), F32),
        scratch_shapes=[pltpu.VMEM((2, FFN_ROWS, D_MODEL), BF16)],
        compiler_params=pltpu.CompilerParams(
            dimension_semantics=("arbitrary",), vmem_limit_bytes=VMEM_LIMIT_BYTES),
        name="ffn_final" if final else "ffn",
    )(x2, x2, mod3, mod3, norm_w, wg, wu, wd, final_w)
    return out.reshape(bsz, seq, D_MODEL)


def _mixer_kernel(xc_ref, xn_ref, modc_ref, modn_ref, nw_ref, wz_in, wxbc_in, wdt_in, wu_in,
                  convw_ref, convb_ref, dtb_ref, alog_ref, dskip_ref, ssdnw_ref,
                  poolw_in, poolb_ref, pools_ref, wout_in,
                  o_ref,
                  wz_ref, wxbc_ref, wdt_ref, wu_ref, poolw_ref, wout_ref,
                  z_s, raw_s, dtr_s, u_s, xbc_s, st_s, diff_s, zs_s, dtc_s, *, tiles_per_seq):
    step = pl.program_id(0)
    tile_in_seq = step % tiles_per_seq
    next_starts_seq = tile_in_seq == tiles_per_seq - 1
    n_chunks = MIX_ROWS // CHUNK

    def in_norm(x, mod_ref):
        sh = mod_ref[0, 3:4, :]
        sc = mod_ref[0, 4:5, :]
        return (_rms(x, nw_ref[...]) * (1.0 + sc) + sh).astype(BF16)

    def rows(c):
        return slice(c * CHUNK, (c + 1) * CHUNK)

    def conv_base(c):
        return (c % n_chunks) * (CONV_PAD + CHUNK) + CONV_PAD

    def pool_base(c):
        return (c % n_chunks) * (POOL_PAD + CHUNK) + POOL_PAD

    pending = []

    def emit(k=None):
        n = len(pending) if k is None else min(k, len(pending))
        for _ in range(n):
            pending.pop(0)()

    def proj_pieces(c, h_n):
        rs = rows(c)

        def dense(dst, w_ref, lo, hi):
            def run():
                dst[rs, lo:hi] = _dot(h_n, w_ref[:, lo:hi])
            return run

        def slabs(dst, base, w_ref, lo, hi):
            def run():
                res = _dot(h_n, w_ref[:, lo:hi])
                for j in range(lo // LANES, hi // LANES):
                    dst[j, base:base + CHUNK, :] = res[:, j * LANES - lo:(j + 1) * LANES - lo]
            return run

        out = [slabs(raw_s, conv_base(c), wxbc_ref, k, k + MXU_COLS) for k in range(0, D_XBC, MXU_COLS)]
        out += [slabs(u_s, pool_base(c), wu_ref, k, k + MXU_COLS) for k in range(0, D_POOL, MXU_COLS)]
        out += [dense(z_s, wz_ref, k, k + MXU_COLS) for k in range(0, D_SSD, MXU_COLS)]
        out.append(dense(dtr_s, wdt_ref, 0, LANES))
        return out

    @pl.when(step == 0)
    def _():
        wz_ref[...] = wz_in[...]
        wxbc_ref[...] = wxbc_in[...]
        wdt_ref[...] = wdt_in[...]
        wu_ref[...] = wu_in[...]
        poolw_ref[...] = poolw_in[...]
        wout_ref[...] = wout_in[...]
        raw_s[:, 0:CONV_PAD, :] = jnp.zeros((D_XBC // LANES, CONV_PAD, LANES), F32)
        u_s[:, 0:POOL_PAD, :] = jnp.zeros((D_POOL // LANES, POOL_PAD, LANES), F32)
        for c in range(n_chunks):
            for run in proj_pieces(c, in_norm(xc_ref[rows(c), :], modc_ref)):
                run()

    @pl.when(tile_in_seq == 0)
    def _():
        st_s[...] = jnp.zeros(st_s.shape, F32)

    gt = modc_ref[0, 5:6, :]
    a_neg = -jnp.exp(alog_ref[...])
    ri = lax.broadcasted_iota(jnp.int32, (CHUNK, CHUNK), 0)
    ci = lax.broadcasted_iota(jnp.int32, (CHUNK, CHUNK), 1)
    causal = ri >= ci
    tril = jnp.where(causal, 1.0, 0.0).astype(BF16)
    low_half = ci < HEAD_DIM
    low_half_row = lax.broadcasted_iota(jnp.int32, (1, LANES), 1) < HEAD_DIM

    def pre(c, seq_tile):
        rs = rows(c)
        last = c == n_chunks - 1

        def carry(hist):
            return jnp.where(next_starts_seq, jnp.zeros_like(hist), hist) if last else hist

        base, nbase = conv_base(c), conv_base(c + 1)
        for j in range(D_XBC // LANES):
            cols = slice(j * LANES, (j + 1) * LANES)
            xc = convb_ref[:, cols]
            for k in range(D_CONV):
                xc = xc + convw_ref[k:k + 1, cols] * raw_s[j, pl.ds(base - (D_CONV - 1) + k, CHUNK), :]
            raw_s[j, nbase - CONV_PAD:nbase, :] = carry(raw_s[j, base + CHUNK - CONV_PAD:base + CHUNK, :])
            xbc_s[rs, cols] = _silu(xc)
            if j % 2 == 1:
                emit(1)
        base, nbase = pool_base(c), pool_base(c + 1)
        diffs = []
        for gi, w in enumerate(POOL_WINDOWS):
            halves = []
            for j in range(gi * POOL_GROUP_DIM // LANES, (gi + 1) * POOL_GROUP_DIM // LANES):
                u_g = u_s[j, base:base + CHUNK, :]
                s = u_g
                for k in range(1, w):
                    s = s + u_s[j, pl.ds(base - k, CHUNK), :]
                if c == 0:
                    pos = seq_tile * MIX_ROWS + 1 + lax.broadcasted_iota(jnp.int32, (CHUNK, LANES), 0)
                    pooled = s / jnp.minimum(pos, w).astype(F32)
                else:
                    pooled = s * (1.0 / w)
                halves.append(pooled - u_g)
                u_s[j, nbase - POOL_PAD:nbase, :] = carry(u_s[j, base + CHUNK - POOL_PAD:base + CHUNK, :])
            diffs.append(jnp.concatenate(halves, axis=1).astype(BF16))
            emit(1)
        zs = _silu(z_s[rs, :])
        dt_c = jax.nn.softplus(dtr_s[rs, :] + dtb_ref[...])
        emit()
        return diffs, zs, dt_c

    def ssd(c, zs, dt_c):
        rs = rows(c)
        p_hi, p_mid, p_lo = _split3(dt_c * a_neg)
        cs = _dot(tril, p_hi) + _dot(tril, p_mid) + _dot(tril, p_lo)
        cs_t = cs.T
        dt_t = dt_c.T
        w_t = dt_t * jnp.exp(cs_t[:, CHUNK - 1:CHUNK] - cs_t)
        xs_c = xbc_s[rs, 0:D_SSD]
        emit(1)
        y_pairs = []
        for g in range(N_GROUPS):
            b0 = D_SSD + g * D_STATE
            c0 = D_SSD + N_GROUPS * D_STATE + g * D_STATE
            b_t = xbc_s[rs, b0:b0 + D_STATE].T
            c_g = xbc_s[rs, c0:c0 + D_STATE]
            cb = _dot(c_g.astype(BF16), b_t.astype(BF16))
            for pr in range(HEADS_PER_GROUP // 2):
                lhs, bw_t, cd = [], [], []
                for r in (2 * pr, 2 * pr + 1):
                    hd = g * HEADS_PER_GROUP + r
                    cs_bc = jnp.broadcast_to(cs[:, hd:hd + 1], (CHUNK, CHUNK))
                    decay = jnp.exp(jnp.where(causal, cs_bc - cs_t[hd:hd + 1, :], -jnp.inf))
                    scores = decay * (cb * dt_t[hd:hd + 1, :])
                    sd = jnp.exp(cs_bc)
                    lhs.append(jnp.concatenate([scores, c_g * sd], axis=1).astype(BF16))
                    bw_t.append((b_t * w_t[hd:hd + 1, :]).astype(BF16))
                    cd.append(sd[CHUNK - 1:CHUNK, :])
                col = (g * HEADS_PER_GROUP + 2 * pr) * HEAD_DIM
                x_p = xs_c[:, col:col + LANES]
                st = st_s[:, col:col + LANES]
                rhs = jnp.concatenate([x_p, st], axis=0).astype(BF16)
                y_pairs.append(jnp.where(low_half, _dot(lhs[0], rhs), _dot(lhs[1], rhs)))
                zero = jnp.zeros_like(x_p)
                x_split = jnp.concatenate(
                    [jnp.where(low_half, x_p, zero), jnp.where(low_half, zero, x_p)], axis=0)
                upd = _dot(jnp.concatenate(bw_t, axis=1), x_split.astype(BF16))
                st_s[:, col:col + LANES] = st * jnp.where(low_half_row, cd[0], cd[1]) + upd
                emit(1)
        y = jnp.concatenate(y_pairs, axis=1) + dskip_ref[...] * xs_c
        yz = y * zs
        gw = D_SSD // N_GROUPS
        parts = []
        for g in range(N_GROUPS):
            blk = yz[:, g * gw:(g + 1) * gw]
            ms = jnp.mean(blk * blk, axis=-1, keepdims=True)
            parts.append(blk * lax.rsqrt(ms + EPS))
        return (jnp.concatenate(parts, axis=1) * ssdnw_ref[...]).astype(BF16)

    def post_pieces(c, y_ssd, diffs):
        rs = rows(c)
        outs = [None] * len(POOL_WINDOWS)

        def pool_piece(gi):
            def run():
                lo, hi = gi * POOL_GROUP_DIM, (gi + 1) * POOL_GROUP_DIM
                o = (_dot(diffs[gi], poolw_ref[gi]) + poolb_ref[:, lo:hi]) * pools_ref[:, lo:hi]
                outs[gi] = o.astype(BF16)
            return run

        def out_piece(lo, hi):
            def run():
                y_pool = jnp.concatenate(outs, axis=1)
                out = (_dot(y_ssd, wout_ref[0:D_SSD, lo:hi])
                       + _dot(y_pool, wout_ref[D_SSD:D_SSD + D_POOL, lo:hi]))
                o_ref[rs, lo:hi] = xc_ref[rs, lo:hi] + gt[:, lo:hi] * out
            return run

        return ([pool_piece(gi) for gi in range(len(POOL_WINDOWS))]
                + [out_piece(k, k + MXU_COLS) for k in range(0, D_MODEL, MXU_COLS)])

    def hand_over(vals):
        diffs, zs, dt_c = vals
        for gi in range(len(POOL_WINDOWS)):
            diff_s[:, gi * POOL_GROUP_DIM:(gi + 1) * POOL_GROUP_DIM] = diffs[gi]
        zs_s[...] = zs
        dtc_s[...] = dt_c

    @pl.when(step == 0)
    def _():
        hand_over(pre(0, tile_in_seq))

    diffs = [diff_s[:, gi * POOL_GROUP_DIM:(gi + 1) * POOL_GROUP_DIM] for gi in range(len(POOL_WINDOWS))]
    zs, dt_c = zs_s[...], dtc_s[...]
    h_n = in_norm(xn_ref[rows(0), :], modn_ref)
    pending.extend(proj_pieces(0, h_n))
    for c in range(n_chunks):
        y_ssd = ssd(c, zs, dt_c)
        pending.extend(post_pieces(c, y_ssd, diffs))
        if c + 1 < n_chunks:
            h_n = in_norm(xn_ref[rows(c + 1), :], modn_ref)
            emit(2)
            diffs, zs, dt_c = pre(c + 1, tile_in_seq)
            pending.extend(proj_pieces(c + 1, h_n))
        else:
            hand_over(pre(0, (step + 1) % tiles_per_seq))
            emit()


def _mixer(x, mod3, norm_w, wz, wxbc, wdt, wu, convw, convb, dtb, alog, dskip, ssdnw,
           poolw, poolb, pools, wout):
    bsz, seq, _ = x.shape
    tiles_per_seq = seq // MIX_ROWS
    n_chunks = MIX_ROWS // CHUNK
    steps = bsz * tiles_per_seq
    nxt = lambda s: jnp.minimum(s + 1, steps - 1)
    consts = [norm_w, wz, wxbc, wdt, wu, convw, convb, dtb, alog, dskip, ssdnw,
              poolw, poolb, pools, wout]
    x2 = x.reshape(bsz * seq, D_MODEL)
    out = pl.pallas_call(
        functools.partial(_mixer_kernel, tiles_per_seq=tiles_per_seq),
        grid=(steps,),
        in_specs=[
            pl.BlockSpec((MIX_ROWS, D_MODEL), lambda s: (s, 0)),
            pl.BlockSpec((MIX_ROWS, D_MODEL), lambda s: (nxt(s), 0)),
            pl.BlockSpec((1, N_MOD, D_MODEL), lambda s: (s // tiles_per_seq, 0, 0)),
            pl.BlockSpec((1, N_MOD, D_MODEL), lambda s: (nxt(s) // tiles_per_seq, 0, 0)),
        ] + [_const_spec(a.shape) for a in consts],
        out_specs=pl.BlockSpec((MIX_ROWS, D_MODEL), lambda s: (s, 0)),
        out_shape=jax.ShapeDtypeStruct((bsz * seq, D_MODEL), F32),
        scratch_shapes=[pltpu.VMEM(w.shape, BF16) for w in (wz, wxbc, wdt, wu, poolw, wout)] + [
            pltpu.VMEM((MIX_ROWS, D_SSD), F32),
            pltpu.VMEM((D_XBC // LANES, n_chunks * (CONV_PAD + CHUNK), LANES), F32),
            pltpu.VMEM((MIX_ROWS, LANES), F32),
            pltpu.VMEM((D_POOL // LANES, n_chunks * (POOL_PAD + CHUNK), LANES), F32),
            pltpu.VMEM((MIX_ROWS, D_XBC), F32),
            pltpu.VMEM((D_STATE, D_SSD), F32),
            pltpu.VMEM((CHUNK, D_POOL), BF16),
            pltpu.VMEM((CHUNK, D_SSD), F32),
            pltpu.VMEM((CHUNK, LANES), F32),
        ],
        compiler_params=pltpu.CompilerParams(
            dimension_semantics=("arbitrary",), vmem_limit_bytes=VMEM_LIMIT_BYTES),
        name="mixer",
    )(x2, x2, mod3, mod3, *consts)
    return out.reshape(bsz, seq, D_MODEL)


def _pad_lanes(v):
    return jnp.pad(v.reshape(1, -1), ((0, 0), (0, LANES - v.shape[0])))


def kernel(x, c, w_ada, b_ada, ffn1_norm, ffn1_w_gate, ffn1_w_up, ffn1_w_down, mix_norm, w_in, conv_w, conv_b, dt_bias, a_log, d_skip, ssd_norm_w, pool_w, pool_b, pool_scale, w_out, ffn2_norm, ffn2_w_gate, ffn2_w_up, ffn2_w_down, final_norm):
    bsz = x.shape[0]
    depth = w_ada.shape[0]
    c_pad = jnp.pad(c, ((0, SUBLANES - bsz), (0, 0)))
    row = lambda v: v.reshape(1, -1)
    for i in range(depth):
        mod = _adaln(c_pad, w_ada[i], row(b_ada[i]))
        mod3 = mod[:bsz].reshape(bsz, N_MOD, D_MODEL)
        last = i == depth - 1

        x = _ffn(x, mod3, row(ffn1_norm[i]), ffn1_w_gate[i].astype(BF16), ffn1_w_up[i].astype(BF16),
                 ffn1_w_down[i].astype(BF16), row(final_norm), mod_idx=0, final=False)

        wi = w_in[i]
        o_xbc = D_SSD
        o_dt = o_xbc + D_XBC
        o_u = o_dt + N_HEADS
        wdt = jnp.pad(wi[:, o_dt:o_u], ((0, 0), (0, LANES - N_HEADS)))
        x = _mixer(
            x, mod3, row(mix_norm[i]),
            wi[:, :o_xbc].astype(BF16), wi[:, o_xbc:o_dt].astype(BF16), wdt.astype(BF16),
            wi[:, o_u:].astype(BF16),
            conv_w[i], row(conv_b[i]), _pad_lanes(dt_bias[i]), _pad_lanes(a_log[i]),
            row(jnp.repeat(d_skip[i], HEAD_DIM)), row(ssd_norm_w[i]),
            pool_w[i].astype(BF16), row(pool_b[i]), row(pool_scale[i]), w_out[i].astype(BF16))

        x = _ffn(x, mod3, row(ffn2_norm[i]), ffn2_w_gate[i].astype(BF16), ffn2_w_up[i].astype(BF16),
                 ffn2_w_down[i].astype(BF16), row(final_norm), mod_idx=6, final=last)
    return x
```

```python
import functools

import jax
import jax.numpy as jnp
from jax import lax
from jax.experimental import pallas as pl
from jax.experimental.pallas import tpu as pltpu

F32 = jnp.float32
BF16 = jnp.bfloat16

D_MODEL = 1024
D_FF = 2816
N_MOD = 9
FFN_RES = 0.5
EPS = 1e-6
D_SSD = 1024
HEAD_DIM = 64
N_HEADS = 16
N_GROUPS = 4
HEADS_PER_GROUP = N_HEADS // N_GROUPS
D_STATE = 128
D_CONV = 4
CHUNK = 128
D_POOL = 1024
POOL_WINDOWS = (2, 4, 8, 16)
POOL_GROUP_DIM = 256
D_XBC = D_SSD + 2 * N_GROUPS * D_STATE

LANES = 128
SUBLANES = 8
MXU_COLS = 256
VMEM_LIMIT_BYTES = 60 * 1024 * 1024

FFN_ROWS = 512
NORM_SLICES = 8
FFN_W_ROWS_IN = 128
FFN_W_ROWS_OUT = 256
MIX_ROWS = 512
ADA_COLS = 1024
CONV_PAD = SUBLANES
POOL_PAD = 16

assert HEAD_DIM * 2 == LANES and D_STATE == LANES and CHUNK == LANES


def _dot(a, b):
    return jnp.dot(a, b, preferred_element_type=F32)


def _split3(v):
    hi = v.astype(BF16)
    r1 = v - hi.astype(F32)
    mid = r1.astype(BF16)
    lo = (r1 - mid.astype(F32)).astype(BF16)
    return hi, mid, lo


def _silu(v):
    return v * jax.nn.sigmoid(v)


def _rms(v, w):
    ms = jnp.mean(v * v, axis=-1, keepdims=True)
    return v * lax.rsqrt(ms + EPS) * w


def _const_spec(shape):
    zeros = (0,) * len(shape)
    return pl.BlockSpec(shape, lambda *_: zeros, pipeline_mode=pl.Buffered(1))


def _adaln_kernel(c_ref, w_ref, b_ref, o_ref):
    c = c_ref[...]
    a_hi, a_mid, _ = _split3(_silu(c))
    w_hi, w_mid, _ = _split3(w_ref[...])
    acc = _dot(a_hi, w_hi) + _dot(a_mid, w_hi) + _dot(a_hi, w_mid)
    o_ref[...] = acc + b_ref[...]


def _adaln(c_pad, w_ada, b_ada):
    rows = c_pad.shape[0]
    n = w_ada.shape[1]
    return pl.pallas_call(
        _adaln_kernel,
        grid=(n // ADA_COLS,),
        in_specs=[
            pl.BlockSpec((rows, D_MODEL), lambda j: (0, 0)),
            pl.BlockSpec((D_MODEL, ADA_COLS), lambda j: (0, j)),
            pl.BlockSpec((1, ADA_COLS), lambda j: (0, j)),
        ],
        out_specs=pl.BlockSpec((rows, ADA_COLS), lambda j: (0, j)),
        out_shape=jax.ShapeDtypeStruct((rows, n), F32),
        compiler_params=pltpu.CompilerParams(
            dimension_semantics=("arbitrary",), vmem_limit_bytes=VMEM_LIMIT_BYTES),
        name="adaln",
    )(c_pad, w_ada, b_ada)


def _load_rows_bf16(w_hbm, layer, dst_ref, stage_ref, sem):
    rows = stage_ref.shape[1]
    n_blocks = dst_ref.shape[0] // rows

    def copy(i):
        return pltpu.make_async_copy(
            w_hbm.at[layer, pl.ds(i * rows, rows), :], stage_ref.at[i % 2], sem.at[i % 2])

    copy(0).start()
    for i in range(n_blocks):
        if i + 1 < n_blocks:
            copy(i + 1).start()
        copy(i).wait()
        dst_ref[i * rows:(i + 1) * rows, :] = stage_ref[i % 2].astype(BF16)


def _ffn_kernel(xc_ref, xn_ref, modc_ref, modn_ref, nw_ref, wg_hbm, wu_hbm, wd_hbm, fw_ref, o_ref,
                h_s, wg_ref, wu_ref, wd_ref, stage_in, stage_out, sem_in, sem_out,
                *, layer, mod_idx, final):
    def in_norm(x, mod_ref):
        sh = mod_ref[0, pl.ds(mod_idx, 1), :]
        sc = mod_ref[0, pl.ds(mod_idx + 1, 1), :]
        return (_rms(x, nw_ref[...]) * (1.0 + sc) + sh).astype(BF16)

    step = pl.program_id(0)
    cur = h_s.at[step % 2]
    nxt = h_s.at[(step + 1) % 2]

    @pl.when(step == 0)
    def _():
        _load_rows_bf16(wg_hbm, layer, wg_ref, stage_in, sem_in)
        _load_rows_bf16(wu_hbm, layer, wu_ref, stage_in, sem_in)
        _load_rows_bf16(wd_hbm, layer, wd_ref, stage_out, sem_out)
        cur[...] = in_norm(xc_ref[...], modc_ref)

    n_chunks = D_FF // MXU_COLS
    slice_rows = FFN_ROWS // NORM_SLICES
    acts = []
    for k in range(n_chunks):
        if k < NORM_SLICES:
            rs = slice(k * slice_rows, (k + 1) * slice_rows)
            nxt[rs, :] = in_norm(xn_ref[rs, :], modn_ref)
        cols = slice(k * MXU_COLS, (k + 1) * MXU_COLS)
        h = cur[...]
        g = _dot(h, wg_ref[:, cols])
        u = _dot(h, wu_ref[:, cols])
        acts.append((_silu(g) * u).astype(BF16))
    a = jnp.concatenate(acts, axis=1)
    o = _dot(a, wd_ref[...])
    gt = modc_ref[0, pl.ds(mod_idx + 2, 1), :]
    out = xc_ref[...] + (FFN_RES * gt) * o
    if final:
        out = _rms(out, fw_ref[...])
    o_ref[...] = out


def _ffn(x, mod3, norm_w, wg, wu, wd, final_w, *, layer, mod_idx, final):
    bsz, seq, _ = x.shape
    tiles_per_seq = seq // FFN_ROWS
    steps = bsz * tiles_per_seq
    nxt = lambda s: jnp.minimum(s + 1, steps - 1)
    x2 = x.reshape(bsz * seq, D_MODEL)
    hbm = pl.BlockSpec(memory_space=pl.ANY)
    out = pl.pallas_call(
        functools.partial(_ffn_kernel, layer=layer, mod_idx=mod_idx, final=final),
        grid=(steps,),
        in_specs=[
            pl.BlockSpec((FFN_ROWS, D_MODEL), lambda s: (s, 0)),
            pl.BlockSpec((FFN_ROWS, D_MODEL), lambda s: (nxt(s), 0)),
            pl.BlockSpec((1, N_MOD, D_MODEL), lambda s: (s // tiles_per_seq, 0, 0)),
            pl.BlockSpec((1, N_MOD, D_MODEL), lambda s: (nxt(s) // tiles_per_seq, 0, 0)),
            _const_spec((1, D_MODEL)),
            hbm, hbm, hbm,
            _const_spec((1, D_MODEL)),
        ],
        out_specs=pl.BlockSpec((FFN_ROWS, D_MODEL), lambda s: (s, 0)),
        out_shape=jax.ShapeDtypeStruct((bsz * seq, D_MODEL), F32),
        scratch_shapes=[
            pltpu.VMEM((2, FFN_ROWS, D_MODEL), BF16),
            pltpu.VMEM((D_MODEL, D_FF), BF16),
            pltpu.VMEM((D_MODEL, D_FF), BF16),
            pltpu.VMEM((D_FF, D_MODEL), BF16),
            pltpu.VMEM((2, FFN_W_ROWS_IN, D_FF), F32),
            pltpu.VMEM((2, FFN_W_ROWS_OUT, D_MODEL), F32),
            pltpu.SemaphoreType.DMA((2,)),
            pltpu.SemaphoreType.DMA((2,)),
        ],
        compiler_params=pltpu.CompilerParams(
            dimension_semantics=("arbitrary",), vmem_limit_bytes=VMEM_LIMIT_BYTES),
        name="ffn_final" if final else "ffn",
    )(x2, x2, mod3, mod3, norm_w, wg, wu, wd, final_w)
    return out.reshape(bsz, seq, D_MODEL)


def _mixer_kernel(xc_ref, xn_ref, modc_ref, modn_ref, nw_ref, wz_in, wxbc_in, wdt_in, wu_in,
                  convw_ref, convb_ref, dtb_ref, alog_ref, dskip_ref, ssdnw_ref,
                  poolw_in, poolb_ref, pools_ref, wout_in,
                  o_ref,
                  wz_ref, wxbc_ref, wdt_ref, wu_ref, poolw_ref, wout_ref,
                  z_s, raw_s, dtr_s, u_s, xbc_s, st_s, diff_s, zs_s, dtc_s, *, tiles_per_seq):
    step = pl.program_id(0)
    tile_in_seq = step % tiles_per_seq
    next_starts_seq = tile_in_seq == tiles_per_seq - 1
    n_chunks = MIX_ROWS // CHUNK

    def in_norm(x, mod_ref):
        sh = mod_ref[0, 3:4, :]
        sc = mod_ref[0, 4:5, :]
        return (_rms(x, nw_ref[...]) * (1.0 + sc) + sh).astype(BF16)

    def rows(c):
        return slice(c * CHUNK, (c + 1) * CHUNK)

    def conv_base(c):
        return (c % n_chunks) * (CONV_PAD + CHUNK) + CONV_PAD

    def pool_base(c):
        return (c % n_chunks) * (POOL_PAD + CHUNK) + POOL_PAD

    pending = []

    def emit(k=None):
        n = len(pending) if k is None else min(k, len(pending))
        for _ in range(n):
            pending.pop(0)()

    def proj_pieces(c, h_n):
        rs = rows(c)

        def dense(dst, w_ref, lo, hi):
            def run():
                dst[rs, lo:hi] = _dot(h_n, w_ref[:, lo:hi])
            return run

        def slabs(dst, base, w_ref, lo, hi):
            def run():
                res = _dot(h_n, w_ref[:, lo:hi])
                for j in range(lo // LANES, hi // LANES):
                    dst[j, base:base + CHUNK, :] = res[:, j * LANES - lo:(j + 1) * LANES - lo]
            return run

        out = [slabs(raw_s, conv_base(c), wxbc_ref, k, k + MXU_COLS) for k in range(0, D_XBC, MXU_COLS)]
        out += [slabs(u_s, pool_base(c), wu_ref, k, k + MXU_COLS) for k in range(0, D_POOL, MXU_COLS)]
        out += [dense(z_s, wz_ref, k, k + MXU_COLS) for k in range(0, D_SSD, MXU_COLS)]
        out.append(dense(dtr_s, wdt_ref, 0, LANES))
        return out

    @pl.when(step == 0)
    def _():
        wz_ref[...] = wz_in[...]
        wxbc_ref[...] = wxbc_in[...]
        wdt_ref[...] = wdt_in[...]
        wu_ref[...] = wu_in[...]
        poolw_ref[...] = poolw_in[...]
        wout_ref[...] = wout_in[...]
        raw_s[:, 0:CONV_PAD, :] = jnp.zeros((D_XBC // LANES, CONV_PAD, LANES), F32)
        u_s[:, 0:POOL_PAD, :] = jnp.zeros((D_POOL // LANES, POOL_PAD, LANES), F32)
        for c in range(n_chunks):
            for run in proj_pieces(c, in_norm(xc_ref[rows(c), :], modc_ref)):
                run()

    @pl.when(tile_in_seq == 0)
    def _():
        st_s[...] = jnp.zeros(st_s.shape, F32)

    gt = modc_ref[0, 5:6, :]
    a_neg = -jnp.exp(alog_ref[...])
    ri = lax.broadcasted_iota(jnp.int32, (CHUNK, CHUNK), 0)
    ci = lax.broadcasted_iota(jnp.int32, (CHUNK, CHUNK), 1)
    causal = ri >= ci
    tril = jnp.where(causal, 1.0, 0.0).astype(BF16)
    low_half = ci < HEAD_DIM
    low_half_row = lax.broadcasted_iota(jnp.int32, (1, LANES), 1) < HEAD_DIM

    def pre(c, seq_tile):
        rs = rows(c)
        last = c == n_chunks - 1

        def carry(hist):
            return jnp.where(next_starts_seq, jnp.zeros_like(hist), hist) if last else hist

        base, nbase = conv_base(c), conv_base(c + 1)
        for j in range(D_XBC // LANES):
            cols = slice(j * LANES, (j + 1) * LANES)
            xc = convb_ref[:, cols]
            for k in range(D_CONV):
                xc = xc + convw_ref[k:k + 1, cols] * raw_s[j, pl.ds(base - (D_CONV - 1) + k, CHUNK), :]
            raw_s[j, nbase - CONV_PAD:nbase, :] = carry(raw_s[j, base + CHUNK - CONV_PAD:base + CHUNK, :])
            xbc_s[rs, cols] = _silu(xc)
            if j % 2 == 1:
                emit(1)
        base, nbase = pool_base(c), pool_base(c + 1)
        diffs = []
        for gi, w in enumerate(POOL_WINDOWS):
            halves = []
            for j in range(gi * POOL_GROUP_DIM // LANES, (gi + 1) * POOL_GROUP_DIM // LANES):
                u_g = u_s[j, base:base + CHUNK, :]
                s = u_g
                for k in range(1, w):
                    s = s + u_s[j, pl.ds(base - k, CHUNK), :]
                if c == 0:
                    pos = seq_tile * MIX_ROWS + 1 + lax.broadcasted_iota(jnp.int32, (CHUNK, LANES), 0)
                    pooled = s / jnp.minimum(pos, w).astype(F32)
                else:
                    pooled = s * (1.0 / w)
                halves.append(pooled - u_g)
                u_s[j, nbase - POOL_PAD:nbase, :] = carry(u_s[j, base + CHUNK - POOL_PAD:base + CHUNK, :])
            diffs.append(jnp.concatenate(halves, axis=1).astype(BF16))
            emit(1)
        zs = _silu(z_s[rs, :])
        dt_c = jax.nn.softplus(dtr_s[rs, :] + dtb_ref[...])
        emit()
        return diffs, zs, dt_c

    def ssd(c, zs, dt_c):
        rs = rows(c)
        p_hi, p_mid, p_lo = _split3(dt_c * a_neg)
        cs = _dot(tril, p_hi) + _dot(tril, p_mid) + _dot(tril, p_lo)
        cs_t = cs.T
        dt_t = dt_c.T
        w_t = dt_t * jnp.exp(cs_t[:, CHUNK - 1:CHUNK] - cs_t)
        xs_c = xbc_s[rs, 0:D_SSD]
        emit(1)
        y_pairs = []
        for g in range(N_GROUPS):
            b0 = D_SSD + g * D_STATE
            c0 = D_SSD + N_GROUPS * D_STATE + g * D_STATE
            b_t = xbc_s[rs, b0:b0 + D_STATE].T
            c_g = xbc_s[rs, c0:c0 + D_STATE]
            cb = _dot(c_g.astype(BF16), b_t.astype(BF16))
            for pr in range(HEADS_PER_GROUP // 2):
                lhs, bw_t, cd = [], [], []
                for r in (2 * pr, 2 * pr + 1):
                    hd = g * HEADS_PER_GROUP + r
                    cs_bc = jnp.broadcast_to(cs[:, hd:hd + 1], (CHUNK, CHUNK))
                    decay = jnp.exp(jnp.where(causal, cs_bc - cs_t[hd:hd + 1, :], -jnp.inf))
                    scores = decay * (cb * dt_t[hd:hd + 1, :])
                    sd = jnp.exp(cs_bc)
                    lhs.append(jnp.concatenate([scores, c_g * sd], axis=1).astype(BF16))
                    bw_t.append((b_t * w_t[hd:hd + 1, :]).astype(BF16))
                    cd.append(sd[CHUNK - 1:CHUNK, :])
                col = (g * HEADS_PER_GROUP + 2 * pr) * HEAD_DIM
                x_p = xs_c[:, col:col + LANES]
                st = st_s[:, col:col + LANES]
                rhs = jnp.concatenate([x_p, st], axis=0).astype(BF16)
                y_pairs.append(jnp.where(low_half, _dot(lhs[0], rhs), _dot(lhs[1], rhs)))
                zero = jnp.zeros_like(x_p)
                x_split = jnp.concatenate(
                    [jnp.where(low_half, x_p, zero), jnp.where(low_half, zero, x_p)], axis=0)
                upd = _dot(jnp.concatenate(bw_t, axis=1), x_split.astype(BF16))
                st_s[:, col:col + LANES] = st * jnp.where(low_half_row, cd[0], cd[1]) + upd
                emit(1)
        y = jnp.concatenate(y_pairs, axis=1) + dskip_ref[...] * xs_c
        yz = y * zs
        gw = D_SSD // N_GROUPS
        parts = []
        for g in range(N_GROUPS):
            blk = yz[:, g * gw:(g + 1) * gw]
            ms = jnp.mean(blk * blk, axis=-1, keepdims=True)
            parts.append(blk * lax.rsqrt(ms + EPS))
        return (jnp.concatenate(parts, axis=1) * ssdnw_ref[...]).astype(BF16)

    def post_pieces(c, y_ssd, diffs):
        rs = rows(c)
        outs = [None] * len(POOL_WINDOWS)

        def pool_piece(gi):
            def run():
                lo, hi = gi * POOL_GROUP_DIM, (gi + 1) * POOL_GROUP_DIM
                o = (_dot(diffs[gi], poolw_ref[gi]) + poolb_ref[:, lo:hi]) * pools_ref[:, lo:hi]
                outs[gi] = o.astype(BF16)
            return run

        def out_piece(lo, hi):
            def run():
                y_pool = jnp.concatenate(outs, axis=1)
                out = (_dot(y_ssd, wout_ref[0:D_SSD, lo:hi])
                       + _dot(y_pool, wout_ref[D_SSD:D_SSD + D_POOL, lo:hi]))
                o_ref[rs, lo:hi] = xc_ref[rs, lo:hi] + gt[:, lo:hi] * out
            return run

        return ([pool_piece(gi) for gi in range(len(POOL_WINDOWS))]
                + [out_piece(k, k + MXU_COLS) for k in range(0, D_MODEL, MXU_COLS)])

    def hand_over(vals):
        diffs, zs, dt_c = vals
        for gi in range(len(POOL_WINDOWS)):
            diff_s[:, gi * POOL_GROUP_DIM:(gi + 1) * POOL_GROUP_DIM] = diffs[gi]
        zs_s[...] = zs
        dtc_s[...] = dt_c

    @pl.when(step == 0)
    def _():
        hand_over(pre(0, tile_in_seq))

    diffs = [diff_s[:, gi * POOL_GROUP_DIM:(gi + 1) * POOL_GROUP_DIM] for gi in range(len(POOL_WINDOWS))]
    zs, dt_c = zs_s[...], dtc_s[...]
    h_n = in_norm(xn_ref[rows(0), :], modn_ref)
    pending.extend(proj_pieces(0, h_n))
    for c in range(n_chunks):
        y_ssd = ssd(c, zs, dt_c)
        pending.extend(post_pieces(c, y_ssd, diffs))
        if c + 1 < n_chunks:
            h_n = in_norm(xn_ref[rows(c + 1), :], modn_ref)
            emit(2)
            diffs, zs, dt_c = pre(c + 1, tile_in_seq)
            pending.extend(proj_pieces(c + 1, h_n))
        else:
            hand_over(pre(0, (step + 1) % tiles_per_seq))
            emit()


def _mixer(x, mod3, norm_w, wz, wxbc, wdt, wu, convw, convb, dtb, alog, dskip, ssdnw,
           poolw, poolb, pools, wout):
    bsz, seq, _ = x.shape
    tiles_per_seq = seq // MIX_ROWS
    n_chunks = MIX_ROWS // CHUNK
    steps = bsz * tiles_per_seq
    nxt = lambda s: jnp.minimum(s + 1, steps - 1)
    consts = [norm_w, wz, wxbc, wdt, wu, convw, convb, dtb, alog, dskip, ssdnw,
              poolw, poolb, pools, wout]
    x2 = x.reshape(bsz * seq, D_MODEL)
    out = pl.pallas_call(
        functools.partial(_mixer_kernel, tiles_per_seq=tiles_per_seq),
        grid=(steps,),
        in_specs=[
            pl.BlockSpec((MIX_ROWS, D_MODEL), lambda s: (s, 0)),
            pl.BlockSpec((MIX_ROWS, D_MODEL), lambda s: (nxt(s), 0)),
            pl.BlockSpec((1, N_MOD, D_MODEL), lambda s: (s // tiles_per_seq, 0, 0)),
            pl.BlockSpec((1, N_MOD, D_MODEL), lambda s: (nxt(s) // tiles_per_seq, 0, 0)),
        ] + [_const_spec(a.shape) for a in consts],
        out_specs=pl.BlockSpec((MIX_ROWS, D_MODEL), lambda s: (s, 0)),
        out_shape=jax.ShapeDtypeStruct((bsz * seq, D_MODEL), F32),
        scratch_shapes=[pltpu.VMEM(w.shape, BF16) for w in (wz, wxbc, wdt, wu, poolw, wout)] + [
            pltpu.VMEM((MIX_ROWS, D_SSD), F32),
            pltpu.VMEM((D_XBC // LANES, n_chunks * (CONV_PAD + CHUNK), LANES), F32),
            pltpu.VMEM((MIX_ROWS, LANES), F32),
            pltpu.VMEM((D_POOL // LANES, n_chunks * (POOL_PAD + CHUNK), LANES), F32),
            pltpu.VMEM((MIX_ROWS, D_XBC), F32),
            pltpu.VMEM((D_STATE, D_SSD), F32),
            pltpu.VMEM((CHUNK, D_POOL), BF16),
            pltpu.VMEM((CHUNK, D_SSD), F32),
            pltpu.VMEM((CHUNK, LANES), F32),
        ],
        compiler_params=pltpu.CompilerParams(
            dimension_semantics=("arbitrary",), vmem_limit_bytes=VMEM_LIMIT_BYTES),
        name="mixer",
    )(x2, x2, mod3, mod3, *consts)
    return out.reshape(bsz, seq, D_MODEL)


def _pad_lanes(v):
    return jnp.pad(v.reshape(1, -1), ((0, 0), (0, LANES - v.shape[0])))


def kernel(x, c, w_ada, b_ada, ffn1_norm, ffn1_w_gate, ffn1_w_up, ffn1_w_down, mix_norm, w_in, conv_w, conv_b, dt_bias, a_log, d_skip, ssd_norm_w, pool_w, pool_b, pool_scale, w_out, ffn2_norm, ffn2_w_gate, ffn2_w_up, ffn2_w_down, final_norm):
    bsz = x.shape[0]
    depth = w_ada.shape[0]
    c_pad = jnp.pad(c, ((0, SUBLANES - bsz), (0, 0)))
    row = lambda v: v.reshape(1, -1)
    for i in range(depth):
        mod = _adaln(c_pad, w_ada[i], row(b_ada[i]))
        mod3 = mod[:bsz].reshape(bsz, N_MOD, D_MODEL)
        last = i == depth - 1

        x = _ffn(x, mod3, row(ffn1_norm[i]), ffn1_w_gate, ffn1_w_up, ffn1_w_down, row(final_norm),
                 layer=i, mod_idx=0, final=False)

        wi = w_in[i]
        o_xbc = D_SSD
        o_dt = o_xbc + D_XBC
        o_u = o_dt + N_HEADS
        wdt = jnp.pad(wi[:, o_dt:o_u], ((0, 0), (0, LANES - N_HEADS)))
        x = _mixer(
            x, mod3, row(mix_norm[i]),
            wi[:, :o_xbc].astype(BF16), wi[:, o_xbc:o_dt].astype(BF16), wdt.astype(BF16),
            wi[:, o_u:].astype(BF16),
            conv_w[i], row(conv_b[i]), _pad_lanes(dt_bias[i]), _pad_lanes(a_log[i]),
            row(jnp.repeat(d_skip[i], HEAD_DIM)), row(ssd_norm_w[i]),
            pool_w[i].astype(BF16), row(pool_b[i]), row(pool_scale[i]), w_out[i].astype(BF16))

        x = _ffn(x, mod3, row(ffn2_norm[i]), ffn2_w_gate, ffn2_w_up, ffn2_w_down, row(final_norm),
                 layer=i, mod_idx=6, final=last)
    return x
```

```python
import functools

import jax
import jax.numpy as jnp
from jax import lax
from jax.experimental import pallas as pl
from jax.experimental.pallas import tpu as pltpu

F32 = jnp.float32
BF16 = jnp.bfloat16

D_MODEL = 1024
D_FF = 2816
N_MOD = 9
FFN_RES = 0.5
EPS = 1e-6
D_SSD = 1024
HEAD_DIM = 64
N_HEADS = 16
N_GROUPS = 4
HEADS_PER_GROUP = N_HEADS // N_GROUPS
D_STATE = 128
D_CONV = 4
CHUNK = 128
D_POOL = 1024
POOL_WINDOWS = (2, 4, 8, 16)
POOL_GROUP_DIM = 256
D_XBC = D_SSD + 2 * N_GROUPS * D_STATE
DT_COL = D_SSD + D_XBC

LANES = 128
SUBLANES = 8
MXU_COLS = 256
VMEM_LIMIT_BYTES = 60 * 1024 * 1024

FFN_ROWS = 512
NORM_SLICES = 8
FFN_W_ROWS_IN = 128
FFN_W_ROWS_OUT = 256
MIX_W_ROWS = 256
W_SLOTS = 8
MIX_ROWS = 512
ADA_COLS = 1024
CONV_PAD = SUBLANES
POOL_PAD = 16

assert HEAD_DIM * 2 == LANES and D_STATE == LANES and CHUNK == LANES


def _dot(a, b):
    return jnp.dot(a, b, preferred_element_type=F32)


def _split3(v):
    hi = v.astype(BF16)
    r1 = v - hi.astype(F32)
    mid = r1.astype(BF16)
    lo = (r1 - mid.astype(F32)).astype(BF16)
    return hi, mid, lo


def _silu(v):
    return v * jax.nn.sigmoid(v)


def _rms(v, w):
    ms = jnp.mean(v * v, axis=-1, keepdims=True)
    return v * lax.rsqrt(ms + EPS) * w


def _const_spec(shape):
    zeros = (0,) * len(shape)
    return pl.BlockSpec(shape, lambda *_: zeros, pipeline_mode=pl.Buffered(1))


def _adaln_kernel(c_ref, w_ref, b_ref, o_ref):
    c = c_ref[...]
    a_hi, a_mid, _ = _split3(_silu(c))
    w_hi, w_mid, _ = _split3(w_ref[...])
    acc = _dot(a_hi, w_hi) + _dot(a_mid, w_hi) + _dot(a_hi, w_mid)
    o_ref[...] = acc + b_ref[...]


def _adaln(c_pad, w_ada, b_ada):
    rows = c_pad.shape[0]
    n = w_ada.shape[1]
    return pl.pallas_call(
        _adaln_kernel,
        grid=(n // ADA_COLS,),
        in_specs=[
            pl.BlockSpec((rows, D_MODEL), lambda j: (0, 0)),
            pl.BlockSpec((D_MODEL, ADA_COLS), lambda j: (0, j)),
            pl.BlockSpec((1, ADA_COLS), lambda j: (0, j)),
        ],
        out_specs=pl.BlockSpec((rows, ADA_COLS), lambda j: (0, j)),
        out_shape=jax.ShapeDtypeStruct((rows, n), F32),
        compiler_params=pltpu.CompilerParams(
            dimension_semantics=("arbitrary",), vmem_limit_bytes=VMEM_LIMIT_BYTES),
        name="adaln",
    )(c_pad, w_ada, b_ada)


class _BlockStream:
    def __init__(self, stage_ref, sem):
        self.stage, self.sem, self.blocks = stage_ref, sem, []
        self.n_slots = stage_ref.shape[0]

    def add(self, src, sink):
        self.blocks.append((src, sink))

    def _copy(self, i):
        slot = i % self.n_slots
        return pltpu.make_async_copy(self.blocks[i][0], self.stage.at[slot], self.sem.at[slot])

    def prime(self):
        for i in range(min(self.n_slots, len(self.blocks))):
            self._copy(i).start()

    def drain(self):
        for i in range(len(self.blocks)):
            self._copy(i).wait()
            self.blocks[i][1](self.stage[i % self.n_slots])
            if i + self.n_slots < len(self.blocks):
                self._copy(i + self.n_slots).start()


def _stream_rows_bf16(stream, w_hbm, layer, dst_ref):
    rows = stream.stage.shape[1]

    def sink(r):
        def run(block):
            dst_ref[r:r + rows, :] = block.astype(BF16)
        return run

    for r in range(0, dst_ref.shape[0], rows):
        stream.add(w_hbm.at[layer, pl.ds(r, rows), :], sink(r))


def _ffn_kernel(xc_ref, xn_ref, modc_ref, modn_ref, nw_ref, wg_hbm, wu_hbm, wd_hbm, fw_ref, o_ref,
                h_s, wg_ref, wu_ref, wd_ref, stage_in, stage_out, sem_in, sem_out,
                *, layer, mod_idx, final):
    def in_norm(x, mod_ref):
        sh = mod_ref[0, pl.ds(mod_idx, 1), :]
        sc = mod_ref[0, pl.ds(mod_idx + 1, 1), :]
        return (_rms(x, nw_ref[...]) * (1.0 + sc) + sh).astype(BF16)

    step = pl.program_id(0)
    cur = h_s.at[step % 2]
    nxt = h_s.at[(step + 1) % 2]

    @pl.when(step == 0)
    def _():
        s_in, s_out = _BlockStream(stage_in, sem_in), _BlockStream(stage_out, sem_out)
        _stream_rows_bf16(s_in, wg_hbm, layer, wg_ref)
        _stream_rows_bf16(s_in, wu_hbm, layer, wu_ref)
        _stream_rows_bf16(s_out, wd_hbm, layer, wd_ref)
        s_in.prime()
        s_out.prime()
        s_in.drain()
        s_out.drain()
        cur[...] = in_norm(xc_ref[...], modc_ref)

    n_chunks = D_FF // MXU_COLS
    slice_rows = FFN_ROWS // NORM_SLICES
    acts = []
    for k in range(n_chunks):
        if k < NORM_SLICES:
            rs = slice(k * slice_rows, (k + 1) * slice_rows)
            nxt[rs, :] = in_norm(xn_ref[rs, :], modn_ref)
        cols = slice(k * MXU_COLS, (k + 1) * MXU_COLS)
        h = cur[...]
        g = _dot(h, wg_ref[:, cols])
        u = _dot(h, wu_ref[:, cols])
        acts.append((_silu(g) * u).astype(BF16))
    a = jnp.concatenate(acts, axis=1)
    o = _dot(a, wd_ref[...])
    gt = modc_ref[0, pl.ds(mod_idx + 2, 1), :]
    out = xc_ref[...] + (FFN_RES * gt) * o
    if final:
        out = _rms(out, fw_ref[...])
    o_ref[...] = out


def _ffn(x, mod3, norm_w, wg, wu, wd, final_w, *, layer, mod_idx, final):
    bsz, seq, _ = x.shape
    tiles_per_seq = seq // FFN_ROWS
    steps = bsz * tiles_per_seq
    nxt = lambda s: jnp.minimum(s + 1, steps - 1)
    x2 = x.reshape(bsz * seq, D_MODEL)
    hbm = pl.BlockSpec(memory_space=pl.ANY)
    out = pl.pallas_call(
        functools.partial(_ffn_kernel, layer=layer, mod_idx=mod_idx, final=final),
        grid=(steps,),
        in_specs=[
            pl.BlockSpec((FFN_ROWS, D_MODEL), lambda s: (s, 0)),
            pl.BlockSpec((FFN_ROWS, D_MODEL), lambda s: (nxt(s), 0)),
            pl.BlockSpec((1, N_MOD, D_MODEL), lambda s: (s // tiles_per_seq, 0, 0)),
            pl.BlockSpec((1, N_MOD, D_MODEL), lambda s: (nxt(s) // tiles_per_seq, 0, 0)),
            _const_spec((1, D_MODEL)),
            hbm, hbm, hbm,
            _const_spec((1, D_MODEL)),
        ],
        out_specs=pl.BlockSpec((FFN_ROWS, D_MODEL), lambda s: (s, 0)),
        out_shape=jax.ShapeDtypeStruct((bsz * seq, D_MODEL), F32),
        scratch_shapes=[
            pltpu.VMEM((2, FFN_ROWS, D_MODEL), BF16),
            pltpu.VMEM((D_MODEL, D_FF), BF16),
            pltpu.VMEM((D_MODEL, D_FF), BF16),
            pltpu.VMEM((D_FF, D_MODEL), BF16),
            pltpu.VMEM((W_SLOTS, FFN_W_ROWS_IN, D_FF), F32),
            pltpu.VMEM((W_SLOTS, FFN_W_ROWS_OUT, D_MODEL), F32),
            pltpu.SemaphoreType.DMA((W_SLOTS,)),
            pltpu.SemaphoreType.DMA((W_SLOTS,)),
        ],
        compiler_params=pltpu.CompilerParams(
            dimension_semantics=("arbitrary",), vmem_limit_bytes=VMEM_LIMIT_BYTES),
        name="ffn_final" if final else "ffn",
    )(x2, x2, mod3, mod3, norm_w, wg, wu, wd, final_w)
    return out.reshape(bsz, seq, D_MODEL)


def _mixer_kernel(xc_ref, xn_ref, modc_ref, modn_ref, nw_ref, win_hbm, wtail_ref,
                  convw_ref, convb_ref, dtb_ref, alog_ref, dskip_ref, ssdnw_ref,
                  poolw_in, poolb_ref, pools_ref, wout_hbm,
                  o_ref,
                  wz_ref, wxbc_ref, wdt_ref, wu_ref, poolw_ref, wout_ref, stage, sem,
                  z_s, raw_s, dtr_s, u_s, xbc_s, st_s, diff_s, zs_s, dtc_s, *, layer, tiles_per_seq):
    step = pl.program_id(0)
    tile_in_seq = step % tiles_per_seq
    next_starts_seq = tile_in_seq == tiles_per_seq - 1
    n_chunks = MIX_ROWS // CHUNK

    def in_norm(x, mod_ref):
        sh = mod_ref[0, 3:4, :]
        sc = mod_ref[0, 4:5, :]
        return (_rms(x, nw_ref[...]) * (1.0 + sc) + sh).astype(BF16)

    def rows(c):
        return slice(c * CHUNK, (c + 1) * CHUNK)

    def conv_base(c):
        return (c % n_chunks) * (CONV_PAD + CHUNK) + CONV_PAD

    def pool_base(c):
        return (c % n_chunks) * (POOL_PAD + CHUNK) + POOL_PAD

    pending = []

    def emit(k=None):
        n = len(pending) if k is None else min(k, len(pending))
        for _ in range(n):
            pending.pop(0)()

    def proj_pieces(c, h_n):
        rs = rows(c)

        def dense(dst, w_ref, lo, hi):
            def run():
                dst[rs, lo:hi] = _dot(h_n, w_ref[:, lo:hi])
            return run

        def slabs(dst, base, w_ref, lo, hi):
            def run():
                res = _dot(h_n, w_ref[:, lo:hi])
                for j in range(lo // LANES, hi // LANES):
                    dst[j, base:base + CHUNK, :] = res[:, j * LANES - lo:(j + 1) * LANES - lo]
            return run

        out = [slabs(raw_s, conv_base(c), wxbc_ref, k, k + MXU_COLS) for k in range(0, D_XBC, MXU_COLS)]
        out += [slabs(u_s, pool_base(c), wu_ref, k, k + MXU_COLS) for k in range(0, D_POOL, MXU_COLS)]
        out += [dense(z_s, wz_ref, k, k + MXU_COLS) for k in range(0, D_SSD, MXU_COLS)]
        out.append(dense(dtr_s, wdt_ref, 0, LANES))
        return out

    @pl.when(step == 0)
    def _():
        stream = _BlockStream(stage, sem)
        w_rows, w_cols = stage.shape[1], stage.shape[2]

        def cast_into(dst, r, c0):
            def run(block):
                dst[r:r + w_rows, c0:c0 + w_cols] = block.astype(BF16)
            return run

        def dt_u_into(r):
            def run(block):
                lane = lax.broadcasted_iota(jnp.int32, (w_rows, LANES), 1)
                wdt_ref[r:r + w_rows, :] = jnp.where(lane < N_HEADS, block[:, 0:LANES], 0.0).astype(BF16)
                moved = pltpu.roll(block, w_cols - N_HEADS, 1)
                last = jnp.where(lane >= LANES - N_HEADS, wtail_ref[r:r + w_rows, :],
                                 moved[:, w_cols - LANES:])
                wu_ref[r:r + w_rows, :] = jnp.concatenate(
                    [moved[:, :w_cols - LANES], last], axis=1).astype(BF16)
            return run

        for r in range(0, D_MODEL, w_rows):
            stream.add(win_hbm.at[layer, pl.ds(r, w_rows), pl.ds(0, w_cols)], cast_into(wz_ref, r, 0))
        for c0 in range(0, D_XBC, w_cols):
            for r in range(0, D_MODEL, w_rows):
                stream.add(win_hbm.at[layer, pl.ds(r, w_rows), pl.ds(D_SSD + c0, w_cols)],
                           cast_into(wxbc_ref, r, c0))
        for r in range(0, D_MODEL, w_rows):
            stream.add(win_hbm.at[layer, pl.ds(r, w_rows), pl.ds(DT_COL, w_cols)], dt_u_into(r))
        for r in range(0, D_SSD + D_POOL, w_rows):
            stream.add(wout_hbm.at[layer, pl.ds(r, w_rows), :], cast_into(wout_ref, r, 0))
        stream.prime()
        poolw_ref[...] = poolw_in[...]
        stream.drain()
        raw_s[:, 0:CONV_PAD, :] = jnp.zeros((D_XBC // LANES, CONV_PAD, LANES), F32)
        u_s[:, 0:POOL_PAD, :] = jnp.zeros((D_POOL // LANES, POOL_PAD, LANES), F32)
        for c in range(n_chunks):
            for run in proj_pieces(c, in_norm(xc_ref[rows(c), :], modc_ref)):
                run()

    @pl.when(tile_in_seq == 0)
    def _():
        st_s[...] = jnp.zeros(st_s.shape, F32)

    gt = modc_ref[0, 5:6, :]
    a_neg = -jnp.exp(alog_ref[...])
    ri = lax.broadcasted_iota(jnp.int32, (CHUNK, CHUNK), 0)
    ci = lax.broadcasted_iota(jnp.int32, (CHUNK, CHUNK), 1)
    causal = ri >= ci
    tril = jnp.where(causal, 1.0, 0.0).astype(BF16)
    low_half = ci < HEAD_DIM
    low_half_row = lax.broadcasted_iota(jnp.int32, (1, LANES), 1) < HEAD_DIM

    def pre(c, seq_tile):
        rs = rows(c)
        last = c == n_chunks - 1

        def carry(hist):
            return jnp.where(next_starts_seq, jnp.zeros_like(hist), hist) if last else hist

        base, nbase = conv_base(c), conv_base(c + 1)
        for j in range(D_XBC // LANES):
            cols = slice(j * LANES, (j + 1) * LANES)
            xc = convb_ref[:, cols]
            for k in range(D_CONV):
                xc = xc + convw_ref[k:k + 1, cols] * raw_s[j, pl.ds(base - (D_CONV - 1) + k, CHUNK), :]
            raw_s[j, nbase - CONV_PAD:nbase, :] = carry(raw_s[j, base + CHUNK - CONV_PAD:base + CHUNK, :])
            xbc_s[rs, cols] = _silu(xc)
            if j % 2 == 1:
                emit(1)
        base, nbase = pool_base(c), pool_base(c + 1)
        diffs = []
        for gi, w in enumerate(POOL_WINDOWS):
            halves = []
            for j in range(gi * POOL_GROUP_DIM // LANES, (gi + 1) * POOL_GROUP_DIM // LANES):
                u_g = u_s[j, base:base + CHUNK, :]
                s = u_g
                for k in range(1, w):
                    s = s + u_s[j, pl.ds(base - k, CHUNK), :]
                if c == 0:
                    pos = seq_tile * MIX_ROWS + 1 + lax.broadcasted_iota(jnp.int32, (CHUNK, LANES), 0)
                    pooled = s / jnp.minimum(pos, w).astype(F32)
                else:
                    pooled = s * (1.0 / w)
                halves.append(pooled - u_g)
                u_s[j, nbase - POOL_PAD:nbase, :] = carry(u_s[j, base + CHUNK - POOL_PAD:base + CHUNK, :])
            diffs.append(jnp.concatenate(halves, axis=1).astype(BF16))
            emit(1)
        zs = _silu(z_s[rs, :])
        dt_c = jax.nn.softplus(dtr_s[rs, :] + dtb_ref[...])
        emit()
        return diffs, zs, dt_c

    def ssd(c, zs, dt_c):
        rs = rows(c)
        p_hi, p_mid, p_lo = _split3(dt_c * a_neg)
        cs = _dot(tril, p_hi) + _dot(tril, p_mid) + _dot(tril, p_lo)
        cs_t = cs.T
        dt_t = dt_c.T
        w_t = dt_t * jnp.exp(cs_t[:, CHUNK - 1:CHUNK] - cs_t)
        xs_c = xbc_s[rs, 0:D_SSD]
        emit(1)
        y_pairs = []
        for g in range(N_GROUPS):
            b0 = D_SSD + g * D_STATE
            c0 = D_SSD + N_GROUPS * D_STATE + g * D_STATE
            b_t = xbc_s[rs, b0:b0 + D_STATE].T
            c_g = xbc_s[rs, c0:c0 + D_STATE]
            cb = _dot(c_g.astype(BF16), b_t.astype(BF16))
            for pr in range(HEADS_PER_GROUP // 2):
                lhs, bw_t, cd = [], [], []
                for r in (2 * pr, 2 * pr + 1):
                    hd = g * HEADS_PER_GROUP + r
                    cs_bc = jnp.broadcast_to(cs[:, hd:hd + 1], (CHUNK, CHUNK))
                    decay = jnp.exp(jnp.where(causal, cs_bc - cs_t[hd:hd + 1, :], -jnp.inf))
                    scores = decay * (cb * dt_t[hd:hd + 1, :])
                    sd = jnp.exp(cs_bc)
                    lhs.append(jnp.concatenate([scores, c_g * sd], axis=1).astype(BF16))
                    bw_t.append((b_t * w_t[hd:hd + 1, :]).astype(BF16))
                    cd.append(sd[CHUNK - 1:CHUNK, :])
                col = (g * HEADS_PER_GROUP + 2 * pr) * HEAD_DIM
                x_p = xs_c[:, col:col + LANES]
                st = st_s[:, col:col + LANES]
                rhs = jnp.concatenate([x_p, st], axis=0).astype(BF16)
                y_pairs.append(jnp.where(low_half, _dot(lhs[0], rhs), _dot(lhs[1], rhs)))
                zero = jnp.zeros_like(x_p)
                x_split = jnp.concatenate(
                    [jnp.where(low_half, x_p, zero), jnp.where(low_half, zero, x_p)], axis=0)
                upd = _dot(jnp.concatenate(bw_t, axis=1), x_split.astype(BF16))
                st_s[:, col:col + LANES] = st * jnp.where(low_half_row, cd[0], cd[1]) + upd
                emit(1)
        y = jnp.concatenate(y_pairs, axis=1) + dskip_ref[...] * xs_c
        yz = y * zs
        gw = D_SSD // N_GROUPS
        parts = []
        for g in range(N_GROUPS):
            blk = yz[:, g * gw:(g + 1) * gw]
            ms = jnp.mean(blk * blk, axis=-1, keepdims=True)
            parts.append(blk * lax.rsqrt(ms + EPS))
        return (jnp.concatenate(parts, axis=1) * ssdnw_ref[...]).astype(BF16)

    def post_pieces(c, y_ssd, diffs):
        rs = rows(c)
        outs = [None] * len(POOL_WINDOWS)

        def pool_piece(gi):
            def run():
                lo, hi = gi * POOL_GROUP_DIM, (gi + 1) * POOL_GROUP_DIM
                o = (_dot(diffs[gi], poolw_ref[gi]) + poolb_ref[:, lo:hi]) * pools_ref[:, lo:hi]
                outs[gi] = o.astype(BF16)
            return run

        def out_piece(lo, hi):
            def run():
                y_pool = jnp.concatenate(outs, axis=1)
                out = (_dot(y_ssd, wout_ref[0:D_SSD, lo:hi])
                       + _dot(y_pool, wout_ref[D_SSD:D_SSD + D_POOL, lo:hi]))
                o_ref[rs, lo:hi] = xc_ref[rs, lo:hi] + gt[:, lo:hi] * out
            return run

        return ([pool_piece(gi) for gi in range(len(POOL_WINDOWS))]
                + [out_piece(k, k + MXU_COLS) for k in range(0, D_MODEL, MXU_COLS)])

    def hand_over(vals):
        diffs, zs, dt_c = vals
        for gi in range(len(POOL_WINDOWS)):
            diff_s[:, gi * POOL_GROUP_DIM:(gi + 1) * POOL_GROUP_DIM] = diffs[gi]
        zs_s[...] = zs
        dtc_s[...] = dt_c

    @pl.when(step == 0)
    def _():
        hand_over(pre(0, tile_in_seq))

    diffs = [diff_s[:, gi * POOL_GROUP_DIM:(gi + 1) * POOL_GROUP_DIM] for gi in range(len(POOL_WINDOWS))]
    zs, dt_c = zs_s[...], dtc_s[...]
    h_n = in_norm(xn_ref[rows(0), :], modn_ref)
    pending.extend(proj_pieces(0, h_n))
    for c in range(n_chunks):
        y_ssd = ssd(c, zs, dt_c)
        pending.extend(post_pieces(c, y_ssd, diffs))
        if c + 1 < n_chunks:
            h_n = in_norm(xn_ref[rows(c + 1), :], modn_ref)
            emit(2)
            diffs, zs, dt_c = pre(c + 1, tile_in_seq)
            pending.extend(proj_pieces(c + 1, h_n))
        else:
            hand_over(pre(0, (step + 1) % tiles_per_seq))
            emit()


def _mixer(x, mod3, norm_w, w_in, w_tail, convw, convb, dtb, alog, dskip, ssdnw,
           poolw, poolb, pools, w_out, *, layer):
    bsz, seq, _ = x.shape
    tiles_per_seq = seq // MIX_ROWS
    n_chunks = MIX_ROWS // CHUNK
    steps = bsz * tiles_per_seq
    nxt = lambda s: jnp.minimum(s + 1, steps - 1)
    hbm = pl.BlockSpec(memory_space=pl.ANY)
    vmem_consts = [w_tail, convw, convb, dtb, alog, dskip, ssdnw, poolw, poolb, pools]
    x2 = x.reshape(bsz * seq, D_MODEL)
    out = pl.pallas_call(
        functools.partial(_mixer_kernel, layer=layer, tiles_per_seq=tiles_per_seq),
        grid=(steps,),
        in_specs=[
            pl.BlockSpec((MIX_ROWS, D_MODEL), lambda s: (s, 0)),
            pl.BlockSpec((MIX_ROWS, D_MODEL), lambda s: (nxt(s), 0)),
            pl.BlockSpec((1, N_MOD, D_MODEL), lambda s: (s // tiles_per_seq, 0, 0)),
            pl.BlockSpec((1, N_MOD, D_MODEL), lambda s: (nxt(s) // tiles_per_seq, 0, 0)),
            _const_spec(norm_w.shape), hbm,
        ] + [_const_spec(a.shape) for a in vmem_consts] + [hbm],
        out_specs=pl.BlockSpec((MIX_ROWS, D_MODEL), lambda s: (s, 0)),
        out_shape=jax.ShapeDtypeStruct((bsz * seq, D_MODEL), F32),
        scratch_shapes=[
            pltpu.VMEM((D_MODEL, D_SSD), BF16),
            pltpu.VMEM((D_MODEL, D_XBC), BF16),
            pltpu.VMEM((D_MODEL, LANES), BF16),
            pltpu.VMEM((D_MODEL, D_POOL), BF16),
            pltpu.VMEM(poolw.shape, BF16),
            pltpu.VMEM((D_SSD + D_POOL, D_MODEL), BF16),
            pltpu.VMEM((W_SLOTS, MIX_W_ROWS, D_MODEL), F32),
            pltpu.SemaphoreType.DMA((W_SLOTS,)),
            pltpu.VMEM((MIX_ROWS, D_SSD), F32),
            pltpu.VMEM((D_XBC // LANES, n_chunks * (CONV_PAD + CHUNK), LANES), F32),
            pltpu.VMEM((MIX_ROWS, LANES), F32),
            pltpu.VMEM((D_POOL // LANES, n_chunks * (POOL_PAD + CHUNK), LANES), F32),
            pltpu.VMEM((MIX_ROWS, D_XBC), F32),
            pltpu.VMEM((D_STATE, D_SSD), F32),
            pltpu.VMEM((CHUNK, D_POOL), BF16),
            pltpu.VMEM((CHUNK, D_SSD), F32),
            pltpu.VMEM((CHUNK, LANES), F32),
        ],
        compiler_params=pltpu.CompilerParams(
            dimension_semantics=("arbitrary",), vmem_limit_bytes=VMEM_LIMIT_BYTES),
        name="mixer",
    )(x2, x2, mod3, mod3, norm_w, w_in, *vmem_consts, w_out)
    return out.reshape(bsz, seq, D_MODEL)


def _pad_lanes(v):
    return jnp.pad(v.reshape(1, -1), ((0, 0), (0, LANES - v.shape[0])))


def kernel(x, c, w_ada, b_ada, ffn1_norm, ffn1_w_gate, ffn1_w_up, ffn1_w_down, mix_norm, w_in, conv_w, conv_b, dt_bias, a_log, d_skip, ssd_norm_w, pool_w, pool_b, pool_scale, w_out, ffn2_norm, ffn2_w_gate, ffn2_w_up, ffn2_w_down, final_norm):
    bsz = x.shape[0]
    depth = w_ada.shape[0]
    c_pad = jnp.pad(c, ((0, SUBLANES - bsz), (0, 0)))
    row = lambda v: v.reshape(1, -1)
    for i in range(depth):
        mod = _adaln(c_pad, w_ada[i], row(b_ada[i]))
        mod3 = mod[:bsz].reshape(bsz, N_MOD, D_MODEL)
        last = i == depth - 1

        x = _ffn(x, mod3, row(ffn1_norm[i]), ffn1_w_gate, ffn1_w_up, ffn1_w_down, row(final_norm),
                 layer=i, mod_idx=0, final=False)

        w_tail = jnp.pad(w_in[i][:, DT_COL + D_POOL:], ((0, 0), (LANES - N_HEADS, 0)))
        x = _mixer(
            x, mod3, row(mix_norm[i]), w_in, w_tail,
            conv_w[i], row(conv_b[i]), _pad_lanes(dt_bias[i]), _pad_lanes(a_log[i]),
            row(jnp.repeat(d_skip[i], HEAD_DIM)), row(ssd_norm_w[i]),
            pool_w[i].astype(BF16), row(pool_b[i]), row(pool_scale[i]), w_out, layer=i)

        x = _ffn(x, mod3, row(ffn2_norm[i]), ffn2_w_gate, ffn2_w_up, ffn2_w_down, row(final_norm),
                 layer=i, mod_idx=6, final=last)
    return x
```

```python
import functools

import jax
import jax.numpy as jnp
from jax import lax
from jax.experimental import pallas as pl
from jax.experimental.pallas import tpu as pltpu

F32 = jnp.float32
BF16 = jnp.bfloat16

D_MODEL = 1024
D_FF = 2816
N_MOD = 9
FFN_RES = 0.5
EPS = 1e-6
D_SSD = 1024
HEAD_DIM = 64
N_HEADS = 16
N_GROUPS = 4
HEADS_PER_GROUP = N_HEADS // N_GROUPS
D_STATE = 128
D_CONV = 4
CHUNK = 128
D_POOL = 1024
POOL_WINDOWS = (2, 4, 8, 16)
POOL_GROUP_DIM = 256
D_XBC = D_SSD + 2 * N_GROUPS * D_STATE
DT_COL = D_SSD + D_XBC

LANES = 128
SUBLANES = 8
MXU_COLS = 256
VMEM_LIMIT_BYTES = 60 * 1024 * 1024

FFN_ROWS = 512
NORM_SLICES = 8
FFN_W_ROWS_IN = 128
FFN_W_ROWS_OUT = 256
MIX_W_ROWS = 256
W_SLOTS = 8
MIX_ROWS = 512
ADA_COLS = 1024
CONV_PAD = SUBLANES
POOL_PAD = 16

assert HEAD_DIM * 2 == LANES and D_STATE == LANES and CHUNK == LANES


def _dot(a, b):
    return jnp.dot(a, b, preferred_element_type=F32)


def _split3(v):
    hi = v.astype(BF16)
    r1 = v - hi.astype(F32)
    mid = r1.astype(BF16)
    lo = (r1 - mid.astype(F32)).astype(BF16)
    return hi, mid, lo


def _silu(v):
    return v * jax.nn.sigmoid(v)


def _rms(v, w):
    ms = jnp.mean(v * v, axis=-1, keepdims=True)
    return v * lax.rsqrt(ms + EPS) * w


def _const_spec(shape):
    zeros = (0,) * len(shape)
    return pl.BlockSpec(shape, lambda *_: zeros, pipeline_mode=pl.Buffered(1))


def _adaln_kernel(ct_ref, w_ref, b_ref, o_ref, act_s):
    bsz = o_ref.shape[0]

    @pl.when(pl.program_id(0) == 0)
    def _():
        act = _silu(ct_ref[...])
        for m in range(bsz):
            act_s[m] = jnp.broadcast_to(act[:, m:m + 1], (D_MODEL, LANES))

    for m in range(bsz):
        a = act_s[m]
        cols = [jnp.sum(w_ref[:, j:j + LANES] * a, axis=0, keepdims=True)
                for j in range(0, ADA_COLS, LANES)]
        o_ref[m:m + 1, :] = jnp.concatenate(cols, axis=1) + b_ref[...]


def _adaln(c, w_ada, b_ada):
    bsz = c.shape[0]
    n = w_ada.shape[1]
    c_t = jnp.pad(c.T, ((0, 0), (0, LANES - bsz)))
    return pl.pallas_call(
        _adaln_kernel,
        grid=(n // ADA_COLS,),
        in_specs=[
            pl.BlockSpec((D_MODEL, LANES), lambda j: (0, 0)),
            pl.BlockSpec((D_MODEL, ADA_COLS), lambda j: (0, j)),
            pl.BlockSpec((1, ADA_COLS), lambda j: (0, j)),
        ],
        out_specs=pl.BlockSpec((bsz, ADA_COLS), lambda j: (0, j)),
        out_shape=jax.ShapeDtypeStruct((bsz, n), F32),
        scratch_shapes=[pltpu.VMEM((bsz, D_MODEL, LANES), F32)],
        compiler_params=pltpu.CompilerParams(
            dimension_semantics=("arbitrary",), vmem_limit_bytes=VMEM_LIMIT_BYTES),
        name="adaln",
    )(c_t, w_ada, b_ada)


class _BlockStream:
    def __init__(self, stage_ref, sem):
        self.stage, self.sem, self.blocks = stage_ref, sem, []
        self.n_slots = stage_ref.shape[0]

    def add(self, src, sink):
        self.blocks.append((src, sink))

    def _slot(self, i):
        return self.stage.at[i % self.n_slots, pl.ds(0, self.blocks[i][0].shape[0])]

    def _copy(self, i):
        return pltpu.make_async_copy(self.blocks[i][0], self._slot(i), self.sem.at[i % self.n_slots])

    def prime(self):
        for i in range(min(self.n_slots, len(self.blocks))):
            self._copy(i).start()

    def drain(self):
        for i in range(len(self.blocks)):
            self._copy(i).wait()
            self.blocks[i][1](self._slot(i)[...])
            if i + self.n_slots < len(self.blocks):
                self._copy(i + self.n_slots).start()


def _stream_rows_bf16(stream, w_hbm, layer, dst_ref):
    rows = stream.stage.shape[1]

    def sink(r):
        def run(block):
            dst_ref[r:r + rows, :] = block.astype(BF16)
        return run

    for r in range(0, dst_ref.shape[0], rows):
        stream.add(w_hbm.at[layer, pl.ds(r, rows), :], sink(r))


def _ffn_kernel(xc_ref, xn_ref, modc_ref, modn_ref, nw_ref, wg_hbm, wu_hbm, wd_hbm, fw_ref, o_ref,
                h_s, wg_ref, wu_ref, wd_ref, stage_in, stage_out, sem_in, sem_out,
                *, layer, mod_idx, final):
    def in_norm(x, mod_ref):
        sh = mod_ref[0, pl.ds(mod_idx, 1), :]
        sc = mod_ref[0, pl.ds(mod_idx + 1, 1), :]
        return (_rms(x, nw_ref[...]) * (1.0 + sc) + sh).astype(BF16)

    step = pl.program_id(0)
    cur = h_s.at[step % 2]
    nxt = h_s.at[(step + 1) % 2]

    @pl.when(step == 0)
    def _():
        s_in, s_out = _BlockStream(stage_in, sem_in), _BlockStream(stage_out, sem_out)
        _stream_rows_bf16(s_in, wg_hbm, layer, wg_ref)
        _stream_rows_bf16(s_in, wu_hbm, layer, wu_ref)
        _stream_rows_bf16(s_out, wd_hbm, layer, wd_ref)
        s_in.prime()
        s_out.prime()
        s_in.drain()
        s_out.drain()
        cur[...] = in_norm(xc_ref[...], modc_ref)

    n_chunks = D_FF // MXU_COLS
    slice_rows = FFN_ROWS // NORM_SLICES
    acts = []
    for k in range(n_chunks):
        if k < NORM_SLICES:
            rs = slice(k * slice_rows, (k + 1) * slice_rows)
            nxt[rs, :] = in_norm(xn_ref[rs, :], modn_ref)
        cols = slice(k * MXU_COLS, (k + 1) * MXU_COLS)
        h = cur[...]
        g = _dot(h, wg_ref[:, cols])
        u = _dot(h, wu_ref[:, cols])
        acts.append((_silu(g) * u).astype(BF16))
    a = jnp.concatenate(acts, axis=1)
    o = _dot(a, wd_ref[...])
    gt = modc_ref[0, pl.ds(mod_idx + 2, 1), :]
    out = xc_ref[...] + (FFN_RES * gt) * o
    if final:
        out = _rms(out, fw_ref[...])
    o_ref[...] = out


def _ffn(x, mod3, norm_w, wg, wu, wd, final_w, *, layer, mod_idx, final):
    bsz, seq, _ = x.shape
    tiles_per_seq = seq // FFN_ROWS
    steps = bsz * tiles_per_seq
    nxt = lambda s: jnp.minimum(s + 1, steps - 1)
    x2 = x.reshape(bsz * seq, D_MODEL)
    hbm = pl.BlockSpec(memory_space=pl.ANY)
    out = pl.pallas_call(
        functools.partial(_ffn_kernel, layer=layer, mod_idx=mod_idx, final=final),
        grid=(steps,),
        in_specs=[
            pl.BlockSpec((FFN_ROWS, D_MODEL), lambda s: (s, 0)),
            pl.BlockSpec((FFN_ROWS, D_MODEL), lambda s: (nxt(s), 0)),
            pl.BlockSpec((1, N_MOD, D_MODEL), lambda s: (s // tiles_per_seq, 0, 0)),
            pl.BlockSpec((1, N_MOD, D_MODEL), lambda s: (nxt(s) // tiles_per_seq, 0, 0)),
            _const_spec((1, D_MODEL)),
            hbm, hbm, hbm,
            _const_spec((1, D_MODEL)),
        ],
        out_specs=pl.BlockSpec((FFN_ROWS, D_MODEL), lambda s: (s, 0)),
        out_shape=jax.ShapeDtypeStruct((bsz * seq, D_MODEL), F32),
        scratch_shapes=[
            pltpu.VMEM((2, FFN_ROWS, D_MODEL), BF16),
            pltpu.VMEM((D_MODEL, D_FF), BF16),
            pltpu.VMEM((D_MODEL, D_FF), BF16),
            pltpu.VMEM((D_FF, D_MODEL), BF16),
            pltpu.VMEM((W_SLOTS, FFN_W_ROWS_IN, D_FF), F32),
            pltpu.VMEM((W_SLOTS, FFN_W_ROWS_OUT, D_MODEL), F32),
            pltpu.SemaphoreType.DMA((W_SLOTS,)),
            pltpu.SemaphoreType.DMA((W_SLOTS,)),
        ],
        compiler_params=pltpu.CompilerParams(
            dimension_semantics=("arbitrary",), vmem_limit_bytes=VMEM_LIMIT_BYTES),
        name="ffn_final" if final else "ffn",
    )(x2, x2, mod3, mod3, norm_w, wg, wu, wd, final_w)
    return out.reshape(bsz, seq, D_MODEL)


def _mixer_kernel(xc_ref, xn_ref, modc_ref, modn_ref, nw_ref, wint_hbm,
                  convw_ref, convb_ref, dtb_ref, alog_ref, dskip_ref, ssdnw_ref,
                  poolw_in, poolb_ref, pools_ref, wout_hbm,
                  o_ref,
                  wz_ref, wxbc_ref, wdt_ref, wu_ref, poolw_ref, wout_ref, stage, sem,
                  z_s, raw_s, dtr_s, u_s, xbc_s, st_s, diff_s, zs_s, dtc_s, *, layer, tiles_per_seq):
    step = pl.program_id(0)
    tile_in_seq = step % tiles_per_seq
    next_starts_seq = tile_in_seq == tiles_per_seq - 1
    n_chunks = MIX_ROWS // CHUNK

    def in_norm(x, mod_ref):
        sh = mod_ref[0, 3:4, :]
        sc = mod_ref[0, 4:5, :]
        return (_rms(x, nw_ref[...]) * (1.0 + sc) + sh).astype(BF16)

    def rows(c):
        return slice(c * CHUNK, (c + 1) * CHUNK)

    def conv_base(c):
        return (c % n_chunks) * (CONV_PAD + CHUNK) + CONV_PAD

    def pool_base(c):
        return (c % n_chunks) * (POOL_PAD + CHUNK) + POOL_PAD

    pending = []

    def emit(k=None):
        n = len(pending) if k is None else min(k, len(pending))
        for _ in range(n):
            pending.pop(0)()

    def proj_pieces(c, h_n):
        rs = rows(c)

        def dense(dst, w_ref, lo, hi):
            def run():
                dst[rs, lo:hi] = _dot(h_n, w_ref[:, lo:hi])
            return run

        def slabs(dst, base, w_ref, lo, hi):
            def run():
                res = _dot(h_n, w_ref[:, lo:hi])
                for j in range(lo // LANES, hi // LANES):
                    dst[j, base:base + CHUNK, :] = res[:, j * LANES - lo:(j + 1) * LANES - lo]
            return run

        out = [slabs(raw_s, conv_base(c), wxbc_ref, k, k + MXU_COLS) for k in range(0, D_XBC, MXU_COLS)]
        out += [slabs(u_s, pool_base(c), wu_ref, k, k + MXU_COLS) for k in range(0, D_POOL, MXU_COLS)]
        out += [dense(z_s, wz_ref, k, k + MXU_COLS) for k in range(0, D_SSD, MXU_COLS)]
        out.append(dense(dtr_s, wdt_ref, 0, LANES))
        return out

    @pl.when(step == 0)
    def _():
        stream = _BlockStream(stage, sem)
        w_rows, w_cols = stage.shape[1], stage.shape[2]

        def cast_into(dst, r, c0):
            def run(block):
                dst[r:r + w_rows, c0:c0 + w_cols] = block.astype(BF16)
            return run

        def transposed_into(dst, c0):
            def run(block):
                dst[:, c0:c0 + block.shape[0]] = block.T.astype(BF16)
            return run

        def dt_into(block):
            lane = lax.broadcasted_iota(jnp.int32, (D_MODEL, LANES), 1)
            wdt_ref[...] = jnp.where(lane < N_HEADS, block.T, 0.0).astype(BF16)

        def feature_rows(f0):
            return lambda r, n: wint_hbm.at[layer, pl.ds(f0 + r, n), :]

        for src, dst, width in ((feature_rows(0), wz_ref, D_SSD), (feature_rows(D_SSD), wxbc_ref, D_XBC),
                                (feature_rows(DT_COL + N_HEADS), wu_ref, D_POOL)):
            for r in range(0, width, w_rows):
                stream.add(src(r, w_rows), transposed_into(dst, r))
        stream.add(feature_rows(DT_COL)(0, LANES), dt_into)
        for r in range(0, D_SSD + D_POOL, w_rows):
            stream.add(wout_hbm.at[layer, pl.ds(r, w_rows), :], cast_into(wout_ref, r, 0))
        stream.prime()
        poolw_ref[...] = poolw_in[...]
        stream.drain()
        raw_s[:, 0:CONV_PAD, :] = jnp.zeros((D_XBC // LANES, CONV_PAD, LANES), F32)
        u_s[:, 0:POOL_PAD, :] = jnp.zeros((D_POOL // LANES, POOL_PAD, LANES), F32)
        for c in range(n_chunks):
            for run in proj_pieces(c, in_norm(xc_ref[rows(c), :], modc_ref)):
                run()

    @pl.when(tile_in_seq == 0)
    def _():
        st_s[...] = jnp.zeros(st_s.shape, F32)

    gt = modc_ref[0, 5:6, :]
    a_neg = -jnp.exp(alog_ref[...])
    ri = lax.broadcasted_iota(jnp.int32, (CHUNK, CHUNK), 0)
    ci = lax.broadcasted_iota(jnp.int32, (CHUNK, CHUNK), 1)
    causal = ri >= ci
    tril = jnp.where(causal, 1.0, 0.0).astype(BF16)
    low_half = ci < HEAD_DIM
    low_half_row = lax.broadcasted_iota(jnp.int32, (1, LANES), 1) < HEAD_DIM

    def pre(c, seq_tile):
        rs = rows(c)
        last = c == n_chunks - 1

        def carry(hist):
            return jnp.where(next_starts_seq, jnp.zeros_like(hist), hist) if last else hist

        base, nbase = conv_base(c), conv_base(c + 1)
        for j in range(D_XBC // LANES):
            cols = slice(j * LANES, (j + 1) * LANES)
            xc = convb_ref[:, cols]
            for k in range(D_CONV):
                xc = xc + convw_ref[k:k + 1, cols] * raw_s[j, pl.ds(base - (D_CONV - 1) + k, CHUNK), :]
            raw_s[j, nbase - CONV_PAD:nbase, :] = carry(raw_s[j, base + CHUNK - CONV_PAD:base + CHUNK, :])
            xbc_s[rs, cols] = _silu(xc)
            if j % 2 == 1:
                emit(1)
        base, nbase = pool_base(c), pool_base(c + 1)
        diffs = []
        for gi, w in enumerate(POOL_WINDOWS):
            halves = []
            for j in range(gi * POOL_GROUP_DIM // LANES, (gi + 1) * POOL_GROUP_DIM // LANES):
                u_g = u_s[j, base:base + CHUNK, :]
                s = u_g
                for k in range(1, w):
                    s = s + u_s[j, pl.ds(base - k, CHUNK), :]
                if c == 0:
                    pos = seq_tile * MIX_ROWS + 1 + lax.broadcasted_iota(jnp.int32, (CHUNK, LANES), 0)
                    pooled = s / jnp.minimum(pos, w).astype(F32)
                else:
                    pooled = s * (1.0 / w)
                halves.append(pooled - u_g)
                u_s[j, nbase - POOL_PAD:nbase, :] = carry(u_s[j, base + CHUNK - POOL_PAD:base + CHUNK, :])
            diffs.append(jnp.concatenate(halves, axis=1).astype(BF16))
            emit(1)
        zs = _silu(z_s[rs, :])
        dt_c = jax.nn.softplus(dtr_s[rs, :] + dtb_ref[...])
        emit()
        return diffs, zs, dt_c

    def ssd(c, zs, dt_c):
        rs = rows(c)
        p_hi, p_mid, p_lo = _split3(dt_c * a_neg)
        cs = _dot(tril, p_hi) + _dot(tril, p_mid) + _dot(tril, p_lo)
        cs_t = cs.T
        dt_t = dt_c.T
        w_t = dt_t * jnp.exp(cs_t[:, CHUNK - 1:CHUNK] - cs_t)
        xs_c = xbc_s[rs, 0:D_SSD]
        emit(1)
        y_pairs = []
        for g in range(N_GROUPS):
            b0 = D_SSD + g * D_STATE
            c0 = D_SSD + N_GROUPS * D_STATE + g * D_STATE
            b_t = xbc_s[rs, b0:b0 + D_STATE].T
            c_g = xbc_s[rs, c0:c0 + D_STATE]
            cb = _dot(c_g.astype(BF16), b_t.astype(BF16))
            for pr in range(HEADS_PER_GROUP // 2):
                lhs, bw_t, cd = [], [], []
                for r in (2 * pr, 2 * pr + 1):
                    hd = g * HEADS_PER_GROUP + r
                    cs_bc = jnp.broadcast_to(cs[:, hd:hd + 1], (CHUNK, CHUNK))
                    decay = jnp.exp(jnp.where(causal, cs_bc - cs_t[hd:hd + 1, :], -jnp.inf))
                    scores = decay * (cb * dt_t[hd:hd + 1, :])
                    sd = jnp.exp(cs_bc)
                    lhs.append(jnp.concatenate([scores, c_g * sd], axis=1).astype(BF16))
                    bw_t.append((b_t * w_t[hd:hd + 1, :]).astype(BF16))
                    cd.append(sd[CHUNK - 1:CHUNK, :])
                col = (g * HEADS_PER_GROUP + 2 * pr) * HEAD_DIM
                x_p = xs_c[:, col:col + LANES]
                st = st_s[:, col:col + LANES]
                rhs = jnp.concatenate([x_p, st], axis=0).astype(BF16)
                y_pairs.append(jnp.where(low_half, _dot(lhs[0], rhs), _dot(lhs[1], rhs)))
                zero = jnp.zeros_like(x_p)
                x_split = jnp.concatenate(
                    [jnp.where(low_half, x_p, zero), jnp.where(low_half, zero, x_p)], axis=0)
                upd = _dot(jnp.concatenate(bw_t, axis=1), x_split.astype(BF16))
                st_s[:, col:col + LANES] = st * jnp.where(low_half_row, cd[0], cd[1]) + upd
                emit(1)
        y = jnp.concatenate(y_pairs, axis=1) + dskip_ref[...] * xs_c
        yz = y * zs
        gw = D_SSD // N_GROUPS
        parts = []
        for g in range(N_GROUPS):
            blk = yz[:, g * gw:(g + 1) * gw]
            ms = jnp.mean(blk * blk, axis=-1, keepdims=True)
            parts.append(blk * lax.rsqrt(ms + EPS))
        return (jnp.concatenate(parts, axis=1) * ssdnw_ref[...]).astype(BF16)

    def post_pieces(c, y_ssd, diffs):
        rs = rows(c)
        outs = [None] * len(POOL_WINDOWS)

        def pool_piece(gi):
            def run():
                lo, hi = gi * POOL_GROUP_DIM, (gi + 1) * POOL_GROUP_DIM
                o = (_dot(diffs[gi], poolw_ref[gi]) + poolb_ref[:, lo:hi]) * pools_ref[:, lo:hi]
                outs[gi] = o.astype(BF16)
            return run

        def out_piece(lo, hi):
            def run():
                y_pool = jnp.concatenate(outs, axis=1)
                out = (_dot(y_ssd, wout_ref[0:D_SSD, lo:hi])
                       + _dot(y_pool, wout_ref[D_SSD:D_SSD + D_POOL, lo:hi]))
                o_ref[rs, lo:hi] = xc_ref[rs, lo:hi] + gt[:, lo:hi] * out
            return run

        return ([pool_piece(gi) for gi in range(len(POOL_WINDOWS))]
                + [out_piece(k, k + MXU_COLS) for k in range(0, D_MODEL, MXU_COLS)])

    def hand_over(vals):
        diffs, zs, dt_c = vals
        for gi in range(len(POOL_WINDOWS)):
            diff_s[:, gi * POOL_GROUP_DIM:(gi + 1) * POOL_GROUP_DIM] = diffs[gi]
        zs_s[...] = zs
        dtc_s[...] = dt_c

    @pl.when(step == 0)
    def _():
        hand_over(pre(0, tile_in_seq))

    diffs = [diff_s[:, gi * POOL_GROUP_DIM:(gi + 1) * POOL_GROUP_DIM] for gi in range(len(POOL_WINDOWS))]
    zs, dt_c = zs_s[...], dtc_s[...]
    h_n = in_norm(xn_ref[rows(0), :], modn_ref)
    pending.extend(proj_pieces(0, h_n))
    for c in range(n_chunks):
        y_ssd = ssd(c, zs, dt_c)
        pending.extend(post_pieces(c, y_ssd, diffs))
        if c + 1 < n_chunks:
            h_n = in_norm(xn_ref[rows(c + 1), :], modn_ref)
            emit(2)
            diffs, zs, dt_c = pre(c + 1, tile_in_seq)
            pending.extend(proj_pieces(c + 1, h_n))
        else:
            hand_over(pre(0, (step + 1) % tiles_per_seq))
            emit()


def _mixer(x, mod3, norm_w, w_in_t, convw, convb, dtb, alog, dskip, ssdnw,
           poolw, poolb, pools, w_out, *, layer):
    bsz, seq, _ = x.shape
    tiles_per_seq = seq // MIX_ROWS
    n_chunks = MIX_ROWS // CHUNK
    steps = bsz * tiles_per_seq
    nxt = lambda s: jnp.minimum(s + 1, steps - 1)
    hbm = pl.BlockSpec(memory_space=pl.ANY)
    vmem_consts = [convw, convb, dtb, alog, dskip, ssdnw, poolw, poolb, pools]
    x2 = x.reshape(bsz * seq, D_MODEL)
    out = pl.pallas_call(
        functools.partial(_mixer_kernel, layer=layer, tiles_per_seq=tiles_per_seq),
        grid=(steps,),
        in_specs=[
            pl.BlockSpec((MIX_ROWS, D_MODEL), lambda s: (s, 0)),
            pl.BlockSpec((MIX_ROWS, D_MODEL), lambda s: (nxt(s), 0)),
            pl.BlockSpec((1, N_MOD, D_MODEL), lambda s: (s // tiles_per_seq, 0, 0)),
            pl.BlockSpec((1, N_MOD, D_MODEL), lambda s: (nxt(s) // tiles_per_seq, 0, 0)),
            _const_spec(norm_w.shape), hbm,
        ] + [_const_spec(a.shape) for a in vmem_consts] + [hbm],
        out_specs=pl.BlockSpec((MIX_ROWS, D_MODEL), lambda s: (s, 0)),
        out_shape=jax.ShapeDtypeStruct((bsz * seq, D_MODEL), F32),
        scratch_shapes=[
            pltpu.VMEM((D_MODEL, D_SSD), BF16),
            pltpu.VMEM((D_MODEL, D_XBC), BF16),
            pltpu.VMEM((D_MODEL, LANES), BF16),
            pltpu.VMEM((D_MODEL, D_POOL), BF16),
            pltpu.VMEM(poolw.shape, BF16),
            pltpu.VMEM((D_SSD + D_POOL, D_MODEL), BF16),
            pltpu.VMEM((W_SLOTS, MIX_W_ROWS, D_MODEL), F32),
            pltpu.SemaphoreType.DMA((W_SLOTS,)),
            pltpu.VMEM((MIX_ROWS, D_SSD), F32),
            pltpu.VMEM((D_XBC // LANES, n_chunks * (CONV_PAD + CHUNK), LANES), F32),
            pltpu.VMEM((MIX_ROWS, LANES), F32),
            pltpu.VMEM((D_POOL // LANES, n_chunks * (POOL_PAD + CHUNK), LANES), F32),
            pltpu.VMEM((MIX_ROWS, D_XBC), F32),
            pltpu.VMEM((D_STATE, D_SSD), F32),
            pltpu.VMEM((CHUNK, D_POOL), BF16),
            pltpu.VMEM((CHUNK, D_SSD), F32),
            pltpu.VMEM((CHUNK, LANES), F32),
        ],
        compiler_params=pltpu.CompilerParams(
            dimension_semantics=("arbitrary",), vmem_limit_bytes=VMEM_LIMIT_BYTES),
        name="mixer",
    )(x2, x2, mod3, mod3, norm_w, w_in_t, *vmem_consts, w_out)
    return out.reshape(bsz, seq, D_MODEL)


def _pad_lanes(v):
    return jnp.pad(v.reshape(1, -1), ((0, 0), (0, LANES - v.shape[0])))


def kernel(x, c, w_ada, b_ada, ffn1_norm, ffn1_w_gate, ffn1_w_up, ffn1_w_down, mix_norm, w_in, conv_w, conv_b, dt_bias, a_log, d_skip, ssd_norm_w, pool_w, pool_b, pool_scale, w_out, ffn2_norm, ffn2_w_gate, ffn2_w_up, ffn2_w_down, final_norm):
    bsz = x.shape[0]
    depth = w_ada.shape[0]
    row = lambda v: v.reshape(1, -1)
    for i in range(depth):
        mod3 = _adaln(c, w_ada[i], row(b_ada[i])).reshape(bsz, N_MOD, D_MODEL)
        last = i == depth - 1

        x = _ffn(x, mod3, row(ffn1_norm[i]), ffn1_w_gate, ffn1_w_up, ffn1_w_down, row(final_norm),
                 layer=i, mod_idx=0, final=False)

        x = _mixer(
            x, mod3, row(mix_norm[i]), jnp.swapaxes(w_in, 1, 2),
            conv_w[i], row(conv_b[i]), _pad_lanes(dt_bias[i]), _pad_lanes(a_log[i]),
            row(jnp.repeat(d_skip[i], HEAD_DIM)), row(ssd_norm_w[i]),
            pool_w[i].astype(BF16), row(pool_b[i]), row(pool_scale[i]), w_out, layer=i)

        x = _ffn(x, mod3, row(ffn2_norm[i]), ffn2_w_gate, ffn2_w_up, ffn2_w_down, row(final_norm),
                 layer=i, mod_idx=6, final=last)
    return x
```

```python
import functools

import jax
import jax.numpy as jnp
from jax import lax
from jax.experimental import pallas as pl
from jax.experimental.pallas import tpu as pltpu

F32 = jnp.float32
BF16 = jnp.bfloat16

D_MODEL = 1024
D_FF = 2816
N_MOD = 9
FFN_RES = 0.5
EPS = 1e-6
D_SSD = 1024
HEAD_DIM = 64
N_HEADS = 16
N_GROUPS = 4
HEADS_PER_GROUP = N_HEADS // N_GROUPS
D_STATE = 128
D_CONV = 4
CHUNK = 128
D_POOL = 1024
POOL_WINDOWS = (2, 4, 8, 16)
POOL_GROUP_DIM = 256
D_XBC = D_SSD + 2 * N_GROUPS * D_STATE
DT_COL = D_SSD + D_XBC

LANES = 128
SUBLANES = 8
MXU_COLS = 256
VMEM_LIMIT_BYTES = 60 * 1024 * 1024

FFN_ROWS = 512
FFN_SUBTILES = 2
NORM_SLICES = 8
FFN_W_ROWS_OUT = 256
MIX_W_ROWS = 256
W_SLOTS = 8
MIX_ROWS = 512
ADA_COLS = 1024
CONV_PAD = SUBLANES
POOL_PAD = 16

assert HEAD_DIM * 2 == LANES and D_STATE == LANES and CHUNK == LANES


def _dot(a, b):
    return jnp.dot(a, b, preferred_element_type=F32)


def _split3(v):
    hi = v.astype(BF16)
    r1 = v - hi.astype(F32)
    mid = r1.astype(BF16)
    lo = (r1 - mid.astype(F32)).astype(BF16)
    return hi, mid, lo


def _silu(v):
    return v * jax.nn.sigmoid(v)


def _rms(v, w):
    ms = jnp.mean(v * v, axis=-1, keepdims=True)
    return v * lax.rsqrt(ms + EPS) * w


def _const_spec(shape):
    zeros = (0,) * len(shape)
    return pl.BlockSpec(shape, lambda *_: zeros, pipeline_mode=pl.Buffered(1))


def _adaln_kernel(c_ref, w_ref, b_ref, o_ref):
    a_hi, a_mid, _ = _split3(_silu(c_ref[...]))
    w_hi, w_mid, _ = _split3(w_ref[...])
    acc = _dot(a_hi, w_hi) + _dot(a_mid, w_hi) + _dot(a_hi, w_mid)
    o_ref[...] = acc + b_ref[...]


def _adaln(c, w_ada, b_ada):
    bsz = c.shape[0]
    n = w_ada.shape[1]
    c_pad = jnp.pad(c, ((0, SUBLANES - bsz), (0, 0)))
    mod = pl.pallas_call(
        _adaln_kernel,
        grid=(n // ADA_COLS,),
        in_specs=[
            pl.BlockSpec((SUBLANES, D_MODEL), lambda j: (0, 0)),
            pl.BlockSpec((D_MODEL, ADA_COLS), lambda j: (0, j)),
            pl.BlockSpec((1, ADA_COLS), lambda j: (0, j)),
        ],
        out_specs=pl.BlockSpec((SUBLANES, ADA_COLS), lambda j: (0, j)),
        out_shape=jax.ShapeDtypeStruct((SUBLANES, n), F32),
        compiler_params=pltpu.CompilerParams(
            dimension_semantics=("arbitrary",), vmem_limit_bytes=VMEM_LIMIT_BYTES),
        name="adaln",
    )(c_pad, w_ada, b_ada)
    return mod[:bsz]


class _BlockStream:
    def __init__(self, stage_ref, sem):
        self.stage, self.sem, self.blocks = stage_ref, sem, []
        self.n_slots = stage_ref.shape[0]
        self.done = 0

    def add(self, src, sink):
        self.blocks.append((src, sink))

    def _slot(self, i):
        return self.stage.at[i % self.n_slots, pl.ds(0, self.blocks[i][0].shape[0])]

    def _copy(self, i):
        return pltpu.make_async_copy(self.blocks[i][0], self._slot(i), self.sem.at[i % self.n_slots])

    def prime(self):
        for i in range(min(self.n_slots, len(self.blocks))):
            self._copy(i).start()

    def advance(self, n):
        for i in range(self.done, self.done + n):
            self._copy(i).wait()
            self.blocks[i][1](self._slot(i)[...])
            if i + self.n_slots < len(self.blocks):
                self._copy(i + self.n_slots).start()
        self.done += n

    def drain(self):
        self.advance(len(self.blocks) - self.done)


def _stream_rows_bf16(stream, w_hbm, layer, dst_ref):
    rows = stream.stage.shape[1]

    def sink(r):
        def run(block):
            dst_ref[r:r + rows, :] = block.astype(BF16)
        return run

    for r in range(0, dst_ref.shape[0], rows):
        stream.add(w_hbm.at[layer, pl.ds(r, rows), :], sink(r))


def _stream_cols_bf16(stream, w_hbm, layer, dst_ref, c0):
    cols = stream.stage.shape[2]

    def run(block):
        dst_ref[:, c0:c0 + cols] = block.astype(BF16)

    stream.add(w_hbm.at[layer, :, pl.ds(c0, cols)], run)


def _ffn_kernel(xc_ref, xn_ref, modc_ref, modn_ref, nw_ref, wg_hbm, wu_hbm, wd_hbm, fw_ref, o_ref,
                h_s, wg_ref, wu_ref, wd_ref, stage_in, stage_out, sem_in, sem_out,
                *, layer, mod_idx, final):
    def in_norm(x, mod_ref):
        sh = mod_ref[0, pl.ds(mod_idx, 1), :]
        sc = mod_ref[0, pl.ds(mod_idx + 1, 1), :]
        return (_rms(x, nw_ref[...]) * (1.0 + sc) + sh).astype(BF16)

    step = pl.program_id(0)
    cur = h_s.at[step % 2]
    nxt = h_s.at[(step + 1) % 2]

    n_chunks = D_FF // MXU_COLS
    slice_rows = FFN_ROWS // NORM_SLICES
    sub_rows = FFN_ROWS // FFN_SUBTILES
    slices_per_sub = NORM_SLICES // FFN_SUBTILES

    def body(first):
        if first:
            s_in, s_out = _BlockStream(stage_in, sem_in), _BlockStream(stage_out, sem_out)
            for k in range(n_chunks):
                _stream_cols_bf16(s_in, wg_hbm, layer, wg_ref, k * MXU_COLS)
                _stream_cols_bf16(s_in, wu_hbm, layer, wu_ref, k * MXU_COLS)
            _stream_rows_bf16(s_out, wd_hbm, layer, wd_ref)
            s_in.prime()
            s_out.prime()
            cur[...] = in_norm(xc_ref[...], modc_ref)

        gt = modc_ref[0, pl.ds(mod_idx + 2, 1), :]
        for sub in range(FFN_SUBTILES):
            rows = slice(sub * sub_rows, (sub + 1) * sub_rows)
            acts = []
            for k in range(n_chunks):
                if first and sub == 0:
                    s_in.advance(2)
                if k < slices_per_sub:
                    j = sub * slices_per_sub + k
                    rs = slice(j * slice_rows, (j + 1) * slice_rows)
                    nxt[rs, :] = in_norm(xn_ref[rs, :], modn_ref)
                cols = slice(k * MXU_COLS, (k + 1) * MXU_COLS)
                h = cur[rows, :]
                g = _dot(h, wg_ref[:, cols])
                u = _dot(h, wu_ref[:, cols])
                acts.append((_silu(g) * u).astype(BF16))
            if first and sub == 0:
                s_out.drain()
            a = jnp.concatenate(acts, axis=1)
            o = _dot(a, wd_ref[...])
            out = xc_ref[rows, :] + (FFN_RES * gt) * o
            if final:
                out = _rms(out, fw_ref[...])
            o_ref[rows, :] = out

    @pl.when(step == 0)
    def _():
        body(True)

    @pl.when(step > 0)
    def _():
        body(False)


def _ffn(x, mod3, norm_w, wg, wu, wd, final_w, *, layer, mod_idx, final):
    bsz, seq, _ = x.shape
    tiles_per_seq = seq // FFN_ROWS
    steps = bsz * tiles_per_seq
    nxt = lambda s: jnp.minimum(s + 1, steps - 1)
    x2 = x.reshape(bsz * seq, D_MODEL)
    hbm = pl.BlockSpec(memory_space=pl.ANY)
    out = pl.pallas_call(
        functools.partial(_ffn_kernel, layer=layer, mod_idx=mod_idx, final=final),
        grid=(steps,),
        in_specs=[
            pl.BlockSpec((FFN_ROWS, D_MODEL), lambda s: (s, 0)),
            pl.BlockSpec((FFN_ROWS, D_MODEL), lambda s: (nxt(s), 0)),
            pl.BlockSpec((1, N_MOD, D_MODEL), lambda s: (s // tiles_per_seq, 0, 0)),
            pl.BlockSpec((1, N_MOD, D_MODEL), lambda s: (nxt(s) // tiles_per_seq, 0, 0)),
            _const_spec((1, D_MODEL)),
            hbm, hbm, hbm,
            _const_spec((1, D_MODEL)),
        ],
        out_specs=pl.BlockSpec((FFN_ROWS, D_MODEL), lambda s: (s, 0)),
        out_shape=jax.ShapeDtypeStruct((bsz * seq, D_MODEL), F32),
        scratch_shapes=[
            pltpu.VMEM((2, FFN_ROWS, D_MODEL), BF16),
            pltpu.VMEM((D_MODEL, D_FF), BF16),
            pltpu.VMEM((D_MODEL, D_FF), BF16),
            pltpu.VMEM((D_FF, D_MODEL), BF16),
            pltpu.VMEM((W_SLOTS, D_MODEL, MXU_COLS), F32),
            pltpu.VMEM((W_SLOTS, FFN_W_ROWS_OUT, D_MODEL), F32),
            pltpu.SemaphoreType.DMA((W_SLOTS,)),
            pltpu.SemaphoreType.DMA((W_SLOTS,)),
        ],
        compiler_params=pltpu.CompilerParams(
            dimension_semantics=("arbitrary",), vmem_limit_bytes=VMEM_LIMIT_BYTES),
        name="ffn_final" if final else "ffn",
    )(x2, x2, mod3, mod3, norm_w, wg, wu, wd, final_w)
    return out.reshape(bsz, seq, D_MODEL)


def _mixer_kernel(xc_ref, xn_ref, modc_ref, modn_ref, nw_ref, wint_hbm,
                  convw_ref, convb_ref, dtb_ref, alog_ref, dskip_ref, ssdnw_ref,
                  poolw_in, poolb_ref, pools_ref, wout_hbm,
                  o_ref,
                  wz_ref, wxbc_ref, wdt_ref, wu_ref, poolw_ref, wout_ref, stage, sem,
                  z_s, raw_s, dtr_s, u_s, xbc_s, st_s, diff_s, zs_s, dtc_s, *, layer, tiles_per_seq):
    step = pl.program_id(0)
    tile_in_seq = step % tiles_per_seq
    next_starts_seq = tile_in_seq == tiles_per_seq - 1
    n_chunks = MIX_ROWS // CHUNK

    def in_norm(x, mod_ref):
        sh = mod_ref[0, 3:4, :]
        sc = mod_ref[0, 4:5, :]
        return (_rms(x, nw_ref[...]) * (1.0 + sc) + sh).astype(BF16)

    def rows(c):
        return slice(c * CHUNK, (c + 1) * CHUNK)

    def conv_base(c):
        return (c % n_chunks) * (CONV_PAD + CHUNK) + CONV_PAD

    def pool_base(c):
        return (c % n_chunks) * (POOL_PAD + CHUNK) + POOL_PAD

    pending = []

    def emit(k=None):
        n = len(pending) if k is None else min(k, len(pending))
        for _ in range(n):
            pending.pop(0)()

    def proj_pieces(c, h_n):
        rs = rows(c)

        def dense(dst, w_ref, lo, hi):
            def run():
                dst[rs, lo:hi] = _dot(h_n, w_ref[:, lo:hi])
            return run

        def slabs(dst, base, w_ref, lo, hi):
            def run():
                res = _dot(h_n, w_ref[:, lo:hi])
                for j in range(lo // LANES, hi // LANES):
                    dst[j, base:base + CHUNK, :] = res[:, j * LANES - lo:(j + 1) * LANES - lo]
            return run

        out = [slabs(raw_s, conv_base(c), wxbc_ref, k, k + MXU_COLS) for k in range(0, D_XBC, MXU_COLS)]
        out += [slabs(u_s, pool_base(c), wu_ref, k, k + MXU_COLS) for k in range(0, D_POOL, MXU_COLS)]
        out += [dense(z_s, wz_ref, k, k + MXU_COLS) for k in range(0, D_SSD, MXU_COLS)]
        out.append(dense(dtr_s, wdt_ref, 0, LANES))
        return out

    @pl.when(step == 0)
    def _():
        stream = _BlockStream(stage, sem)
        w_rows, w_cols = stage.shape[1], stage.shape[2]

        def cast_into(dst, r, c0):
            def run(block):
                dst[r:r + w_rows, c0:c0 + w_cols] = block.astype(BF16)
            return run

        def transposed_into(dst, c0):
            def run(block):
                dst[:, c0:c0 + block.shape[0]] = block.T.astype(BF16)
            return run

        def dt_into(block):
            lane = lax.broadcasted_iota(jnp.int32, (D_MODEL, LANES), 1)
            wdt_ref[...] = jnp.where(lane < N_HEADS, block.T, 0.0).astype(BF16)

        def feature_rows(f0):
            return lambda r, n: wint_hbm.at[layer, pl.ds(f0 + r, n), :]

        for src, dst, width in ((feature_rows(0), wz_ref, D_SSD), (feature_rows(D_SSD), wxbc_ref, D_XBC),
                                (feature_rows(DT_COL + N_HEADS), wu_ref, D_POOL)):
            for r in range(0, width, w_rows):
                stream.add(src(r, w_rows), transposed_into(dst, r))
        stream.add(feature_rows(DT_COL)(0, LANES), dt_into)
        for r in range(0, D_SSD + D_POOL, w_rows):
            stream.add(wout_hbm.at[layer, pl.ds(r, w_rows), :], cast_into(wout_ref, r, 0))
        stream.prime()
        poolw_ref[...] = poolw_in[...]
        raw_s[:, 0:CONV_PAD, :] = jnp.zeros((D_XBC // LANES, CONV_PAD, LANES), F32)
        u_s[:, 0:POOL_PAD, :] = jnp.zeros((D_POOL // LANES, POOL_PAD, LANES), F32)
        pieces = [proj_pieces(c, in_norm(xc_ref[rows(c), :], modc_ref)) for c in range(n_chunks)]
        n_raw, n_u, n_z = D_XBC // MXU_COLS, D_POOL // MXU_COLS, D_SSD // MXU_COLS

        def run(lo, hi):
            for chunk_pieces in pieces:
                for piece in chunk_pieces[lo:hi]:
                    piece()

        stream.advance(D_SSD // w_rows)
        run(n_raw + n_u, n_raw + n_u + n_z)
        stream.advance(D_XBC // w_rows)
        run(0, n_raw)
        stream.advance(D_POOL // w_rows)
        run(n_raw, n_raw + n_u)
        stream.advance(1)
        run(n_raw + n_u + n_z, n_raw + n_u + n_z + 1)
        stream.drain()

    @pl.when(tile_in_seq == 0)
    def _():
        st_s[...] = jnp.zeros(st_s.shape, F32)

    gt = modc_ref[0, 5:6, :]
    a_neg = -jnp.exp(alog_ref[...])
    ri = lax.broadcasted_iota(jnp.int32, (CHUNK, CHUNK), 0)
    ci = lax.broadcasted_iota(jnp.int32, (CHUNK, CHUNK), 1)
    causal = ri >= ci
    tril = jnp.where(causal, 1.0, 0.0).astype(BF16)
    low_half = ci < HEAD_DIM
    low_half_row = lax.broadcasted_iota(jnp.int32, (1, LANES), 1) < HEAD_DIM

    def pre(c, seq_tile):
        rs = rows(c)
        last = c == n_chunks - 1

        def carry(hist):
            return jnp.where(next_starts_seq, jnp.zeros_like(hist), hist) if last else hist

        base, nbase = conv_base(c), conv_base(c + 1)
        for j in range(D_XBC // LANES):
            cols = slice(j * LANES, (j + 1) * LANES)
            xc = convb_ref[:, cols]
            for k in range(D_CONV):
                xc = xc + convw_ref[k:k + 1, cols] * raw_s[j, pl.ds(base - (D_CONV - 1) + k, CHUNK), :]
            raw_s[j, nbase - CONV_PAD:nbase, :] = carry(raw_s[j, base + CHUNK - CONV_PAD:base + CHUNK, :])
            xbc_s[rs, cols] = _silu(xc)
            if j % 2 == 1:
                emit(1)
        base, nbase = pool_base(c), pool_base(c + 1)
        diffs = []
        for gi, w in enumerate(POOL_WINDOWS):
            halves = []
            for j in range(gi * POOL_GROUP_DIM // LANES, (gi + 1) * POOL_GROUP_DIM // LANES):
                u_g = u_s[j, base:base + CHUNK, :]
                s = u_g
                for k in range(1, w):
                    s = s + u_s[j, pl.ds(base - k, CHUNK), :]
                if c == 0:
                    pos = seq_tile * MIX_ROWS + 1 + lax.broadcasted_iota(jnp.int32, (CHUNK, LANES), 0)
                    pooled = s / jnp.minimum(pos, w).astype(F32)
                else:
                    pooled = s * (1.0 / w)
                halves.append(pooled - u_g)
                u_s[j, nbase - POOL_PAD:nbase, :] = carry(u_s[j, base + CHUNK - POOL_PAD:base + CHUNK, :])
            diffs.append(jnp.concatenate(halves, axis=1).astype(BF16))
            emit(1)
        zs = _silu(z_s[rs, :])
        dt_c = jax.nn.softplus(dtr_s[rs, :] + dtb_ref[...])
        emit()
        return diffs, zs, dt_c

    def ssd(c, zs, dt_c):
        rs = rows(c)
        p_hi, p_mid, p_lo = _split3(dt_c * a_neg)
        cs = _dot(tril, p_hi) + _dot(tril, p_mid) + _dot(tril, p_lo)
        cs_t = cs.T
        dt_t = dt_c.T
        w_t = dt_t * jnp.exp(cs_t[:, CHUNK - 1:CHUNK] - cs_t)
        xs_c = xbc_s[rs, 0:D_SSD]
        emit(1)
        y_pairs = []
        for g in range(N_GROUPS):
            b0 = D_SSD + g * D_STATE
            c0 = D_SSD + N_GROUPS * D_STATE + g * D_STATE
            b_t = xbc_s[rs, b0:b0 + D_STATE].T
            c_g = xbc_s[rs, c0:c0 + D_STATE]
            cb = _dot(c_g.astype(BF16), b_t.astype(BF16))
            for pr in range(HEADS_PER_GROUP // 2):
                lhs, bw_t, cd = [], [], []
                for r in (2 * pr, 2 * pr + 1):
                    hd = g * HEADS_PER_GROUP + r
                    cs_bc = jnp.broadcast_to(cs[:, hd:hd + 1], (CHUNK, CHUNK))
                    decay = jnp.exp(jnp.where(causal, cs_bc - cs_t[hd:hd + 1, :], -jnp.inf))
                    scores = decay * (cb * dt_t[hd:hd + 1, :])
                    sd = jnp.exp(cs_bc)
                    lhs.append(jnp.concatenate([scores, c_g * sd], axis=1).astype(BF16))
                    bw_t.append((b_t * w_t[hd:hd + 1, :]).astype(BF16))
                    cd.append(sd[CHUNK - 1:CHUNK, :])
                col = (g * HEADS_PER_GROUP + 2 * pr) * HEAD_DIM
                x_p = xs_c[:, col:col + LANES]
                st = st_s[:, col:col + LANES]
                rhs = jnp.concatenate([x_p, st], axis=0).astype(BF16)
                y_pairs.append(jnp.where(low_half, _dot(lhs[0], rhs), _dot(lhs[1], rhs)))
                zero = jnp.zeros_like(x_p)
                x_split = jnp.concatenate(
                    [jnp.where(low_half, x_p, zero), jnp.where(low_half, zero, x_p)], axis=0)
                upd = _dot(jnp.concatenate(bw_t, axis=1), x_split.astype(BF16))
                st_s[:, col:col + LANES] = st * jnp.where(low_half_row, cd[0], cd[1]) + upd
                emit(1)
        y = jnp.concatenate(y_pairs, axis=1) + dskip_ref[...] * xs_c
        yz = y * zs
        gw = D_SSD // N_GROUPS
        parts = []
        for g in range(N_GROUPS):
            blk = yz[:, g * gw:(g + 1) * gw]
            ms = jnp.mean(blk * blk, axis=-1, keepdims=True)
            parts.append(blk * lax.rsqrt(ms + EPS))
        return (jnp.concatenate(parts, axis=1) * ssdnw_ref[...]).astype(BF16)

    def post_pieces(c, y_ssd, diffs):
        rs = rows(c)
        outs = [None] * len(POOL_WINDOWS)

        def pool_piece(gi):
            def run():
                lo, hi = gi * POOL_GROUP_DIM, (gi + 1) * POOL_GROUP_DIM
                o = (_dot(diffs[gi], poolw_ref[gi]) + poolb_ref[:, lo:hi]) * pools_ref[:, lo:hi]
                outs[gi] = o.astype(BF16)
            return run

        def out_piece(lo, hi):
            def run():
                y_pool = jnp.concatenate(outs, axis=1)
                out = (_dot(y_ssd, wout_ref[0:D_SSD, lo:hi])
                       + _dot(y_pool, wout_ref[D_SSD:D_SSD + D_POOL, lo:hi]))
                o_ref[rs, lo:hi] = xc_ref[rs, lo:hi] + gt[:, lo:hi] * out
            return run

        return ([pool_piece(gi) for gi in range(len(POOL_WINDOWS))]
                + [out_piece(k, k + MXU_COLS) for k in range(0, D_MODEL, MXU_COLS)])

    def hand_over(vals):
        diffs, zs, dt_c = vals
        for gi in range(len(POOL_WINDOWS)):
            diff_s[:, gi * POOL_GROUP_DIM:(gi + 1) * POOL_GROUP_DIM] = diffs[gi]
        zs_s[...] = zs
        dtc_s[...] = dt_c

    @pl.when(step == 0)
    def _():
        hand_over(pre(0, tile_in_seq))

    diffs = [diff_s[:, gi * POOL_GROUP_DIM:(gi + 1) * POOL_GROUP_DIM] for gi in range(len(POOL_WINDOWS))]
    zs, dt_c = zs_s[...], dtc_s[...]
    h_n = in_norm(xn_ref[rows(0), :], modn_ref)
    pending.extend(proj_pieces(0, h_n))
    for c in range(n_chunks):
        y_ssd = ssd(c, zs, dt_c)
        pending.extend(post_pieces(c, y_ssd, diffs))
        if c + 1 < n_chunks:
            h_n = in_norm(xn_ref[rows(c + 1), :], modn_ref)
            emit(2)
            diffs, zs, dt_c = pre(c + 1, tile_in_seq)
            pending.extend(proj_pieces(c + 1, h_n))
        else:
            hand_over(pre(0, (step + 1) % tiles_per_seq))
            emit()


def _mixer(x, mod3, norm_w, w_in_t, convw, convb, dtb, alog, dskip, ssdnw,
           poolw, poolb, pools, w_out, *, layer):
    bsz, seq, _ = x.shape
    tiles_per_seq = seq // MIX_ROWS
    n_chunks = MIX_ROWS // CHUNK
    steps = bsz * tiles_per_seq
    nxt = lambda s: jnp.minimum(s + 1, steps - 1)
    hbm = pl.BlockSpec(memory_space=pl.ANY)
    vmem_consts = [convw, convb, dtb, alog, dskip, ssdnw, poolw, poolb, pools]
    x2 = x.reshape(bsz * seq, D_MODEL)
    out = pl.pallas_call(
        functools.partial(_mixer_kernel, layer=layer, tiles_per_seq=tiles_per_seq),
        grid=(steps,),
        in_specs=[
            pl.BlockSpec((MIX_ROWS, D_MODEL), lambda s: (s, 0)),
            pl.BlockSpec((MIX_ROWS, D_MODEL), lambda s: (nxt(s), 0)),
            pl.BlockSpec((1, N_MOD, D_MODEL), lambda s: (s // tiles_per_seq, 0, 0)),
            pl.BlockSpec((1, N_MOD, D_MODEL), lambda s: (nxt(s) // tiles_per_seq, 0, 0)),
            _const_spec(norm_w.shape), hbm,
        ] + [_const_spec(a.shape) for a in vmem_consts] + [hbm],
        out_specs=pl.BlockSpec((MIX_ROWS, D_MODEL), lambda s: (s, 0)),
        out_shape=jax.ShapeDtypeStruct((bsz * seq, D_MODEL), F32),
        scratch_shapes=[
            pltpu.VMEM((D_MODEL, D_SSD), BF16),
            pltpu.VMEM((D_MODEL, D_XBC), BF16),
            pltpu.VMEM((D_MODEL, LANES), BF16),
            pltpu.VMEM((D_MODEL, D_POOL), BF16),
            pltpu.VMEM(poolw.shape, BF16),
            pltpu.VMEM((D_SSD + D_POOL, D_MODEL), BF16),
            pltpu.VMEM((W_SLOTS, MIX_W_ROWS, D_MODEL), F32),
            pltpu.SemaphoreType.DMA((W_SLOTS,)),
            pltpu.VMEM((MIX_ROWS, D_SSD), F32),
            pltpu.VMEM((D_XBC // LANES, n_chunks * (CONV_PAD + CHUNK), LANES), F32),
            pltpu.VMEM((MIX_ROWS, LANES), F32),
            pltpu.VMEM((D_POOL // LANES, n_chunks * (POOL_PAD + CHUNK), LANES), F32),
            pltpu.VMEM((MIX_ROWS, D_XBC), F32),
            pltpu.VMEM((D_STATE, D_SSD), F32),
            pltpu.VMEM((CHUNK, D_POOL), BF16),
            pltpu.VMEM((CHUNK, D_SSD), F32),
            pltpu.VMEM((CHUNK, LANES), F32),
        ],
        compiler_params=pltpu.CompilerParams(
            dimension_semantics=("arbitrary",), vmem_limit_bytes=VMEM_LIMIT_BYTES),
        name="mixer",
    )(x2, x2, mod3, mod3, norm_w, w_in_t, *vmem_consts, w_out)
    return out.reshape(bsz, seq, D_MODEL)


def _pad_lanes(v):
    return jnp.pad(v.reshape(1, -1), ((0, 0), (0, LANES - v.shape[0])))


def kernel(x, c, w_ada, b_ada, ffn1_norm, ffn1_w_gate, ffn1_w_up, ffn1_w_down, mix_norm, w_in, conv_w, conv_b, dt_bias, a_log, d_skip, ssd_norm_w, pool_w, pool_b, pool_scale, w_out, ffn2_norm, ffn2_w_gate, ffn2_w_up, ffn2_w_down, final_norm):
    bsz = x.shape[0]
    depth = w_ada.shape[0]
    row = lambda v: v.reshape(1, -1)
    for i in range(depth):
        mod3 = _adaln(c, w_ada[i], row(b_ada[i])).reshape(bsz, N_MOD, D_MODEL)
        last = i == depth - 1

        x = _ffn(x, mod3, row(ffn1_norm[i]), ffn1_w_gate, ffn1_w_up, ffn1_w_down, row(final_norm),
                 layer=i, mod_idx=0, final=False)

        x = _mixer(
            x, mod3, row(mix_norm[i]), jnp.swapaxes(w_in, 1, 2),
            conv_w[i], row(conv_b[i]), _pad_lanes(dt_bias[i]), _pad_lanes(a_log[i]),
            row(jnp.repeat(d_skip[i], HEAD_DIM)), row(ssd_norm_w[i]),
            pool_w[i].astype(BF16), row(pool_b[i]), row(pool_scale[i]), w_out, layer=i)

        x = _ffn(x, mod3, row(ffn2_norm[i]), ffn2_w_gate, ffn2_w_up, ffn2_w_down, row(final_norm),
                 layer=i, mod_idx=6, final=last)
    return x
```

```python
import functools

import jax
import jax.numpy as jnp
from jax import lax
from jax.experimental import pallas as pl
from jax.experimental.pallas import tpu as pltpu

F32 = jnp.float32
BF16 = jnp.bfloat16

D_MODEL = 1024
D_FF = 2816
N_MOD = 9
FFN_RES = 0.5
EPS = 1e-6
D_SSD = 1024
HEAD_DIM = 64
N_HEADS = 16
N_GROUPS = 4
HEADS_PER_GROUP = N_HEADS // N_GROUPS
D_STATE = 128
D_CONV = 4
CHUNK = 128
D_POOL = 1024
POOL_WINDOWS = (2, 4, 8, 16)
POOL_GROUP_DIM = 256
D_XBC = D_SSD + 2 * N_GROUPS * D_STATE
DT_COL = D_SSD + D_XBC

LANES = 128
SUBLANES = 8
MXU_COLS = 256
VMEM_LIMIT_BYTES = 60 * 1024 * 1024

FFN_ROWS = 512
FFN_SUBTILES = 2
NORM_SLICES = 8
FFN_W_ROWS_OUT = 256
MIX_W_ROWS = 256
W_SLOTS = 8
MIX_ROWS = 512
ADA_COLS = 1024
CONV_PAD = SUBLANES
POOL_PAD = 16

assert HEAD_DIM * 2 == LANES and D_STATE == LANES and CHUNK == LANES


def _dot(a, b):
    return jnp.dot(a, b, preferred_element_type=F32)


def _split3(v):
    hi = v.astype(BF16)
    r1 = v - hi.astype(F32)
    mid = r1.astype(BF16)
    lo = (r1 - mid.astype(F32)).astype(BF16)
    return hi, mid, lo


def _silu(v):
    return v * jax.nn.sigmoid(v)


def _rms(v, w):
    ms = jnp.mean(v * v, axis=-1, keepdims=True)
    return v * lax.rsqrt(ms + EPS) * w


def _const_spec(shape):
    zeros = (0,) * len(shape)
    return pl.BlockSpec(shape, lambda *_: zeros, pipeline_mode=pl.Buffered(1))


def _adaln_kernel(c_ref, w_ref, b_ref, o_ref):
    a_hi, a_mid, _ = _split3(_silu(c_ref[...]))
    w_hi, w_mid, _ = _split3(w_ref[...])
    acc = _dot(a_hi, w_hi) + _dot(a_mid, w_hi) + _dot(a_hi, w_mid)
    o_ref[...] = acc + b_ref[...]


def _adaln(c, w_ada, b_ada):
    bsz = c.shape[0]
    n = w_ada.shape[1]
    c_pad = jnp.pad(c, ((0, SUBLANES - bsz), (0, 0)))
    mod = pl.pallas_call(
        _adaln_kernel,
        grid=(n // ADA_COLS,),
        in_specs=[
            pl.BlockSpec((SUBLANES, D_MODEL), lambda j: (0, 0)),
            pl.BlockSpec((D_MODEL, ADA_COLS), lambda j: (0, j)),
            pl.BlockSpec((1, ADA_COLS), lambda j: (0, j)),
        ],
        out_specs=pl.BlockSpec((SUBLANES, ADA_COLS), lambda j: (0, j)),
        out_shape=jax.ShapeDtypeStruct((SUBLANES, n), F32),
        compiler_params=pltpu.CompilerParams(
            dimension_semantics=("arbitrary",), vmem_limit_bytes=VMEM_LIMIT_BYTES),
        name="adaln",
    )(c_pad, w_ada, b_ada)
    return mod[:bsz]


class _BlockStream:
    def __init__(self, stage_ref, sem):
        self.stage, self.sem, self.blocks = stage_ref, sem, []
        self.n_slots = stage_ref.shape[0]
        self.done = 0

    def add(self, src, sink):
        self.blocks.append((src, sink))

    def _slot(self, i):
        return self.stage.at[i % self.n_slots, pl.ds(0, self.blocks[i][0].shape[0])]

    def _copy(self, i):
        return pltpu.make_async_copy(self.blocks[i][0], self._slot(i), self.sem.at[i % self.n_slots])

    def prime(self):
        for i in range(min(self.n_slots, len(self.blocks))):
            self._copy(i).start()

    def advance(self, n):
        for i in range(self.done, self.done + n):
            self._copy(i).wait()
            self.blocks[i][1](self._slot(i)[...])
            if i + self.n_slots < len(self.blocks):
                self._copy(i + self.n_slots).start()
        self.done += n

    def drain(self):
        self.advance(len(self.blocks) - self.done)


def _stream_rows_bf16(stream, w_hbm, layer, dst_ref):
    rows = stream.stage.shape[1]

    def sink(r):
        def run(block):
            dst_ref[r:r + rows, :] = block.astype(BF16)
        return run

    for r in range(0, dst_ref.shape[0], rows):
        stream.add(w_hbm.at[layer, pl.ds(r, rows), :], sink(r))


def _stream_cols_bf16(stream, w_hbm, layer, dst_ref, c0):
    cols = stream.stage.shape[2]

    def run(block):
        dst_ref[:, c0:c0 + cols] = block.astype(BF16)

    stream.add(w_hbm.at[layer, :, pl.ds(c0, cols)], run)


def _ffn_kernel(xc_ref, xn_ref, modc_ref, modn_ref, nw_ref, wg_hbm, wu_hbm, wd_hbm, fw_ref, o_ref,
                h_s, wg_ref, wu_ref, wd_ref, stage_in, stage_out, sem_in, sem_out,
                *, layer, mod_idx, final):
    def in_norm(x, mod_ref):
        sh = mod_ref[0, pl.ds(mod_idx, 1), :]
        sc = mod_ref[0, pl.ds(mod_idx + 1, 1), :]
        return (_rms(x, nw_ref[...]) * (1.0 + sc) + sh).astype(BF16)

    step = pl.program_id(0)
    cur = h_s.at[step % 2]
    nxt = h_s.at[(step + 1) % 2]

    n_chunks = D_FF // MXU_COLS
    slice_rows = FFN_ROWS // NORM_SLICES
    sub_rows = FFN_ROWS // FFN_SUBTILES
    slices_per_sub = NORM_SLICES // FFN_SUBTILES

    def body(first):
        if first:
            s_in, s_out = _BlockStream(stage_in, sem_in), _BlockStream(stage_out, sem_out)
            for k in range(n_chunks):
                _stream_cols_bf16(s_in, wg_hbm, layer, wg_ref, k * MXU_COLS)
                _stream_cols_bf16(s_in, wu_hbm, layer, wu_ref, k * MXU_COLS)
            _stream_rows_bf16(s_out, wd_hbm, layer, wd_ref)
            s_in.prime()
            s_out.prime()
            cur[...] = in_norm(xc_ref[...], modc_ref)

        gt = modc_ref[0, pl.ds(mod_idx + 2, 1), :]
        for sub in range(FFN_SUBTILES):
            rows = slice(sub * sub_rows, (sub + 1) * sub_rows)
            acts = []
            for k in range(n_chunks):
                if first and sub == 0:
                    s_in.advance(2)
                if k < slices_per_sub:
                    j = sub * slices_per_sub + k
                    rs = slice(j * slice_rows, (j + 1) * slice_rows)
                    nxt[rs, :] = in_norm(xn_ref[rs, :], modn_ref)
                cols = slice(k * MXU_COLS, (k + 1) * MXU_COLS)
                h = cur[rows, :]
                g = _dot(h, wg_ref[:, cols])
                u = _dot(h, wu_ref[:, cols])
                acts.append((_silu(g) * u).astype(BF16))
            if first and sub == 0:
                s_out.drain()
            a = jnp.concatenate(acts, axis=1)
            o = _dot(a, wd_ref[...])
            out = xc_ref[rows, :] + (FFN_RES * gt) * o
            if final:
                out = _rms(out, fw_ref[...])
            o_ref[rows, :] = out

    @pl.when(step == 0)
    def _():
        body(True)

    @pl.when(step > 0)
    def _():
        body(False)


def _ffn(x, mod3, norm_w, wg, wu, wd, final_w, *, layer, mod_idx, final):
    bsz, seq, _ = x.shape
    tiles_per_seq = seq // FFN_ROWS
    steps = bsz * tiles_per_seq
    nxt = lambda s: jnp.minimum(s + 1, steps - 1)
    x2 = x.reshape(bsz * seq, D_MODEL)
    hbm = pl.BlockSpec(memory_space=pl.ANY)
    out = pl.pallas_call(
        functools.partial(_ffn_kernel, layer=layer, mod_idx=mod_idx, final=final),
        grid=(steps,),
        in_specs=[
            pl.BlockSpec((FFN_ROWS, D_MODEL), lambda s: (s, 0)),
            pl.BlockSpec((FFN_ROWS, D_MODEL), lambda s: (nxt(s), 0)),
            pl.BlockSpec((1, N_MOD, D_MODEL), lambda s: (s // tiles_per_seq, 0, 0)),
            pl.BlockSpec((1, N_MOD, D_MODEL), lambda s: (nxt(s) // tiles_per_seq, 0, 0)),
            _const_spec((1, D_MODEL)),
            hbm, hbm, hbm,
            _const_spec((1, D_MODEL)),
        ],
        out_specs=pl.BlockSpec((FFN_ROWS, D_MODEL), lambda s: (s, 0)),
        out_shape=jax.ShapeDtypeStruct((bsz * seq, D_MODEL), F32),
        scratch_shapes=[
            pltpu.VMEM((2, FFN_ROWS, D_MODEL), BF16),
            pltpu.VMEM((D_MODEL, D_FF), BF16),
            pltpu.VMEM((D_MODEL, D_FF), BF16),
            pltpu.VMEM((D_FF, D_MODEL), BF16),
            pltpu.VMEM((W_SLOTS, D_MODEL, MXU_COLS), F32),
            pltpu.VMEM((W_SLOTS, FFN_W_ROWS_OUT, D_MODEL), F32),
            pltpu.SemaphoreType.DMA((W_SLOTS,)),
            pltpu.SemaphoreType.DMA((W_SLOTS,)),
        ],
        compiler_params=pltpu.CompilerParams(
            dimension_semantics=("arbitrary",), vmem_limit_bytes=VMEM_LIMIT_BYTES),
        name="ffn_final" if final else "ffn",
    )(x2, x2, mod3, mod3, norm_w, wg, wu, wd, final_w)
    return out.reshape(bsz, seq, D_MODEL)


def _mixer_kernel(xc_ref, xn_ref, modc_ref, modn_ref, nw_ref, wint_hbm,
                  convw_ref, convb_ref, dtb_ref, alog_ref, dskip_ref, ssdnw_ref,
                  poolw_in, poolb_ref, pools_ref, wout_hbm,
                  o_ref,
                  wz_ref, wxbc_ref, wdt_ref, wu_ref, poolw_ref, wout_ref, stage, sem,
                  z_s, raw_s, dtr_s, u_s, xbc_s, st_s, diff_s, zs_s, dtc_s, *, layer, tiles_per_seq):
    step = pl.program_id(0)
    tile_in_seq = step % tiles_per_seq
    next_starts_seq = tile_in_seq == tiles_per_seq - 1
    n_chunks = MIX_ROWS // CHUNK

    def in_norm(x, mod_ref):
        sh = mod_ref[0, 3:4, :]
        sc = mod_ref[0, 4:5, :]
        return (_rms(x, nw_ref[...]) * (1.0 + sc) + sh).astype(BF16)

    def rows(c):
        return slice(c * CHUNK, (c + 1) * CHUNK)

    def conv_base(c):
        return (c % n_chunks) * (CONV_PAD + CHUNK) + CONV_PAD

    def pool_base(c):
        return (c % n_chunks) * (POOL_PAD + CHUNK) + POOL_PAD

    pending = []

    def emit(k=None):
        n = len(pending) if k is None else min(k, len(pending))
        for _ in range(n):
            pending.pop(0)()

    def proj_pieces(c, h_n):
        rs = rows(c)

        def dense(dst, w_ref, lo, hi):
            def run():
                dst[rs, lo:hi] = _dot(h_n, w_ref[:, lo:hi])
            return run

        def slabs(dst, base, w_ref, lo, hi):
            def run():
                res = _dot(h_n, w_ref[:, lo:hi])
                for j in range(lo // LANES, hi // LANES):
                    dst[j, base:base + CHUNK, :] = res[:, j * LANES - lo:(j + 1) * LANES - lo]
            return run

        out = [slabs(raw_s, conv_base(c), wxbc_ref, k, k + MXU_COLS) for k in range(0, D_XBC, MXU_COLS)]
        out += [slabs(u_s, pool_base(c), wu_ref, k, k + MXU_COLS) for k in range(0, D_POOL, MXU_COLS)]
        out += [dense(z_s, wz_ref, k, k + MXU_COLS) for k in range(0, D_SSD, MXU_COLS)]
        out.append(dense(dtr_s, wdt_ref, 0, LANES))
        return out

    @pl.when(step == 0)
    def _():
        stream = _BlockStream(stage, sem)
        w_rows, w_cols = stage.shape[1], stage.shape[2]

        def cast_into(dst, r, c0):
            def run(block):
                dst[r:r + w_rows, c0:c0 + w_cols] = block.astype(BF16)
            return run

        def transposed_into(dst, c0):
            def run(block):
                dst[:, c0:c0 + block.shape[0]] = block.T.astype(BF16)
            return run

        def dt_into(block):
            lane = lax.broadcasted_iota(jnp.int32, (D_MODEL, LANES), 1)
            wdt_ref[...] = jnp.where(lane < N_HEADS, block.T, 0.0).astype(BF16)

        def feature_rows(f0):
            return lambda r, n: wint_hbm.at[layer, pl.ds(f0 + r, n), :]

        for src, dst, width in ((feature_rows(0), wz_ref, D_SSD), (feature_rows(D_SSD), wxbc_ref, D_XBC),
                                (feature_rows(DT_COL + N_HEADS), wu_ref, D_POOL)):
            for r in range(0, width, w_rows):
                stream.add(src(r, w_rows), transposed_into(dst, r))
        stream.add(feature_rows(DT_COL)(0, LANES), dt_into)
        for r in range(0, D_SSD + D_POOL, w_rows):
            stream.add(wout_hbm.at[layer, pl.ds(r, w_rows), :], cast_into(wout_ref, r, 0))
        stream.prime()
        poolw_ref[...] = poolw_in[...]
        stream.drain()
        raw_s[:, 0:CONV_PAD, :] = jnp.zeros((D_XBC // LANES, CONV_PAD, LANES), F32)
        u_s[:, 0:POOL_PAD, :] = jnp.zeros((D_POOL // LANES, POOL_PAD, LANES), F32)
        for c in range(n_chunks):
            for run in proj_pieces(c, in_norm(xc_ref[rows(c), :], modc_ref)):
                run()

    @pl.when(tile_in_seq == 0)
    def _():
        st_s[...] = jnp.zeros(st_s.shape, F32)

    gt = modc_ref[0, 5:6, :]
    a_neg = -jnp.exp(alog_ref[...])
    ri = lax.broadcasted_iota(jnp.int32, (CHUNK, CHUNK), 0)
    ci = lax.broadcasted_iota(jnp.int32, (CHUNK, CHUNK), 1)
    causal = ri >= ci
    tril = jnp.where(causal, 1.0, 0.0).astype(BF16)
    low_half = ci < HEAD_DIM
    low_half_row = lax.broadcasted_iota(jnp.int32, (1, LANES), 1) < HEAD_DIM

    def pre(c, seq_tile):
        rs = rows(c)
        last = c == n_chunks - 1

        def carry(hist):
            return jnp.where(next_starts_seq, jnp.zeros_like(hist), hist) if last else hist

        base, nbase = conv_base(c), conv_base(c + 1)
        for j in range(D_XBC // LANES):
            cols = slice(j * LANES, (j + 1) * LANES)
            xc = convb_ref[:, cols]
            for k in range(D_CONV):
                xc = xc + convw_ref[k:k + 1, cols] * raw_s[j, pl.ds(base - (D_CONV - 1) + k, CHUNK), :]
            raw_s[j, nbase - CONV_PAD:nbase, :] = carry(raw_s[j, base + CHUNK - CONV_PAD:base + CHUNK, :])
            xbc_s[rs, cols] = _silu(xc)
            if j % 2 == 1:
                emit(1)
        base, nbase = pool_base(c), pool_base(c + 1)
        diffs = []
        for gi, w in enumerate(POOL_WINDOWS):
            halves = []
            for j in range(gi * POOL_GROUP_DIM // LANES, (gi + 1) * POOL_GROUP_DIM // LANES):
                u_g = u_s[j, base:base + CHUNK, :]
                s = u_g
                for k in range(1, w):
                    s = s + u_s[j, pl.ds(base - k, CHUNK), :]
                if c == 0:
                    pos = seq_tile * MIX_ROWS + 1 + lax.broadcasted_iota(jnp.int32, (CHUNK, LANES), 0)
                    pooled = s / jnp.minimum(pos, w).astype(F32)
                else:
                    pooled = s * (1.0 / w)
                halves.append(pooled - u_g)
                u_s[j, nbase - POOL_PAD:nbase, :] = carry(u_s[j, base + CHUNK - POOL_PAD:base + CHUNK, :])
            diffs.append(jnp.concatenate(halves, axis=1).astype(BF16))
            emit(1)
        zs = _silu(z_s[rs, :])
        dt_c = jax.nn.softplus(dtr_s[rs, :] + dtb_ref[...])
        emit()
        return diffs, zs, dt_c

    def ssd(c, zs, dt_c):
        rs = rows(c)
        p_hi, p_mid, p_lo = _split3(dt_c * a_neg)
        cs = _dot(tril, p_hi) + _dot(tril, p_mid) + _dot(tril, p_lo)
        cs_t = cs.T
        dt_t = dt_c.T
        w_t = dt_t * jnp.exp(cs_t[:, CHUNK - 1:CHUNK] - cs_t)
        xs_c = xbc_s[rs, 0:D_SSD]
        emit(1)
        y_pairs = []
        for g in range(N_GROUPS):
            b0 = D_SSD + g * D_STATE
            c0 = D_SSD + N_GROUPS * D_STATE + g * D_STATE
            b_t = xbc_s[rs, b0:b0 + D_STATE].T
            c_g = xbc_s[rs, c0:c0 + D_STATE]
            cb = _dot(c_g.astype(BF16), b_t.astype(BF16))
            for pr in range(HEADS_PER_GROUP // 2):
                lhs, bw_t, cd = [], [], []
                for r in (2 * pr, 2 * pr + 1):
                    hd = g * HEADS_PER_GROUP + r
                    cs_bc = jnp.broadcast_to(cs[:, hd:hd + 1], (CHUNK, CHUNK))
                    decay = jnp.exp(jnp.where(causal, cs_bc - cs_t[hd:hd + 1, :], -jnp.inf))
                    scores = decay * (cb * dt_t[hd:hd + 1, :])
                    sd = jnp.exp(cs_bc)
                    lhs.append(jnp.concatenate([scores, c_g * sd], axis=1).astype(BF16))
                    bw_t.append((b_t * w_t[hd:hd + 1, :]).astype(BF16))
                    cd.append(sd[CHUNK - 1:CHUNK, :])
                col = (g * HEADS_PER_GROUP + 2 * pr) * HEAD_DIM
                x_p = xs_c[:, col:col + LANES]
                st = st_s[:, col:col + LANES]
                rhs = jnp.concatenate([x_p, st], axis=0).astype(BF16)
                y_pairs.append(jnp.where(low_half, _dot(lhs[0], rhs), _dot(lhs[1], rhs)))
                zero = jnp.zeros_like(x_p)
                x_split = jnp.concatenate(
                    [jnp.where(low_half, x_p, zero), jnp.where(low_half, zero, x_p)], axis=0)
                upd = _dot(jnp.concatenate(bw_t, axis=1), x_split.astype(BF16))
                st_s[:, col:col + LANES] = st * jnp.where(low_half_row, cd[0], cd[1]) + upd
                emit(1)
        y = jnp.concatenate(y_pairs, axis=1) + dskip_ref[...] * xs_c
        yz = y * zs
        gw = D_SSD // N_GROUPS
        parts = []
        for g in range(N_GROUPS):
            blk = yz[:, g * gw:(g + 1) * gw]
            ms = jnp.mean(blk * blk, axis=-1, keepdims=True)
            parts.append(blk * lax.rsqrt(ms + EPS))
        return (jnp.concatenate(parts, axis=1) * ssdnw_ref[...]).astype(BF16)

    def post_pieces(c, y_ssd, diffs):
        rs = rows(c)
        outs = [None] * len(POOL_WINDOWS)

        def pool_piece(gi):
            def run():
                lo, hi = gi * POOL_GROUP_DIM, (gi + 1) * POOL_GROUP_DIM
                o = (_dot(diffs[gi], poolw_ref[gi]) + poolb_ref[:, lo:hi]) * pools_ref[:, lo:hi]
                outs[gi] = o.astype(BF16)
            return run

        def out_piece(lo, hi):
            def run():
                y_pool = jnp.concatenate(outs, axis=1)
                out = (_dot(y_ssd, wout_ref[0:D_SSD, lo:hi])
                       + _dot(y_pool, wout_ref[D_SSD:D_SSD + D_POOL, lo:hi]))
                o_ref[rs, lo:hi] = xc_ref[rs, lo:hi] + gt[:, lo:hi] * out
            return run

        return ([pool_piece(gi) for gi in range(len(POOL_WINDOWS))]
                + [out_piece(k, k + MXU_COLS) for k in range(0, D_MODEL, MXU_COLS)])

    def hand_over(vals):
        diffs, zs, dt_c = vals
        for gi in range(len(POOL_WINDOWS)):
            diff_s[:, gi * POOL_GROUP_DIM:(gi + 1) * POOL_GROUP_DIM] = diffs[gi]
        zs_s[...] = zs
        dtc_s[...] = dt_c

    @pl.when(step == 0)
    def _():
        hand_over(pre(0, tile_in_seq))

    diffs = [diff_s[:, gi * POOL_GROUP_DIM:(gi + 1) * POOL_GROUP_DIM] for gi in range(len(POOL_WINDOWS))]
    zs, dt_c = zs_s[...], dtc_s[...]
    h_n = in_norm(xn_ref[rows(0), :], modn_ref)
    pending.extend(proj_pieces(0, h_n))
    for c in range(n_chunks):
        y_ssd = ssd(c, zs, dt_c)
        pending.extend(post_pieces(c, y_ssd, diffs))
        if c + 1 < n_chunks:
            h_n = in_norm(xn_ref[rows(c + 1), :], modn_ref)
            emit(2)
            diffs, zs, dt_c = pre(c + 1, tile_in_seq)
            pending.extend(proj_pieces(c + 1, h_n))
        else:
            hand_over(pre(0, (step + 1) % tiles_per_seq))
            emit()


def _mixer(x, mod3, norm_w, w_in_t, convw, convb, dtb, alog, dskip, ssdnw,
           poolw, poolb, pools, w_out, *, layer):
    bsz, seq, _ = x.shape
    tiles_per_seq = seq // MIX_ROWS
    n_chunks = MIX_ROWS // CHUNK
    steps = bsz * tiles_per_seq
    nxt = lambda s: jnp.minimum(s + 1, steps - 1)
    hbm = pl.BlockSpec(memory_space=pl.ANY)
    vmem_consts = [convw, convb, dtb, alog, dskip, ssdnw, poolw, poolb, pools]
    x2 = x.reshape(bsz * seq, D_MODEL)
    out = pl.pallas_call(
        functools.partial(_mixer_kernel, layer=layer, tiles_per_seq=tiles_per_seq),
        grid=(steps,),
        in_specs=[
            pl.BlockSpec((MIX_ROWS, D_MODEL), lambda s: (s, 0)),
            pl.BlockSpec((MIX_ROWS, D_MODEL), lambda s: (nxt(s), 0)),
            pl.BlockSpec((1, N_MOD, D_MODEL), lambda s: (s // tiles_per_seq, 0, 0)),
            pl.BlockSpec((1, N_MOD, D_MODEL), lambda s: (nxt(s) // tiles_per_seq, 0, 0)),
            _const_spec(norm_w.shape), hbm,
        ] + [_const_spec(a.shape) for a in vmem_consts] + [hbm],
        out_specs=pl.BlockSpec((MIX_ROWS, D_MODEL), lambda s: (s, 0)),
        out_shape=jax.ShapeDtypeStruct((bsz * seq, D_MODEL), F32),
        scratch_shapes=[
            pltpu.VMEM((D_MODEL, D_SSD), BF16),
            pltpu.VMEM((D_MODEL, D_XBC), BF16),
            pltpu.VMEM((D_MODEL, LANES), BF16),
            pltpu.VMEM((D_MODEL, D_POOL), BF16),
            pltpu.VMEM(poolw.shape, BF16),
            pltpu.VMEM((D_SSD + D_POOL, D_MODEL), BF16),
            pltpu.VMEM((W_SLOTS, MIX_W_ROWS, D_MODEL), F32),
            pltpu.SemaphoreType.DMA((W_SLOTS,)),
            pltpu.VMEM((MIX_ROWS, D_SSD), F32),
            pltpu.VMEM((D_XBC // LANES, n_chunks * (CONV_PAD + CHUNK), LANES), F32),
            pltpu.VMEM((MIX_ROWS, LANES), F32),
            pltpu.VMEM((D_POOL // LANES, n_chunks * (POOL_PAD + CHUNK), LANES), F32),
            pltpu.VMEM((MIX_ROWS, D_XBC), F32),
            pltpu.VMEM((D_STATE, D_SSD), F32),
            pltpu.VMEM((CHUNK, D_POOL), BF16),
            pltpu.VMEM((CHUNK, D_SSD), F32),
            pltpu.VMEM((CHUNK, LANES), F32),
        ],
        compiler_params=pltpu.CompilerParams(
            dimension_semantics=("arbitrary",), vmem_limit_bytes=VMEM_LIMIT_BYTES),
        name="mixer",
    )(x2, x2, mod3, mod3, norm_w, w_in_t, *vmem_consts, w_out)
    return out.reshape(bsz, seq, D_MODEL)


def _pad_lanes(v):
    return jnp.pad(v.reshape(1, -1), ((0, 0), (0, LANES - v.shape[0])))


def kernel(x, c, w_ada, b_ada, ffn1_norm, ffn1_w_gate, ffn1_w_up, ffn1_w_down, mix_norm, w_in, conv_w, conv_b, dt_bias, a_log, d_skip, ssd_norm_w, pool_w, pool_b, pool_scale, w_out, ffn2_norm, ffn2_w_gate, ffn2_w_up, ffn2_w_down, final_norm):
    bsz = x.shape[0]
    depth = w_ada.shape[0]
    row = lambda v: v.reshape(1, -1)
    for i in range(depth):
        mod3 = _adaln(c, w_ada[i], row(b_ada[i])).reshape(bsz, N_MOD, D_MODEL)
        last = i == depth - 1

        x = _ffn(x, mod3, row(ffn1_norm[i]), ffn1_w_gate, ffn1_w_up, ffn1_w_down, row(final_norm),
                 layer=i, mod_idx=0, final=False)

        x = _mixer(
            x, mod3, row(mix_norm[i]), jnp.swapaxes(w_in, 1, 2),
            conv_w[i], row(conv_b[i]), _pad_lanes(dt_bias[i]), _pad_lanes(a_log[i]),
            row(jnp.repeat(d_skip[i], HEAD_DIM)), row(ssd_norm_w[i]),
            pool_w[i].astype(BF16), row(pool_b[i]), row(pool_scale[i]), w_out, layer=i)

        x = _ffn(x, mod3, row(ffn2_norm[i]), ffn2_w_gate, ffn2_w_up, ffn2_w_down, row(final_norm),
                 layer=i, mod_idx=6, final=last)
    return x
```

```python
import functools

import jax
import jax.numpy as jnp
from jax import lax
from jax.experimental import pallas as pl
from jax.experimental.pallas import tpu as pltpu

F32 = jnp.float32
BF16 = jnp.bfloat16

D_MODEL = 1024
D_FF = 2816
N_MOD = 9
FFN_RES = 0.5
EPS = 1e-6
D_SSD = 1024
HEAD_DIM = 64
N_HEADS = 16
N_GROUPS = 4
HEADS_PER_GROUP = N_HEADS // N_GROUPS
D_STATE = 128
D_CONV = 4
CHUNK = 128
D_POOL = 1024
POOL_WINDOWS = (2, 4, 8, 16)
POOL_GROUP_DIM = 256
D_XBC = D_SSD + 2 * N_GROUPS * D_STATE
DT_COL = D_SSD + D_XBC

LANES = 128
SUBLANES = 8
MXU_COLS = 256
VMEM_LIMIT_BYTES = 60 * 1024 * 1024

FFN_ROWS = 512
FFN_SUBTILES = 2
NORM_SLICES = 8
FFN_W_ROWS_OUT = 256
MIX_W_ROWS = 256
W_SLOTS = 8
X_SLOTS = 3
MIX_ROWS = 512
ADA_COLS = 1024
CONV_PAD = SUBLANES
POOL_PAD = 16

assert HEAD_DIM * 2 == LANES and D_STATE == LANES and CHUNK == LANES


def _dot(a, b):
    return jnp.dot(a, b, preferred_element_type=F32)


def _split3(v):
    hi = v.astype(BF16)
    r1 = v - hi.astype(F32)
    mid = r1.astype(BF16)
    lo = (r1 - mid.astype(F32)).astype(BF16)
    return hi, mid, lo


def _silu(v):
    return v * jax.nn.sigmoid(v)


def _rms(v, w):
    ms = jnp.mean(v * v, axis=-1, keepdims=True)
    return v * lax.rsqrt(ms + EPS) * w


def _const_spec(shape):
    zeros = (0,) * len(shape)
    return pl.BlockSpec(shape, lambda *_: zeros, pipeline_mode=pl.Buffered(1))


def _adaln_kernel(c_ref, w_ref, b_ref, o_ref):
    a_hi, a_mid, _ = _split3(_silu(c_ref[...]))
    w_hi, w_mid, _ = _split3(w_ref[...])
    acc = _dot(a_hi, w_hi) + _dot(a_mid, w_hi) + _dot(a_hi, w_mid)
    o_ref[...] = acc + b_ref[...]


def _adaln(c, w_ada, b_ada):
    bsz = c.shape[0]
    n = w_ada.shape[1]
    c_pad = jnp.pad(c, ((0, SUBLANES - bsz), (0, 0)))
    mod = pl.pallas_call(
        _adaln_kernel,
        grid=(n // ADA_COLS,),
        in_specs=[
            pl.BlockSpec((SUBLANES, D_MODEL), lambda j: (0, 0)),
            pl.BlockSpec((D_MODEL, ADA_COLS), lambda j: (0, j)),
            pl.BlockSpec((1, ADA_COLS), lambda j: (0, j)),
        ],
        out_specs=pl.BlockSpec((SUBLANES, ADA_COLS), lambda j: (0, j)),
        out_shape=jax.ShapeDtypeStruct((SUBLANES, n), F32),
        compiler_params=pltpu.CompilerParams(
            dimension_semantics=("arbitrary",), vmem_limit_bytes=VMEM_LIMIT_BYTES),
        name="adaln",
    )(c_pad, w_ada, b_ada)
    return mod[:bsz]


class _BlockStream:
    def __init__(self, stage_ref, sem):
        self.stage, self.sem, self.blocks = stage_ref, sem, []
        self.n_slots = stage_ref.shape[0]
        self.done = 0

    def add(self, src, sink):
        self.blocks.append((src, sink))

    def _slot(self, i):
        return self.stage.at[i % self.n_slots, pl.ds(0, self.blocks[i][0].shape[0])]

    def _copy(self, i):
        return pltpu.make_async_copy(self.blocks[i][0], self._slot(i), self.sem.at[i % self.n_slots])

    def prime(self):
        for i in range(min(self.n_slots, len(self.blocks))):
            self._copy(i).start()

    def advance(self, n):
        for i in range(self.done, self.done + n):
            self._copy(i).wait()
            self.blocks[i][1](self._slot(i)[...])
            if i + self.n_slots < len(self.blocks):
                self._copy(i + self.n_slots).start()
        self.done += n

    def drain(self):
        self.advance(len(self.blocks) - self.done)


def _stream_rows_bf16(stream, w_hbm, layer, dst_ref):
    rows = stream.stage.shape[1]

    def sink(r):
        def run(block):
            dst_ref[r:r + rows, :] = block.astype(BF16)
        return run

    for r in range(0, dst_ref.shape[0], rows):
        stream.add(w_hbm.at[layer, pl.ds(r, rows), :], sink(r))


def _stream_cols_bf16(stream, w_hbm, layer, dst_ref, c0):
    cols = stream.stage.shape[2]

    def run(block):
        dst_ref[:, c0:c0 + cols] = block.astype(BF16)

    stream.add(w_hbm.at[layer, :, pl.ds(c0, cols)], run)


def _ffn_kernel(xc_ref, xn_ref, modc_ref, modn_ref, nw_ref, wg_hbm, wu_hbm, wd_hbm, fw_ref, o_ref,
                h_s, wg_ref, wu_ref, wd_ref, stage_in, stage_out, sem_in, sem_out,
                *, layer, mod_idx, final):
    def in_norm(x, mod_ref):
        sh = mod_ref[0, pl.ds(mod_idx, 1), :]
        sc = mod_ref[0, pl.ds(mod_idx + 1, 1), :]
        return (_rms(x, nw_ref[...]) * (1.0 + sc) + sh).astype(BF16)

    step = pl.program_id(0)
    cur = h_s.at[step % 2]
    nxt = h_s.at[(step + 1) % 2]

    n_chunks = D_FF // MXU_COLS
    slice_rows = FFN_ROWS // NORM_SLICES
    sub_rows = FFN_ROWS // FFN_SUBTILES
    slices_per_sub = NORM_SLICES // FFN_SUBTILES

    def body(first):
        if first:
            s_in, s_out = _BlockStream(stage_in, sem_in), _BlockStream(stage_out, sem_out)
            for k in range(n_chunks):
                _stream_cols_bf16(s_in, wg_hbm, layer, wg_ref, k * MXU_COLS)
                _stream_cols_bf16(s_in, wu_hbm, layer, wu_ref, k * MXU_COLS)
            _stream_rows_bf16(s_out, wd_hbm, layer, wd_ref)
            s_in.prime()
            s_out.prime()
            cur[...] = in_norm(xc_ref[...], modc_ref)

        gt = modc_ref[0, pl.ds(mod_idx + 2, 1), :]
        for sub in range(FFN_SUBTILES):
            rows = slice(sub * sub_rows, (sub + 1) * sub_rows)
            acts = []
            for k in range(n_chunks):
                if first and sub == 0:
                    s_in.advance(2)
                if k < slices_per_sub:
                    j = sub * slices_per_sub + k
                    rs = slice(j * slice_rows, (j + 1) * slice_rows)
                    nxt[rs, :] = in_norm(xn_ref[rs, :], modn_ref)
                cols = slice(k * MXU_COLS, (k + 1) * MXU_COLS)
                h = cur[rows, :]
                g = _dot(h, wg_ref[:, cols])
                u = _dot(h, wu_ref[:, cols])
                acts.append((_silu(g) * u).astype(BF16))
            if first and sub == 0:
                s_out.drain()
            a = jnp.concatenate(acts, axis=1)
            o = _dot(a, wd_ref[...])
            out = xc_ref[rows, :] + (FFN_RES * gt) * o
            if final:
                out = _rms(out, fw_ref[...])
            o_ref[rows, :] = out

    @pl.when(step == 0)
    def _():
        body(True)

    @pl.when(step > 0)
    def _():
        body(False)


def _ffn(x, mod3, norm_w, wg, wu, wd, final_w, *, layer, mod_idx, final):
    bsz, seq, _ = x.shape
    tiles_per_seq = seq // FFN_ROWS
    steps = bsz * tiles_per_seq
    nxt = lambda s: jnp.minimum(s + 1, steps - 1)
    x2 = x.reshape(bsz * seq, D_MODEL)
    hbm = pl.BlockSpec(memory_space=pl.ANY)
    out = pl.pallas_call(
        functools.partial(_ffn_kernel, layer=layer, mod_idx=mod_idx, final=final),
        grid=(steps,),
        in_specs=[
            pl.BlockSpec((FFN_ROWS, D_MODEL), lambda s: (s, 0)),
            pl.BlockSpec((FFN_ROWS, D_MODEL), lambda s: (nxt(s), 0)),
            pl.BlockSpec((1, N_MOD, D_MODEL), lambda s: (s // tiles_per_seq, 0, 0)),
            pl.BlockSpec((1, N_MOD, D_MODEL), lambda s: (nxt(s) // tiles_per_seq, 0, 0)),
            _const_spec((1, D_MODEL)),
            hbm, hbm, hbm,
            _const_spec((1, D_MODEL)),
        ],
        out_specs=pl.BlockSpec((FFN_ROWS, D_MODEL), lambda s: (s, 0)),
        out_shape=jax.ShapeDtypeStruct((bsz * seq, D_MODEL), F32),
        scratch_shapes=[
            pltpu.VMEM((2, FFN_ROWS, D_MODEL), BF16),
            pltpu.VMEM((D_MODEL, D_FF), BF16),
            pltpu.VMEM((D_MODEL, D_FF), BF16),
            pltpu.VMEM((D_FF, D_MODEL), BF16),
            pltpu.VMEM((W_SLOTS, D_MODEL, MXU_COLS), F32),
            pltpu.VMEM((W_SLOTS, FFN_W_ROWS_OUT, D_MODEL), F32),
            pltpu.SemaphoreType.DMA((W_SLOTS,)),
            pltpu.SemaphoreType.DMA((W_SLOTS,)),
        ],
        compiler_params=pltpu.CompilerParams(
            dimension_semantics=("arbitrary",), vmem_limit_bytes=VMEM_LIMIT_BYTES),
        name="ffn_final" if final else "ffn",
    )(x2, x2, mod3, mod3, norm_w, wg, wu, wd, final_w)
    return out.reshape(bsz, seq, D_MODEL)


def _mixer_kernel(x_hbm, modc_ref, modn_ref, nw_ref, wint_hbm,
                  convw_ref, convb_ref, dtb_ref, alog_ref, dskip_ref, ssdnw_ref,
                  poolw_in, poolb_ref, pools_ref, wout_hbm,
                  o_ref,
                  wz_ref, wxbc_ref, wdt_ref, wu_ref, poolw_ref, wout_ref, stage, sem,
                  z_s, raw_s, dtr_s, u_s, xbc_s, st_s, diff_s, zs_s, dtc_s, x_ring, x_sem,
                  *, layer, tiles_per_seq):
    step = pl.program_id(0)
    n_steps = pl.num_programs(0)
    tile_in_seq = step % tiles_per_seq
    next_starts_seq = tile_in_seq == tiles_per_seq - 1
    n_chunks = MIX_ROWS // CHUNK

    def x_copy(t):
        slot = t % X_SLOTS
        row0 = pl.multiple_of(t * MIX_ROWS, MIX_ROWS)
        return pltpu.make_async_copy(x_hbm.at[pl.ds(row0, MIX_ROWS), :], x_ring.at[slot], x_sem.at[slot])

    @pl.when(step == 0)
    def _():
        x_copy(0).start()
        x_copy(1).start()
        x_copy(0).wait()
        x_copy(1).wait()

    @pl.when(jnp.logical_and(step > 0, step + 1 < n_steps))
    def _():
        x_copy(step + 1).wait()

    @pl.when(step + 2 < n_steps)
    def _():
        x_copy(step + 2).start()

    xc_ref = x_ring.at[step % X_SLOTS]
    xn_ref = x_ring.at[(step + 1) % X_SLOTS]

    def in_norm(x, mod_ref):
        sh = mod_ref[0, 3:4, :]
        sc = mod_ref[0, 4:5, :]
        return (_rms(x, nw_ref[...]) * (1.0 + sc) + sh).astype(BF16)

    def rows(c):
        return slice(c * CHUNK, (c + 1) * CHUNK)

    def conv_base(c):
        return (c % n_chunks) * (CONV_PAD + CHUNK) + CONV_PAD

    def pool_base(c):
        return (c % n_chunks) * (POOL_PAD + CHUNK) + POOL_PAD

    pending = []

    def emit(k=None):
        n = len(pending) if k is None else min(k, len(pending))
        for _ in range(n):
            pending.pop(0)()

    def proj_pieces(c, h_n):
        rs = rows(c)

        def dense(dst, w_ref, lo, hi):
            def run():
                dst[rs, lo:hi] = _dot(h_n, w_ref[:, lo:hi])
            return run

        def slabs(dst, base, w_ref, lo, hi):
            def run():
                res = _dot(h_n, w_ref[:, lo:hi])
                for j in range(lo // LANES, hi // LANES):
                    dst[j, base:base + CHUNK, :] = res[:, j * LANES - lo:(j + 1) * LANES - lo]
            return run

        out = [slabs(raw_s, conv_base(c), wxbc_ref, k, k + MXU_COLS) for k in range(0, D_XBC, MXU_COLS)]
        out += [slabs(u_s, pool_base(c), wu_ref, k, k + MXU_COLS) for k in range(0, D_POOL, MXU_COLS)]
        out += [dense(z_s, wz_ref, k, k + MXU_COLS) for k in range(0, D_SSD, MXU_COLS)]
        out.append(dense(dtr_s, wdt_ref, 0, LANES))
        return out

    @pl.when(step == 0)
    def _():
        stream = _BlockStream(stage, sem)
        w_rows, w_cols = stage.shape[1], stage.shape[2]

        def cast_into(dst, r, c0):
            def run(block):
                dst[r:r + w_rows, c0:c0 + w_cols] = block.astype(BF16)
            return run

        def transposed_into(dst, c0):
            def run(block):
                dst[:, c0:c0 + block.shape[0]] = block.T.astype(BF16)
            return run

        def dt_into(block):
            lane = lax.broadcasted_iota(jnp.int32, (D_MODEL, LANES), 1)
            wdt_ref[...] = jnp.where(lane < N_HEADS, block.T, 0.0).astype(BF16)

        def feature_rows(f0):
            return lambda r, n: wint_hbm.at[layer, pl.ds(f0 + r, n), :]

        for src, dst, width in ((feature_rows(0), wz_ref, D_SSD), (feature_rows(D_SSD), wxbc_ref, D_XBC),
                                (feature_rows(DT_COL + N_HEADS), wu_ref, D_POOL)):
            for r in range(0, width, w_rows):
                stream.add(src(r, w_rows), transposed_into(dst, r))
        stream.add(feature_rows(DT_COL)(0, LANES), dt_into)
        for r in range(0, D_SSD + D_POOL, w_rows):
            stream.add(wout_hbm.at[layer, pl.ds(r, w_rows), :], cast_into(wout_ref, r, 0))
        stream.prime()
        poolw_ref[...] = poolw_in[...]
        stream.drain()
        raw_s[:, 0:CONV_PAD, :] = jnp.zeros((D_XBC // LANES, CONV_PAD, LANES), F32)
        u_s[:, 0:POOL_PAD, :] = jnp.zeros((D_POOL // LANES, POOL_PAD, LANES), F32)
        for c in range(n_chunks):
            for run in proj_pieces(c, in_norm(xc_ref[rows(c), :], modc_ref)):
                run()

    @pl.when(tile_in_seq == 0)
    def _():
        st_s[...] = jnp.zeros(st_s.shape, F32)

    gt = modc_ref[0, 5:6, :]
    a_neg = -jnp.exp(alog_ref[...])
    ri = lax.broadcasted_iota(jnp.int32, (CHUNK, CHUNK), 0)
    ci = lax.broadcasted_iota(jnp.int32, (CHUNK, CHUNK), 1)
    causal = ri >= ci
    tril = jnp.where(causal, 1.0, 0.0).astype(BF16)
    low_half = ci < HEAD_DIM
    low_half_row = lax.broadcasted_iota(jnp.int32, (1, LANES), 1) < HEAD_DIM

    def pre(c, seq_tile):
        rs = rows(c)
        last = c == n_chunks - 1

        def carry(hist):
            return jnp.where(next_starts_seq, jnp.zeros_like(hist), hist) if last else hist

        base, nbase = conv_base(c), conv_base(c + 1)
        for j in range(D_XBC // LANES):
            cols = slice(j * LANES, (j + 1) * LANES)
            xc = convb_ref[:, cols]
            for k in range(D_CONV):
                xc = xc + convw_ref[k:k + 1, cols] * raw_s[j, pl.ds(base - (D_CONV - 1) + k, CHUNK), :]
            raw_s[j, nbase - CONV_PAD:nbase, :] = carry(raw_s[j, base + CHUNK - CONV_PAD:base + CHUNK, :])
            xbc_s[rs, cols] = _silu(xc)
            if j % 2 == 1:
                emit(1)
        base, nbase = pool_base(c), pool_base(c + 1)
        diffs = []
        for gi, w in enumerate(POOL_WINDOWS):
            halves = []
            for j in range(gi * POOL_GROUP_DIM // LANES, (gi + 1) * POOL_GROUP_DIM // LANES):
                u_g = u_s[j, base:base + CHUNK, :]
                s = u_g
                for k in range(1, w):
                    s = s + u_s[j, pl.ds(base - k, CHUNK), :]
                if c == 0:
                    pos = seq_tile * MIX_ROWS + 1 + lax.broadcasted_iota(jnp.int32, (CHUNK, LANES), 0)
                    pooled = s / jnp.minimum(pos, w).astype(F32)
                else:
                    pooled = s * (1.0 / w)
                halves.append(pooled - u_g)
                u_s[j, nbase - POOL_PAD:nbase, :] = carry(u_s[j, base + CHUNK - POOL_PAD:base + CHUNK, :])
            diffs.append(jnp.concatenate(halves, axis=1).astype(BF16))
            emit(1)
        zs = _silu(z_s[rs, :])
        dt_c = jax.nn.softplus(dtr_s[rs, :] + dtb_ref[...])
        emit()
        return diffs, zs, dt_c

    def ssd(c, zs, dt_c):
        rs = rows(c)
        p_hi, p_mid, p_lo = _split3(dt_c * a_neg)
        cs = _dot(tril, p_hi) + _dot(tril, p_mid) + _dot(tril, p_lo)
        cs_t = cs.T
        dt_t = dt_c.T
        w_t = dt_t * jnp.exp(cs_t[:, CHUNK - 1:CHUNK] - cs_t)
        xs_c = xbc_s[rs, 0:D_SSD]
        emit(1)
        y_pairs = []
        for g in range(N_GROUPS):
            b0 = D_SSD + g * D_STATE
            c0 = D_SSD + N_GROUPS * D_STATE + g * D_STATE
            b_t = xbc_s[rs, b0:b0 + D_STATE].T
            c_g = xbc_s[rs, c0:c0 + D_STATE]
            cb = _dot(c_g.astype(BF16), b_t.astype(BF16))
            for pr in range(HEADS_PER_GROUP // 2):
                lhs, bw_t, cd = [], [], []
                for r in (2 * pr, 2 * pr + 1):
                    hd = g * HEADS_PER_GROUP + r
                    cs_bc = jnp.broadcast_to(cs[:, hd:hd + 1], (CHUNK, CHUNK))
                    decay = jnp.exp(jnp.where(causal, cs_bc - cs_t[hd:hd + 1, :], -jnp.inf))
                    scores = decay * (cb * dt_t[hd:hd + 1, :])
                    sd = jnp.exp(cs_bc)
                    lhs.append(jnp.concatenate([scores, c_g * sd], axis=1).astype(BF16))
                    bw_t.append((b_t * w_t[hd:hd + 1, :]).astype(BF16))
                    cd.append(sd[CHUNK - 1:CHUNK, :])
                col = (g * HEADS_PER_GROUP + 2 * pr) * HEAD_DIM
                x_p = xs_c[:, col:col + LANES]
                st = st_s[:, col:col + LANES]
                rhs = jnp.concatenate([x_p, st], axis=0).astype(BF16)
                y_pairs.append(jnp.where(low_half, _dot(lhs[0], rhs), _dot(lhs[1], rhs)))
                zero = jnp.zeros_like(x_p)
                x_split = jnp.concatenate(
                    [jnp.where(low_half, x_p, zero), jnp.where(low_half, zero, x_p)], axis=0)
                upd = _dot(jnp.concatenate(bw_t, axis=1), x_split.astype(BF16))
                st_s[:, col:col + LANES] = st * jnp.where(low_half_row, cd[0], cd[1]) + upd
                emit(1)
        y = jnp.concatenate(y_pairs, axis=1) + dskip_ref[...] * xs_c
        yz = y * zs
        gw = D_SSD // N_GROUPS
        parts = []
        for g in range(N_GROUPS):
            blk = yz[:, g * gw:(g + 1) * gw]
            ms = jnp.mean(blk * blk, axis=-1, keepdims=True)
            parts.append(blk * lax.rsqrt(ms + EPS))
        return (jnp.concatenate(parts, axis=1) * ssdnw_ref[...]).astype(BF16)

    def post_pieces(c, y_ssd, diffs):
        rs = rows(c)
        outs = [None] * len(POOL_WINDOWS)

        def pool_piece(gi):
            def run():
                lo, hi = gi * POOL_GROUP_DIM, (gi + 1) * POOL_GROUP_DIM
                o = (_dot(diffs[gi], poolw_ref[gi]) + poolb_ref[:, lo:hi]) * pools_ref[:, lo:hi]
                outs[gi] = o.astype(BF16)
            return run

        def out_piece(lo, hi):
            def run():
                y_pool = jnp.concatenate(outs, axis=1)
                out = (_dot(y_ssd, wout_ref[0:D_SSD, lo:hi])
                       + _dot(y_pool, wout_ref[D_SSD:D_SSD + D_POOL, lo:hi]))
                o_ref[rs, lo:hi] = xc_ref[rs, lo:hi] + gt[:, lo:hi] * out
            return run

        return ([pool_piece(gi) for gi in range(len(POOL_WINDOWS))]
                + [out_piece(k, k + MXU_COLS) for k in range(0, D_MODEL, MXU_COLS)])

    def hand_over(vals):
        diffs, zs, dt_c = vals
        for gi in range(len(POOL_WINDOWS)):
            diff_s[:, gi * POOL_GROUP_DIM:(gi + 1) * POOL_GROUP_DIM] = diffs[gi]
        zs_s[...] = zs
        dtc_s[...] = dt_c

    @pl.when(step == 0)
    def _():
        hand_over(pre(0, tile_in_seq))

    diffs = [diff_s[:, gi * POOL_GROUP_DIM:(gi + 1) * POOL_GROUP_DIM] for gi in range(len(POOL_WINDOWS))]
    zs, dt_c = zs_s[...], dtc_s[...]
    h_n = in_norm(xn_ref[rows(0), :], modn_ref)
    pending.extend(proj_pieces(0, h_n))
    for c in range(n_chunks):
        y_ssd = ssd(c, zs, dt_c)
        pending.extend(post_pieces(c, y_ssd, diffs))
        if c + 1 < n_chunks:
            h_n = in_norm(xn_ref[rows(c + 1), :], modn_ref)
            emit(2)
            diffs, zs, dt_c = pre(c + 1, tile_in_seq)
            pending.extend(proj_pieces(c + 1, h_n))
        else:
            hand_over(pre(0, (step + 1) % tiles_per_seq))
            emit()


def _mixer(x, mod3, norm_w, w_in_t, convw, convb, dtb, alog, dskip, ssdnw,
           poolw, poolb, pools, w_out, *, layer):
    bsz, seq, _ = x.shape
    tiles_per_seq = seq // MIX_ROWS
    n_chunks = MIX_ROWS // CHUNK
    steps = bsz * tiles_per_seq
    assert steps >= X_SLOTS
    nxt = lambda s: jnp.minimum(s + 1, steps - 1)
    hbm = pl.BlockSpec(memory_space=pl.ANY)
    vmem_consts = [convw, convb, dtb, alog, dskip, ssdnw, poolw, poolb, pools]
    x2 = x.reshape(bsz * seq, D_MODEL)
    out = pl.pallas_call(
        functools.partial(_mixer_kernel, layer=layer, tiles_per_seq=tiles_per_seq),
        grid=(steps,),
        in_specs=[
            hbm,
            pl.BlockSpec((1, N_MOD, D_MODEL), lambda s: (s // tiles_per_seq, 0, 0)),
            pl.BlockSpec((1, N_MOD, D_MODEL), lambda s: (nxt(s) // tiles_per_seq, 0, 0)),
            _const_spec(norm_w.shape), hbm,
        ] + [_const_spec(a.shape) for a in vmem_consts] + [hbm],
        out_specs=pl.BlockSpec((MIX_ROWS, D_MODEL), lambda s: (s, 0)),
        out_shape=jax.ShapeDtypeStruct((bsz * seq, D_MODEL), F32),
        scratch_shapes=[
            pltpu.VMEM((D_MODEL, D_SSD), BF16),
            pltpu.VMEM((D_MODEL, D_XBC), BF16),
            pltpu.VMEM((D_MODEL, LANES), BF16),
            pltpu.VMEM((D_MODEL, D_POOL), BF16),
            pltpu.VMEM(poolw.shape, BF16),
            pltpu.VMEM((D_SSD + D_POOL, D_MODEL), BF16),
            pltpu.VMEM((W_SLOTS, MIX_W_ROWS, D_MODEL), F32),
            pltpu.SemaphoreType.DMA((W_SLOTS,)),
            pltpu.VMEM((MIX_ROWS, D_SSD), F32),
            pltpu.VMEM((D_XBC // LANES, n_chunks * (CONV_PAD + CHUNK), LANES), F32),
            pltpu.VMEM((MIX_ROWS, LANES), F32),
            pltpu.VMEM((D_POOL // LANES, n_chunks * (POOL_PAD + CHUNK), LANES), F32),
            pltpu.VMEM((MIX_ROWS, D_XBC), F32),
            pltpu.VMEM((D_STATE, D_SSD), F32),
            pltpu.VMEM((CHUNK, D_POOL), BF16),
            pltpu.VMEM((CHUNK, D_SSD), F32),
            pltpu.VMEM((CHUNK, LANES), F32),
            pltpu.VMEM((X_SLOTS, MIX_ROWS, D_MODEL), F32),
            pltpu.SemaphoreType.DMA((X_SLOTS,)),
        ],
        compiler_params=pltpu.CompilerParams(
            dimension_semantics=("arbitrary",), vmem_limit_bytes=VMEM_LIMIT_BYTES),
        name="mixer",
    )(x2, mod3, mod3, norm_w, w_in_t, *vmem_consts, w_out)
    return out.reshape(bsz, seq, D_MODEL)


def _pad_lanes(v):
    return jnp.pad(v.reshape(1, -1), ((0, 0), (0, LANES - v.shape[0])))


def kernel(x, c, w_ada, b_ada, ffn1_norm, ffn1_w_gate, ffn1_w_up, ffn1_w_down, mix_norm, w_in, conv_w, conv_b, dt_bias, a_log, d_skip, ssd_norm_w, pool_w, pool_b, pool_scale, w_out, ffn2_norm, ffn2_w_gate, ffn2_w_up, ffn2_w_down, final_norm):
    bsz = x.shape[0]
    depth = w_ada.shape[0]
    row = lambda v: v.reshape(1, -1)
    for i in range(depth):
        mod3 = _adaln(c, w_ada[i], row(b_ada[i])).reshape(bsz, N_MOD, D_MODEL)
        last = i == depth - 1

        x = _ffn(x, mod3, row(ffn1_norm[i]), ffn1_w_gate, ffn1_w_up, ffn1_w_down, row(final_norm),
                 layer=i, mod_idx=0, final=False)

        x = _mixer(
            x, mod3, row(mix_norm[i]), jnp.swapaxes(w_in, 1, 2),
            conv_w[i], row(conv_b[i]), _pad_lanes(dt_bias[i]), _pad_lanes(a_log[i]),
            row(jnp.repeat(d_skip[i], HEAD_DIM)), row(ssd_norm_w[i]),
            pool_w[i].astype(BF16), row(pool_b[i]), row(pool_scale[i]), w_out, layer=i)

        x = _ffn(x, mod3, row(ffn2_norm[i]), ffn2_w_gate, ffn2_w_up, ffn2_w_down, row(final_norm),
                 layer=i, mod_idx=6, final=last)
    return x
```

```python
import functools

import jax
import jax.numpy as jnp
from jax import lax
from jax.experimental import pallas as pl
from jax.experimental.pallas import tpu as pltpu

F32 = jnp.float32
BF16 = jnp.bfloat16

D_MODEL = 1024
D_FF = 2816
N_MOD = 9
FFN_RES = 0.5
EPS = 1e-6
D_SSD = 1024
HEAD_DIM = 64
N_HEADS = 16
N_GROUPS = 4
HEADS_PER_GROUP = N_HEADS // N_GROUPS
D_STATE = 128
D_CONV = 4
CHUNK = 128
D_POOL = 1024
POOL_WINDOWS = (2, 4, 8, 16)
POOL_GROUP_DIM = 256
D_XBC = D_SSD + 2 * N_GROUPS * D_STATE
DT_COL = D_SSD + D_XBC

LANES = 128
SUBLANES = 8
MXU_COLS = 256
VMEM_LIMIT_BYTES = 60 * 1024 * 1024

FFN_ROWS = 512
FFN_SUBTILES = 2
NORM_SLICES = 8
FFN_W_ROWS_OUT = 256
MIX_W_ROWS = 256
W_SLOTS = 8
X_SLOTS = 3
MIX_ROWS = 512
ADA_COLS = 1024
CONV_PAD = SUBLANES
POOL_PAD = 16

assert HEAD_DIM * 2 == LANES and D_STATE == LANES and CHUNK == LANES


def _dot(a, b):
    return jnp.dot(a, b, preferred_element_type=F32)


def _split3(v):
    hi = v.astype(BF16)
    r1 = v - hi.astype(F32)
    mid = r1.astype(BF16)
    lo = (r1 - mid.astype(F32)).astype(BF16)
    return hi, mid, lo


def _silu(v):
    return v * jax.nn.sigmoid(v)


def _rms(v, w):
    ms = jnp.mean(v * v, axis=-1, keepdims=True)
    return v * lax.rsqrt(ms + EPS) * w


def _const_spec(shape):
    zeros = (0,) * len(shape)
    return pl.BlockSpec(shape, lambda *_: zeros, pipeline_mode=pl.Buffered(1))


def _adaln_kernel(c_ref, w_ref, b_ref, o_ref):
    a_hi, a_mid, _ = _split3(_silu(c_ref[...]))
    w_hi, w_mid, _ = _split3(w_ref[...])
    acc = _dot(a_hi, w_hi) + _dot(a_mid, w_hi) + _dot(a_hi, w_mid)
    o_ref[...] = acc + b_ref[...]


def _adaln(c, w_ada, b_ada):
    bsz = c.shape[0]
    n = w_ada.shape[1]
    c_pad = jnp.pad(c, ((0, SUBLANES - bsz), (0, 0)))
    mod = pl.pallas_call(
        _adaln_kernel,
        grid=(n // ADA_COLS,),
        in_specs=[
            pl.BlockSpec((SUBLANES, D_MODEL), lambda j: (0, 0)),
            pl.BlockSpec((D_MODEL, ADA_COLS), lambda j: (0, j)),
            pl.BlockSpec((1, ADA_COLS), lambda j: (0, j)),
        ],
        out_specs=pl.BlockSpec((SUBLANES, ADA_COLS), lambda j: (0, j)),
        out_shape=jax.ShapeDtypeStruct((SUBLANES, n), F32),
        compiler_params=pltpu.CompilerParams(
            dimension_semantics=("arbitrary",), vmem_limit_bytes=VMEM_LIMIT_BYTES),
        name="adaln",
    )(c_pad, w_ada, b_ada)
    return mod[:bsz]


class _BlockStream:
    def __init__(self, stage_ref, sem):
        self.stage, self.sem, self.blocks = stage_ref, sem, []
        self.n_slots = stage_ref.shape[0]
        self.done = 0

    def add(self, src, sink):
        self.blocks.append((src, sink))

    def _slot(self, i):
        return self.stage.at[i % self.n_slots, pl.ds(0, self.blocks[i][0].shape[0])]

    def _copy(self, i):
        return pltpu.make_async_copy(self.blocks[i][0], self._slot(i), self.sem.at[i % self.n_slots])

    def prime(self):
        for i in range(min(self.n_slots, len(self.blocks))):
            self._copy(i).start()

    def advance(self, n):
        for i in range(self.done, self.done + n):
            self._copy(i).wait()
            self.blocks[i][1](self._slot(i)[...])
            if i + self.n_slots < len(self.blocks):
                self._copy(i + self.n_slots).start()
        self.done += n

    def drain(self):
        self.advance(len(self.blocks) - self.done)


def _stream_rows_bf16(stream, w_hbm, layer, dst_ref):
    rows = stream.stage.shape[1]

    def sink(r):
        def run(block):
            dst_ref[r:r + rows, :] = block.astype(BF16)
        return run

    for r in range(0, dst_ref.shape[0], rows):
        stream.add(w_hbm.at[layer, pl.ds(r, rows), :], sink(r))


def _stream_cols_bf16(stream, w_hbm, layer, dst_ref, c0):
    cols = stream.stage.shape[2]

    def run(block):
        dst_ref[:, c0:c0 + cols] = block.astype(BF16)

    stream.add(w_hbm.at[layer, :, pl.ds(c0, cols)], run)


def _ffn_kernel(xc_ref, xn_ref, modc_ref, modn_ref, nw_ref, wg_hbm, wu_hbm, wd_hbm, fw_ref, o_ref,
                h_s, wg_ref, wu_ref, wd_ref, stage_in, stage_out, sem_in, sem_out,
                *, layer, mod_idx, final):
    def in_norm(x, mod_ref):
        sh = mod_ref[0, pl.ds(mod_idx, 1), :]
        sc = mod_ref[0, pl.ds(mod_idx + 1, 1), :]
        return (_rms(x, nw_ref[...]) * (1.0 + sc) + sh).astype(BF16)

    step = pl.program_id(0)
    cur = h_s.at[step % 2]
    nxt = h_s.at[(step + 1) % 2]

    n_chunks = D_FF // MXU_COLS
    slice_rows = FFN_ROWS // NORM_SLICES
    sub_rows = FFN_ROWS // FFN_SUBTILES
    slices_per_sub = NORM_SLICES // FFN_SUBTILES

    def body(first):
        if first:
            s_in, s_out = _BlockStream(stage_in, sem_in), _BlockStream(stage_out, sem_out)
            for k in range(n_chunks):
                _stream_cols_bf16(s_in, wg_hbm, layer, wg_ref, k * MXU_COLS)
                _stream_cols_bf16(s_in, wu_hbm, layer, wu_ref, k * MXU_COLS)
            _stream_rows_bf16(s_out, wd_hbm, layer, wd_ref)
            s_in.prime()
            s_out.prime()
            cur[...] = in_norm(xc_ref[...], modc_ref)

        gt = modc_ref[0, pl.ds(mod_idx + 2, 1), :]
        for sub in range(FFN_SUBTILES):
            rows = slice(sub * sub_rows, (sub + 1) * sub_rows)
            acts = []
            for k in range(n_chunks):
                if first and sub == 0:
                    s_in.advance(2)
                if k < slices_per_sub:
                    j = sub * slices_per_sub + k
                    rs = slice(j * slice_rows, (j + 1) * slice_rows)
                    nxt[rs, :] = in_norm(xn_ref[rs, :], modn_ref)
                cols = slice(k * MXU_COLS, (k + 1) * MXU_COLS)
                h = cur[rows, :]
                g = _dot(h, wg_ref[:, cols])
                u = _dot(h, wu_ref[:, cols])
                acts.append((_silu(g) * u).astype(BF16))
            if first and sub == 0:
                s_out.drain()
            a = jnp.concatenate(acts, axis=1)
            o = _dot(a, wd_ref[...])
            out = xc_ref[rows, :] + (FFN_RES * gt) * o
            if final:
                out = _rms(out, fw_ref[...])
            o_ref[rows, :] = out

    @pl.when(step == 0)
    def _():
        body(True)

    @pl.when(step > 0)
    def _():
        body(False)


def _ffn(x, mod3, norm_w, wg, wu, wd, final_w, *, layer, mod_idx, final):
    bsz, seq, _ = x.shape
    tiles_per_seq = seq // FFN_ROWS
    steps = bsz * tiles_per_seq
    nxt = lambda s: jnp.minimum(s + 1, steps - 1)
    x2 = x.reshape(bsz * seq, D_MODEL)
    hbm = pl.BlockSpec(memory_space=pl.ANY)
    out = pl.pallas_call(
        functools.partial(_ffn_kernel, layer=layer, mod_idx=mod_idx, final=final),
        grid=(steps,),
        in_specs=[
            pl.BlockSpec((FFN_ROWS, D_MODEL), lambda s: (s, 0)),
            pl.BlockSpec((FFN_ROWS, D_MODEL), lambda s: (nxt(s), 0)),
            pl.BlockSpec((1, N_MOD, D_MODEL), lambda s: (s // tiles_per_seq, 0, 0)),
            pl.BlockSpec((1, N_MOD, D_MODEL), lambda s: (nxt(s) // tiles_per_seq, 0, 0)),
            _const_spec((1, D_MODEL)),
            hbm, hbm, hbm,
            _const_spec((1, D_MODEL)),
        ],
        out_specs=pl.BlockSpec((FFN_ROWS, D_MODEL), lambda s: (s, 0)),
        out_shape=jax.ShapeDtypeStruct((bsz * seq, D_MODEL), F32),
        scratch_shapes=[
            pltpu.VMEM((2, FFN_ROWS, D_MODEL), BF16),
            pltpu.VMEM((D_MODEL, D_FF), BF16),
            pltpu.VMEM((D_MODEL, D_FF), BF16),
            pltpu.VMEM((D_FF, D_MODEL), BF16),
            pltpu.VMEM((W_SLOTS, D_MODEL, MXU_COLS), F32),
            pltpu.VMEM((W_SLOTS, FFN_W_ROWS_OUT, D_MODEL), F32),
            pltpu.SemaphoreType.DMA((W_SLOTS,)),
            pltpu.SemaphoreType.DMA((W_SLOTS,)),
        ],
        compiler_params=pltpu.CompilerParams(
            dimension_semantics=("arbitrary",), vmem_limit_bytes=VMEM_LIMIT_BYTES),
        name="ffn_final" if final else "ffn",
    )(x2, x2, mod3, mod3, norm_w, wg, wu, wd, final_w)
    return out.reshape(bsz, seq, D_MODEL)


def _mixer_kernel(x_hbm, modc_ref, modn_ref, nw_ref, wint_hbm,
                  convw_ref, convb_ref, dtb_ref, alog_ref, dskip_ref, ssdnw_ref,
                  poolw_in, poolb_ref, pools_ref, wout_hbm,
                  o_ref,
                  wz_ref, wxbc_ref, wdt_ref, wu_ref, poolw_ref, wout_ref, stage, sem,
                  z_s, raw_s, dtr_s, u_s, xbc_s, st_s, diff_s, zs_s, dtc_s, x_ring, x_sem,
                  *, layer, tiles_per_seq):
    step = pl.program_id(0)
    n_steps = pl.num_programs(0)
    tile_in_seq = step % tiles_per_seq
    next_starts_seq = tile_in_seq == tiles_per_seq - 1
    n_chunks = MIX_ROWS // CHUNK

    def x_copy(t):
        slot = t % X_SLOTS
        row0 = pl.multiple_of(t * MIX_ROWS, MIX_ROWS)
        return pltpu.make_async_copy(x_hbm.at[pl.ds(row0, MIX_ROWS), :], x_ring.at[slot], x_sem.at[slot])

    @pl.when(step == 0)
    def _():
        x_copy(0).start()
        x_copy(1).start()
        x_copy(0).wait()
        x_copy(1).wait()

    @pl.when(jnp.logical_and(step > 0, step + 1 < n_steps))
    def _():
        x_copy(step + 1).wait()

    @pl.when(step + 2 < n_steps)
    def _():
        x_copy(step + 2).start()

    xc_ref = x_ring.at[step % X_SLOTS]
    xn_ref = x_ring.at[(step + 1) % X_SLOTS]

    def in_norm(x, mod_ref):
        sh = mod_ref[0, 3:4, :]
        sc = mod_ref[0, 4:5, :]
        return (_rms(x, nw_ref[...]) * (1.0 + sc) + sh).astype(BF16)

    def rows(c):
        return slice(c * CHUNK, (c + 1) * CHUNK)

    def conv_base(c):
        return (c % n_chunks) * (CONV_PAD + CHUNK) + CONV_PAD

    def pool_base(c):
        return (c % n_chunks) * (POOL_PAD + CHUNK) + POOL_PAD

    pending = []

    def emit(k=None):
        n = len(pending) if k is None else min(k, len(pending))
        for _ in range(n):
            pending.pop(0)()

    def proj_pieces(c, h_n):
        rs = rows(c)

        def dense(dst, w_ref, lo, hi):
            def run():
                dst[rs, lo:hi] = _dot(h_n, w_ref[:, lo:hi])
            return run

        def slabs(dst, base, w_ref, lo, hi):
            def run():
                res = _dot(h_n, w_ref[:, lo:hi])
                for j in range(lo // LANES, hi // LANES):
                    dst[j, base:base + CHUNK, :] = res[:, j * LANES - lo:(j + 1) * LANES - lo]
            return run

        out = [slabs(raw_s, conv_base(c), wxbc_ref, k, k + MXU_COLS) for k in range(0, D_XBC, MXU_COLS)]
        out += [slabs(u_s, pool_base(c), wu_ref, k, k + MXU_COLS) for k in range(0, D_POOL, MXU_COLS)]
        out += [dense(z_s, wz_ref, k, k + MXU_COLS) for k in range(0, D_SSD, MXU_COLS)]
        out.append(dense(dtr_s, wdt_ref, 0, LANES))
        return out

    @pl.when(step == 0)
    def _():
        stream = _BlockStream(stage, sem)
        w_rows, w_cols = stage.shape[1], stage.shape[2]

        def cast_into(dst, r, c0):
            def run(block):
                dst[r:r + w_rows, c0:c0 + w_cols] = block.astype(BF16)
            return run

        def transposed_into(dst, c0):
            def run(block):
                dst[:, c0:c0 + block.shape[0]] = block.T.astype(BF16)
            return run

        def dt_into(block):
            lane = lax.broadcasted_iota(jnp.int32, (D_MODEL, LANES), 1)
            wdt_ref[...] = jnp.where(lane < N_HEADS, block.T, 0.0).astype(BF16)

        def feature_rows(f0):
            return lambda r, n: wint_hbm.at[layer, pl.ds(f0 + r, n), :]

        for src, dst, width in ((feature_rows(0), wz_ref, D_SSD), (feature_rows(D_SSD), wxbc_ref, D_XBC),
                                (feature_rows(DT_COL + N_HEADS), wu_ref, D_POOL)):
            for r in range(0, width, w_rows):
                stream.add(src(r, w_rows), transposed_into(dst, r))
        stream.add(feature_rows(DT_COL)(0, LANES), dt_into)
        for r in range(0, D_SSD + D_POOL, w_rows):
            stream.add(wout_hbm.at[layer, pl.ds(r, w_rows), :], cast_into(wout_ref, r, 0))
        stream.prime()
        poolw_ref[...] = poolw_in[...]
        stream.drain()
        raw_s[:, 0:CONV_PAD, :] = jnp.zeros((D_XBC // LANES, CONV_PAD, LANES), F32)
        u_s[:, 0:POOL_PAD, :] = jnp.zeros((D_POOL // LANES, POOL_PAD, LANES), F32)
        for c in range(n_chunks):
            for run in proj_pieces(c, in_norm(xc_ref[rows(c), :], modc_ref)):
                run()

    @pl.when(tile_in_seq == 0)
    def _():
        st_s[...] = jnp.zeros(st_s.shape, F32)

    gt = modc_ref[0, 5:6, :]
    a_neg = -jnp.exp(alog_ref[...])
    ri = lax.broadcasted_iota(jnp.int32, (CHUNK, CHUNK), 0)
    ci = lax.broadcasted_iota(jnp.int32, (CHUNK, CHUNK), 1)
    causal = ri >= ci
    tril = jnp.where(causal, 1.0, 0.0).astype(BF16)
    low_half = ci < HEAD_DIM
    low_half_row = lax.broadcasted_iota(jnp.int32, (1, LANES), 1) < HEAD_DIM

    def pre(c, seq_tile):
        rs = rows(c)
        last = c == n_chunks - 1

        def carry(hist):
            return jnp.where(next_starts_seq, jnp.zeros_like(hist), hist) if last else hist

        base, nbase = conv_base(c), conv_base(c + 1)
        for j in range(D_XBC // LANES):
            cols = slice(j * LANES, (j + 1) * LANES)
            xc = convb_ref[:, cols]
            for k in range(D_CONV):
                xc = xc + convw_ref[k:k + 1, cols] * raw_s[j, pl.ds(base - (D_CONV - 1) + k, CHUNK), :]
            raw_s[j, nbase - CONV_PAD:nbase, :] = carry(raw_s[j, base + CHUNK - CONV_PAD:base + CHUNK, :])
            xbc_s[rs, cols] = _silu(xc)
            if j % 2 == 1:
                emit(1)
        base, nbase = pool_base(c), pool_base(c + 1)
        diffs = []
        for gi, w in enumerate(POOL_WINDOWS):
            halves = []
            for j in range(gi * POOL_GROUP_DIM // LANES, (gi + 1) * POOL_GROUP_DIM // LANES):
                u_g = u_s[j, base:base + CHUNK, :]
                s = u_g
                for k in range(1, w):
                    s = s + u_s[j, pl.ds(base - k, CHUNK), :]
                if c == 0:
                    pos = seq_tile * MIX_ROWS + 1 + lax.broadcasted_iota(jnp.int32, (CHUNK, LANES), 0)
                    pooled = s / jnp.minimum(pos, w).astype(F32)
                else:
                    pooled = s * (1.0 / w)
                halves.append(pooled - u_g)
                u_s[j, nbase - POOL_PAD:nbase, :] = carry(u_s[j, base + CHUNK - POOL_PAD:base + CHUNK, :])
            diffs.append(jnp.concatenate(halves, axis=1).astype(BF16))
            emit(1)
        zs = _silu(z_s[rs, :])
        dt_c = jax.nn.softplus(dtr_s[rs, :] + dtb_ref[...])
        emit()
        return diffs, zs, dt_c

    def ssd(c, zs, dt_c):
        rs = rows(c)
        p_hi, p_mid, p_lo = _split3(dt_c * a_neg)
        cs = _dot(tril, p_hi) + _dot(tril, p_mid) + _dot(tril, p_lo)
        cs_t = cs.T
        dt_t = dt_c.T
        w_t = dt_t * jnp.exp(cs_t[:, CHUNK - 1:CHUNK] - cs_t)
        xs_c = xbc_s[rs, 0:D_SSD]
        emit(1)
        y_pairs = []
        for g in range(N_GROUPS):
            b0 = D_SSD + g * D_STATE
            c0 = D_SSD + N_GROUPS * D_STATE + g * D_STATE
            b_t = xbc_s[rs, b0:b0 + D_STATE].T
            c_g = xbc_s[rs, c0:c0 + D_STATE]
            cb = _dot(c_g.astype(BF16), b_t.astype(BF16))
            for pr in range(HEADS_PER_GROUP // 2):
                lhs, bw_t, cd = [], [], []
                for r in (2 * pr, 2 * pr + 1):
                    hd = g * HEADS_PER_GROUP + r
                    cs_bc = jnp.broadcast_to(cs[:, hd:hd + 1], (CHUNK, CHUNK))
                    decay = jnp.exp(jnp.where(causal, cs_bc - cs_t[hd:hd + 1, :], -jnp.inf))
                    scores = decay * (cb * dt_t[hd:hd + 1, :])
                    sd = jnp.exp(cs_bc)
                    lhs.append(jnp.concatenate([scores, c_g * sd], axis=1).astype(BF16))
                    bw_t.append((b_t * w_t[hd:hd + 1, :]).astype(BF16))
                    cd.append(sd[CHUNK - 1:CHUNK, :])
                col = (g * HEADS_PER_GROUP + 2 * pr) * HEAD_DIM
                x_p = xs_c[:, col:col + LANES]
                st = st_s[:, col:col + LANES]
                rhs = jnp.concatenate([x_p, st], axis=0).astype(BF16)
                y_pairs.append(jnp.where(low_half, _dot(lhs[0], rhs), _dot(lhs[1], rhs)))
                zero = jnp.zeros_like(x_p)
                x_split = jnp.concatenate(
                    [jnp.where(low_half, x_p, zero), jnp.where(low_half, zero, x_p)], axis=0)
                upd = _dot(jnp.concatenate(bw_t, axis=1), x_split.astype(BF16))
                st_s[:, col:col + LANES] = st * jnp.where(low_half_row, cd[0], cd[1]) + upd
                emit(1)
        y = jnp.concatenate(y_pairs, axis=1) + dskip_ref[...] * xs_c
        yz = y * zs
        gw = D_SSD // N_GROUPS
        parts = []
        for g in range(N_GROUPS):
            blk = yz[:, g * gw:(g + 1) * gw]
            ms = jnp.mean(blk * blk, axis=-1, keepdims=True)
            parts.append(blk * lax.rsqrt(ms + EPS))
        return (jnp.concatenate(parts, axis=1) * ssdnw_ref[...]).astype(BF16)

    def post_pieces(c, y_ssd, diffs):
        rs = rows(c)
        outs = [None] * len(POOL_WINDOWS)

        def pool_piece(gi):
            def run():
                lo, hi = gi * POOL_GROUP_DIM, (gi + 1) * POOL_GROUP_DIM
                o = (_dot(diffs[gi], poolw_ref[gi]) + poolb_ref[:, lo:hi]) * pools_ref[:, lo:hi]
                outs[gi] = o.astype(BF16)
            return run

        def out_piece(lo, hi):
            def run():
                y_pool = jnp.concatenate(outs, axis=1)
                out = (_dot(y_ssd, wout_ref[0:D_SSD, lo:hi])
                       + _dot(y_pool, wout_ref[D_SSD:D_SSD + D_POOL, lo:hi]))
                o_ref[rs, lo:hi] = xc_ref[rs, lo:hi] + gt[:, lo:hi] * out
            return run

        return ([pool_piece(gi) for gi in range(len(POOL_WINDOWS))]
                + [out_piece(k, k + MXU_COLS) for k in range(0, D_MODEL, MXU_COLS)])

    def hand_over(vals):
        diffs, zs, dt_c = vals
        for gi in range(len(POOL_WINDOWS)):
            diff_s[:, gi * POOL_GROUP_DIM:(gi + 1) * POOL_GROUP_DIM] = diffs[gi]
        zs_s[...] = zs
        dtc_s[...] = dt_c

    @pl.when(step == 0)
    def _():
        hand_over(pre(0, tile_in_seq))

    diffs = [diff_s[:, gi * POOL_GROUP_DIM:(gi + 1) * POOL_GROUP_DIM] for gi in range(len(POOL_WINDOWS))]
    zs, dt_c = zs_s[...], dtc_s[...]
    h_n = in_norm(xn_ref[rows(0), :], modn_ref)
    pending.extend(proj_pieces(0, h_n))
    for c in range(n_chunks):
        y_ssd = ssd(c, zs, dt_c)
        pending.extend(post_pieces(c, y_ssd, diffs))
        if c + 1 < n_chunks:
            h_n = in_norm(xn_ref[rows(c + 1), :], modn_ref)
            emit(2)
            diffs, zs, dt_c = pre(c + 1, tile_in_seq)
            pending.extend(proj_pieces(c + 1, h_n))
        else:
            hand_over(pre(0, (step + 1) % tiles_per_seq))
            emit()


def _mixer(x, mod3, norm_w, w_in_t, convw, convb, dtb, alog, dskip, ssdnw,
           poolw, poolb, pools, w_out, *, layer):
    bsz, seq, _ = x.shape
    tiles_per_seq = seq // MIX_ROWS
    n_chunks = MIX_ROWS // CHUNK
    steps = bsz * tiles_per_seq
    assert steps >= X_SLOTS
    nxt = lambda s: jnp.minimum(s + 1, steps - 1)
    hbm = pl.BlockSpec(memory_space=pl.ANY)
    vmem_consts = [convw, convb, dtb, alog, dskip, ssdnw, poolw, poolb, pools]
    x2 = x.reshape(bsz * seq, D_MODEL)
    out = pl.pallas_call(
        functools.partial(_mixer_kernel, layer=layer, tiles_per_seq=tiles_per_seq),
        grid=(steps,),
        in_specs=[
            hbm,
            pl.BlockSpec((1, N_MOD, D_MODEL), lambda s: (s // tiles_per_seq, 0, 0)),
            pl.BlockSpec((1, N_MOD, D_MODEL), lambda s: (nxt(s) // tiles_per_seq, 0, 0)),
            _const_spec(norm_w.shape), hbm,
        ] + [_const_spec(a.shape) for a in vmem_consts] + [hbm],
        out_specs=pl.BlockSpec((MIX_ROWS, D_MODEL), lambda s: (s, 0)),
        out_shape=jax.ShapeDtypeStruct((bsz * seq, D_MODEL), F32),
        scratch_shapes=[
            pltpu.VMEM((D_MODEL, D_SSD), BF16),
            pltpu.VMEM((D_MODEL, D_XBC), BF16),
            pltpu.VMEM((D_MODEL, LANES), BF16),
            pltpu.VMEM((D_MODEL, D_POOL), BF16),
            pltpu.VMEM(poolw.shape, BF16),
            pltpu.VMEM((D_SSD + D_POOL, D_MODEL), BF16),
            pltpu.VMEM((W_SLOTS, MIX_W_ROWS, D_MODEL), F32),
            pltpu.SemaphoreType.DMA((W_SLOTS,)),
            pltpu.VMEM((MIX_ROWS, D_SSD), F32),
            pltpu.VMEM((D_XBC // LANES, n_chunks * (CONV_PAD + CHUNK), LANES), F32),
            pltpu.VMEM((MIX_ROWS, LANES), F32),
            pltpu.VMEM((D_POOL // LANES, n_chunks * (POOL_PAD + CHUNK), LANES), F32),
            pltpu.VMEM((MIX_ROWS, D_XBC), F32),
            pltpu.VMEM((D_STATE, D_SSD), F32),
            pltpu.VMEM((CHUNK, D_POOL), BF16),
            pltpu.VMEM((CHUNK, D_SSD), F32),
            pltpu.VMEM((CHUNK, LANES), F32),
            pltpu.VMEM((X_SLOTS, MIX_ROWS, D_MODEL), F32),
            pltpu.SemaphoreType.DMA((X_SLOTS,)),
        ],
        input_output_aliases={0: 0},
        compiler_params=pltpu.CompilerParams(
            dimension_semantics=("arbitrary",), vmem_limit_bytes=VMEM_LIMIT_BYTES),
        name="mixer",
    )(x2, mod3, mod3, norm_w, w_in_t, *vmem_consts, w_out)
    return out.reshape(bsz, seq, D_MODEL)


def _pad_lanes(v):
    return jnp.pad(v.reshape(1, -1), ((0, 0), (0, LANES - v.shape[0])))


def kernel(x, c, w_ada, b_ada, ffn1_norm, ffn1_w_gate, ffn1_w_up, ffn1_w_down, mix_norm, w_in, conv_w, conv_b, dt_bias, a_log, d_skip, ssd_norm_w, pool_w, pool_b, pool_scale, w_out, ffn2_norm, ffn2_w_gate, ffn2_w_up, ffn2_w_down, final_norm):
    bsz = x.shape[0]
    depth = w_ada.shape[0]
    row = lambda v: v.reshape(1, -1)
    for i in range(depth):
        mod3 = _adaln(c, w_ada[i], row(b_ada[i])).reshape(bsz, N_MOD, D_MODEL)
        last = i == depth - 1

        x = _ffn(x, mod3, row(ffn1_norm[i]), ffn1_w_gate, ffn1_w_up, ffn1_w_down, row(final_norm),
                 layer=i, mod_idx=0, final=False)

        x = _mixer(
            x, mod3, row(mix_norm[i]), jnp.swapaxes(w_in, 1, 2),
            conv_w[i], row(conv_b[i]), _pad_lanes(dt_bias[i]), _pad_lanes(a_log[i]),
            row(jnp.repeat(d_skip[i], HEAD_DIM)), row(ssd_norm_w[i]),
            pool_w[i].astype(BF16), row(pool_b[i]), row(pool_scale[i]), w_out, layer=i)

        x = _ffn(x, mod3, row(ffn2_norm[i]), ffn2_w_gate, ffn2_w_up, ffn2_w_down, row(final_norm),
                 layer=i, mod_idx=6, final=last)
    return x
```

```python
import functools

import jax
import jax.numpy as jnp
from jax import lax
from jax.experimental import pallas as pl
from jax.experimental.pallas import tpu as pltpu

F32 = jnp.float32
BF16 = jnp.bfloat16

D_MODEL = 1024
D_FF = 2816
N_MOD = 9
FFN_RES = 0.5
EPS = 1e-6
D_SSD = 1024
HEAD_DIM = 64
N_HEADS = 16
N_GROUPS = 4
HEADS_PER_GROUP = N_HEADS // N_GROUPS
D_STATE = 128
D_CONV = 4
CHUNK = 128
D_POOL = 1024
POOL_WINDOWS = (2, 4, 8, 16)
POOL_GROUP_DIM = 256
D_XBC = D_SSD + 2 * N_GROUPS * D_STATE
DT_COL = D_SSD + D_XBC

LANES = 128
SUBLANES = 8
MXU_COLS = 256
VMEM_LIMIT_BYTES = 60 * 1024 * 1024

FFN_ROWS = 1024
FFN_SUBTILES = 4
NORM_SLICES = 8
FFN_W_ROWS_OUT = 256
MIX_W_ROWS = 256
W_SLOTS = 8
FFN_W_SLOTS = 4
X_SLOTS = 3
MIX_ROWS = 512
ADA_COLS = 1024
CONV_PAD = SUBLANES
POOL_PAD = 16

assert HEAD_DIM * 2 == LANES and D_STATE == LANES and CHUNK == LANES


def _dot(a, b):
    return jnp.dot(a, b, preferred_element_type=F32)


def _split3(v):
    hi = v.astype(BF16)
    r1 = v - hi.astype(F32)
    mid = r1.astype(BF16)
    lo = (r1 - mid.astype(F32)).astype(BF16)
    return hi, mid, lo


def _silu(v):
    return v * jax.nn.sigmoid(v)


def _rms(v, w):
    ms = jnp.mean(v * v, axis=-1, keepdims=True)
    return v * lax.rsqrt(ms + EPS) * w


def _const_spec(shape):
    zeros = (0,) * len(shape)
    return pl.BlockSpec(shape, lambda *_: zeros, pipeline_mode=pl.Buffered(1))


def _adaln_kernel(c_ref, w_ref, b_ref, o_ref):
    a_hi, a_mid, _ = _split3(_silu(c_ref[...]))
    w_hi, w_mid, _ = _split3(w_ref[...])
    acc = _dot(a_hi, w_hi) + _dot(a_mid, w_hi) + _dot(a_hi, w_mid)
    o_ref[...] = acc + b_ref[...]


def _adaln(c, w_ada, b_ada):
    bsz = c.shape[0]
    n = w_ada.shape[1]
    c_pad = jnp.pad(c, ((0, SUBLANES - bsz), (0, 0)))
    mod = pl.pallas_call(
        _adaln_kernel,
        grid=(n // ADA_COLS,),
        in_specs=[
            pl.BlockSpec((SUBLANES, D_MODEL), lambda j: (0, 0)),
            pl.BlockSpec((D_MODEL, ADA_COLS), lambda j: (0, j)),
            pl.BlockSpec((1, ADA_COLS), lambda j: (0, j)),
        ],
        out_specs=pl.BlockSpec((SUBLANES, ADA_COLS), lambda j: (0, j)),
        out_shape=jax.ShapeDtypeStruct((SUBLANES, n), F32),
        compiler_params=pltpu.CompilerParams(
            dimension_semantics=("arbitrary",), vmem_limit_bytes=VMEM_LIMIT_BYTES),
        name="adaln",
    )(c_pad, w_ada, b_ada)
    return mod[:bsz]


def _ring_tiles(x_hbm, ring, sem):
    step, n_steps = pl.program_id(0), pl.num_programs(0)
    n_slots, rows = ring.shape[0], ring.shape[1]

    def copy(t):
        row0 = pl.multiple_of(t * rows, rows)
        return pltpu.make_async_copy(x_hbm.at[pl.ds(row0, rows), :], ring.at[t % n_slots], sem.at[t % n_slots])

    @pl.when(step == 0)
    def _():
        copy(0).start()
        copy(1).start()
        copy(0).wait()
        copy(1).wait()

    @pl.when(jnp.logical_and(step > 0, step + 1 < n_steps))
    def _():
        copy(step + 1).wait()

    @pl.when(step + 2 < n_steps)
    def _():
        copy(step + 2).start()

    return ring.at[step % n_slots], ring.at[(step + 1) % n_slots]


class _BlockStream:
    def __init__(self, stage_ref, sem):
        self.stage, self.sem, self.blocks = stage_ref, sem, []
        self.n_slots = stage_ref.shape[0]
        self.done = 0

    def add(self, src, sink):
        self.blocks.append((src, sink))

    def _slot(self, i):
        return self.stage.at[i % self.n_slots, pl.ds(0, self.blocks[i][0].shape[0])]

    def _copy(self, i):
        return pltpu.make_async_copy(self.blocks[i][0], self._slot(i), self.sem.at[i % self.n_slots])

    def prime(self):
        for i in range(min(self.n_slots, len(self.blocks))):
            self._copy(i).start()

    def advance(self, n):
        for i in range(self.done, self.done + n):
            self._copy(i).wait()
            self.blocks[i][1](self._slot(i)[...])
            if i + self.n_slots < len(self.blocks):
                self._copy(i + self.n_slots).start()
        self.done += n

    def drain(self):
        self.advance(len(self.blocks) - self.done)


def _stream_rows_bf16(stream, w_hbm, layer, dst_ref):
    rows = stream.stage.shape[1]

    def sink(r):
        def run(block):
            dst_ref[r:r + rows, :] = block.astype(BF16)
        return run

    for r in range(0, dst_ref.shape[0], rows):
        stream.add(w_hbm.at[layer, pl.ds(r, rows), :], sink(r))


def _stream_cols_bf16(stream, w_hbm, layer, dst_ref, c0):
    cols = stream.stage.shape[2]

    def run(block):
        dst_ref[:, c0:c0 + cols] = block.astype(BF16)

    stream.add(w_hbm.at[layer, :, pl.ds(c0, cols)], run)


def _ffn_kernel(x_hbm, modc_ref, modn_ref, nw_ref, wg_hbm, wu_hbm, wd_hbm, fw_ref, o_ref,
                h_s, wg_ref, wu_ref, wd_ref, stage_in, stage_out, sem_in, sem_out, x_ring, x_sem,
                *, layer, mod_idx, final):
    def in_norm(x, mod_ref):
        sh = mod_ref[0, pl.ds(mod_idx, 1), :]
        sc = mod_ref[0, pl.ds(mod_idx + 1, 1), :]
        return (_rms(x, nw_ref[...]) * (1.0 + sc) + sh).astype(BF16)

    step = pl.program_id(0)
    xc_ref, xn_ref = _ring_tiles(x_hbm, x_ring, x_sem)
    cur = h_s.at[step % 2]
    nxt = h_s.at[(step + 1) % 2]

    n_chunks = D_FF // MXU_COLS
    slice_rows = FFN_ROWS // NORM_SLICES
    sub_rows = FFN_ROWS // FFN_SUBTILES
    slices_per_sub = NORM_SLICES // FFN_SUBTILES

    def body(first):
        if first:
            s_in, s_out = _BlockStream(stage_in, sem_in), _BlockStream(stage_out, sem_out)
            for k in range(n_chunks):
                _stream_cols_bf16(s_in, wg_hbm, layer, wg_ref, k * MXU_COLS)
                _stream_cols_bf16(s_in, wu_hbm, layer, wu_ref, k * MXU_COLS)
            _stream_rows_bf16(s_out, wd_hbm, layer, wd_ref)
            s_in.prime()
            s_out.prime()
            cur[...] = in_norm(xc_ref[...], modc_ref)

        gt = modc_ref[0, pl.ds(mod_idx + 2, 1), :]
        for sub in range(FFN_SUBTILES):
            rows = slice(sub * sub_rows, (sub + 1) * sub_rows)
            acts = []
            for k in range(n_chunks):
                if first and sub == 0:
                    s_in.advance(2)
                if k < slices_per_sub:
                    j = sub * slices_per_sub + k
                    rs = slice(j * slice_rows, (j + 1) * slice_rows)
                    nxt[rs, :] = in_norm(xn_ref[rs, :], modn_ref)
                cols = slice(k * MXU_COLS, (k + 1) * MXU_COLS)
                h = cur[rows, :]
                g = _dot(h, wg_ref[:, cols])
                u = _dot(h, wu_ref[:, cols])
                acts.append((_silu(g) * u).astype(BF16))
            if first and sub == 0:
                s_out.drain()
            a = jnp.concatenate(acts, axis=1)
            o = _dot(a, wd_ref[...])
            out = xc_ref[rows, :] + (FFN_RES * gt) * o
            if final:
                out = _rms(out, fw_ref[...])
            o_ref[rows, :] = out

    @pl.when(step == 0)
    def _():
        body(True)

    @pl.when(step > 0)
    def _():
        body(False)


def _ffn(x, mod3, norm_w, wg, wu, wd, final_w, *, layer, mod_idx, final):
    bsz, seq, _ = x.shape
    tiles_per_seq = seq // FFN_ROWS
    steps = bsz * tiles_per_seq
    nxt = lambda s: jnp.minimum(s + 1, steps - 1)
    x2 = x.reshape(bsz * seq, D_MODEL)
    hbm = pl.BlockSpec(memory_space=pl.ANY)
    out = pl.pallas_call(
        functools.partial(_ffn_kernel, layer=layer, mod_idx=mod_idx, final=final),
        grid=(steps,),
        in_specs=[
            hbm,
            pl.BlockSpec((1, N_MOD, D_MODEL), lambda s: (s // tiles_per_seq, 0, 0)),
            pl.BlockSpec((1, N_MOD, D_MODEL), lambda s: (nxt(s) // tiles_per_seq, 0, 0)),
            _const_spec((1, D_MODEL)),
            hbm, hbm, hbm,
            _const_spec((1, D_MODEL)),
        ],
        out_specs=pl.BlockSpec((FFN_ROWS, D_MODEL), lambda s: (s, 0)),
        out_shape=jax.ShapeDtypeStruct((bsz * seq, D_MODEL), F32),
        scratch_shapes=[
            pltpu.VMEM((2, FFN_ROWS, D_MODEL), BF16),
            pltpu.VMEM((D_MODEL, D_FF), BF16),
            pltpu.VMEM((D_MODEL, D_FF), BF16),
            pltpu.VMEM((D_FF, D_MODEL), BF16),
            pltpu.VMEM((FFN_W_SLOTS, D_MODEL, MXU_COLS), F32),
            pltpu.VMEM((FFN_W_SLOTS, FFN_W_ROWS_OUT, D_MODEL), F32),
            pltpu.SemaphoreType.DMA((FFN_W_SLOTS,)),
            pltpu.SemaphoreType.DMA((FFN_W_SLOTS,)),
            pltpu.VMEM((X_SLOTS, FFN_ROWS, D_MODEL), F32),
            pltpu.SemaphoreType.DMA((X_SLOTS,)),
        ],
        compiler_params=pltpu.CompilerParams(
            dimension_semantics=("arbitrary",), vmem_limit_bytes=VMEM_LIMIT_BYTES),
        name="ffn_final" if final else "ffn",
    )(x2, mod3, mod3, norm_w, wg, wu, wd, final_w)
    return out.reshape(bsz, seq, D_MODEL)


def _mixer_kernel(x_hbm, modc_ref, modn_ref, nw_ref, wint_hbm,
                  convw_ref, convb_ref, dtb_ref, alog_ref, dskip_ref, ssdnw_ref,
                  poolw_in, poolb_ref, pools_ref, wout_hbm,
                  o_ref,
                  wz_ref, wxbc_ref, wdt_ref, wu_ref, poolw_ref, wout_ref, stage, sem,
                  z_s, raw_s, dtr_s, u_s, xbc_s, st_s, diff_s, zs_s, dtc_s, x_ring, x_sem,
                  *, layer, tiles_per_seq):
    step = pl.program_id(0)
    tile_in_seq = step % tiles_per_seq
    next_starts_seq = tile_in_seq == tiles_per_seq - 1
    n_chunks = MIX_ROWS // CHUNK
    xc_ref, xn_ref = _ring_tiles(x_hbm, x_ring, x_sem)

    def in_norm(x, mod_ref):
        sh = mod_ref[0, 3:4, :]
        sc = mod_ref[0, 4:5, :]
        return (_rms(x, nw_ref[...]) * (1.0 + sc) + sh).astype(BF16)

    def rows(c):
        return slice(c * CHUNK, (c + 1) * CHUNK)

    def conv_base(c):
        return (c % n_chunks) * (CONV_PAD + CHUNK) + CONV_PAD

    def pool_base(c):
        return (c % n_chunks) * (POOL_PAD + CHUNK) + POOL_PAD

    pending = []

    def emit(k=None):
        n = len(pending) if k is None else min(k, len(pending))
        for _ in range(n):
            pending.pop(0)()

    def proj_pieces(c, h_n):
        rs = rows(c)

        def dense(dst, w_ref, lo, hi):
            def run():
                dst[rs, lo:hi] = _dot(h_n, w_ref[:, lo:hi])
            return run

        def slabs(dst, base, w_ref, lo, hi):
            def run():
                res = _dot(h_n, w_ref[:, lo:hi])
                for j in range(lo // LANES, hi // LANES):
                    dst[j, base:base + CHUNK, :] = res[:, j * LANES - lo:(j + 1) * LANES - lo]
            return run

        out = [slabs(raw_s, conv_base(c), wxbc_ref, k, k + MXU_COLS) for k in range(0, D_XBC, MXU_COLS)]
        out += [slabs(u_s, pool_base(c), wu_ref, k, k + MXU_COLS) for k in range(0, D_POOL, MXU_COLS)]
        out += [dense(z_s, wz_ref, k, k + MXU_COLS) for k in range(0, D_SSD, MXU_COLS)]
        out.append(dense(dtr_s, wdt_ref, 0, LANES))
        return out

    @pl.when(step == 0)
    def _():
        stream = _BlockStream(stage, sem)
        w_rows, w_cols = stage.shape[1], stage.shape[2]

        def cast_into(dst, r, c0):
            def run(block):
                dst[r:r + w_rows, c0:c0 + w_cols] = block.astype(BF16)
            return run

        def transposed_into(dst, c0):
            def run(block):
                dst[:, c0:c0 + block.shape[0]] = block.T.astype(BF16)
            return run

        def dt_into(block):
            lane = lax.broadcasted_iota(jnp.int32, (D_MODEL, LANES), 1)
            wdt_ref[...] = jnp.where(lane < N_HEADS, block.T, 0.0).astype(BF16)

        def feature_rows(f0):
            return lambda r, n: wint_hbm.at[layer, pl.ds(f0 + r, n), :]

        for src, dst, width in ((feature_rows(0), wz_ref, D_SSD), (feature_rows(D_SSD), wxbc_ref, D_XBC),
                                (feature_rows(DT_COL + N_HEADS), wu_ref, D_POOL)):
            for r in range(0, width, w_rows):
                stream.add(src(r, w_rows), transposed_into(dst, r))
        stream.add(feature_rows(DT_COL)(0, LANES), dt_into)
        for r in range(0, D_SSD + D_POOL, w_rows):
            stream.add(wout_hbm.at[layer, pl.ds(r, w_rows), :], cast_into(wout_ref, r, 0))
        stream.prime()
        poolw_ref[...] = poolw_in[...]
        stream.drain()
        raw_s[:, 0:CONV_PAD, :] = jnp.zeros((D_XBC // LANES, CONV_PAD, LANES), F32)
        u_s[:, 0:POOL_PAD, :] = jnp.zeros((D_POOL // LANES, POOL_PAD, LANES), F32)
        for c in range(n_chunks):
            for run in proj_pieces(c, in_norm(xc_ref[rows(c), :], modc_ref)):
                run()

    @pl.when(tile_in_seq == 0)
    def _():
        st_s[...] = jnp.zeros(st_s.shape, F32)

    gt = modc_ref[0, 5:6, :]
    a_neg = -jnp.exp(alog_ref[...])
    ri = lax.broadcasted_iota(jnp.int32, (CHUNK, CHUNK), 0)
    ci = lax.broadcasted_iota(jnp.int32, (CHUNK, CHUNK), 1)
    causal = ri >= ci
    tril = jnp.where(causal, 1.0, 0.0).astype(BF16)
    low_half = ci < HEAD_DIM
    low_half_row = lax.broadcasted_iota(jnp.int32, (1, LANES), 1) < HEAD_DIM

    def pre(c, seq_tile):
        rs = rows(c)
        last = c == n_chunks - 1

        def carry(hist):
            return jnp.where(next_starts_seq, jnp.zeros_like(hist), hist) if last else hist

        base, nbase = conv_base(c), conv_base(c + 1)
        for j in range(D_XBC // LANES):
            cols = slice(j * LANES, (j + 1) * LANES)
            xc = convb_ref[:, cols]
            for k in range(D_CONV):
                xc = xc + convw_ref[k:k + 1, cols] * raw_s[j, pl.ds(base - (D_CONV - 1) + k, CHUNK), :]
            raw_s[j, nbase - CONV_PAD:nbase, :] = carry(raw_s[j, base + CHUNK - CONV_PAD:base + CHUNK, :])
            xbc_s[rs, cols] = _silu(xc)
            if j % 2 == 1:
                emit(1)
        base, nbase = pool_base(c), pool_base(c + 1)
        diffs = []
        for gi, w in enumerate(POOL_WINDOWS):
            halves = []
            for j in range(gi * POOL_GROUP_DIM // LANES, (gi + 1) * POOL_GROUP_DIM // LANES):
                u_g = u_s[j, base:base + CHUNK, :]
                s = u_g
                for k in range(1, w):
                    s = s + u_s[j, pl.ds(base - k, CHUNK), :]
                if c == 0:
                    pos = seq_tile * MIX_ROWS + 1 + lax.broadcasted_iota(jnp.int32, (CHUNK, LANES), 0)
                    pooled = s / jnp.minimum(pos, w).astype(F32)
                else:
                    pooled = s * (1.0 / w)
                halves.append(pooled - u_g)
                u_s[j, nbase - POOL_PAD:nbase, :] = carry(u_s[j, base + CHUNK - POOL_PAD:base + CHUNK, :])
            diffs.append(jnp.concatenate(halves, axis=1).astype(BF16))
            emit(1)
        zs = _silu(z_s[rs, :])
        dt_c = jax.nn.softplus(dtr_s[rs, :] + dtb_ref[...])
        emit()
        return diffs, zs, dt_c

    def ssd(c, zs, dt_c):
        rs = rows(c)
        p_hi, p_mid, p_lo = _split3(dt_c * a_neg)
        cs = _dot(tril, p_hi) + _dot(tril, p_mid) + _dot(tril, p_lo)
        cs_t = cs.T
        dt_t = dt_c.T
        w_t = dt_t * jnp.exp(cs_t[:, CHUNK - 1:CHUNK] - cs_t)
        xs_c = xbc_s[rs, 0:D_SSD]
        emit(1)
        y_pairs = []
        for g in range(N_GROUPS):
            b0 = D_SSD + g * D_STATE
            c0 = D_SSD + N_GROUPS * D_STATE + g * D_STATE
            b_t = xbc_s[rs, b0:b0 + D_STATE].T
            c_g = xbc_s[rs, c0:c0 + D_STATE]
            cb = _dot(c_g.astype(BF16), b_t.astype(BF16))
            for pr in range(HEADS_PER_GROUP // 2):
                lhs, bw_t, cd = [], [], []
                for r in (2 * pr, 2 * pr + 1):
                    hd = g * HEADS_PER_GROUP + r
                    cs_bc = jnp.broadcast_to(cs[:, hd:hd + 1], (CHUNK, CHUNK))
                    decay = jnp.exp(jnp.where(causal, cs_bc - cs_t[hd:hd + 1, :], -jnp.inf))
                    scores = decay * (cb * dt_t[hd:hd + 1, :])
                    sd = jnp.exp(cs_bc)
                    lhs.append(jnp.concatenate([scores, c_g * sd], axis=1).astype(BF16))
                    bw_t.append((b_t * w_t[hd:hd + 1, :]).astype(BF16))
                    cd.append(sd[CHUNK - 1:CHUNK, :])
                col = (g * HEADS_PER_GROUP + 2 * pr) * HEAD_DIM
                x_p = xs_c[:, col:col + LANES]
                st = st_s[:, col:col + LANES]
                rhs = jnp.concatenate([x_p, st], axis=0).astype(BF16)
                y_pairs.append(jnp.where(low_half, _dot(lhs[0], rhs), _dot(lhs[1], rhs)))
                zero = jnp.zeros_like(x_p)
                x_split = jnp.concatenate(
                    [jnp.where(low_half, x_p, zero), jnp.where(low_half, zero, x_p)], axis=0)
                upd = _dot(jnp.concatenate(bw_t, axis=1), x_split.astype(BF16))
                st_s[:, col:col + LANES] = st * jnp.where(low_half_row, cd[0], cd[1]) + upd
                emit(1)
        y = jnp.concatenate(y_pairs, axis=1) + dskip_ref[...] * xs_c
        yz = y * zs
        gw = D_SSD // N_GROUPS
        parts = []
        for g in range(N_GROUPS):
            blk = yz[:, g * gw:(g + 1) * gw]
            ms = jnp.mean(blk * blk, axis=-1, keepdims=True)
            parts.append(blk * lax.rsqrt(ms + EPS))
        return (jnp.concatenate(parts, axis=1) * ssdnw_ref[...]).astype(BF16)

    def post_pieces(c, y_ssd, diffs):
        rs = rows(c)
        outs = [None] * len(POOL_WINDOWS)

        def pool_piece(gi):
            def run():
                lo, hi = gi * POOL_GROUP_DIM, (gi + 1) * POOL_GROUP_DIM
                o = (_dot(diffs[gi], poolw_ref[gi]) + poolb_ref[:, lo:hi]) * pools_ref[:, lo:hi]
                outs[gi] = o.astype(BF16)
            return run

        def out_piece(lo, hi):
            def run():
                y_pool = jnp.concatenate(outs, axis=1)
                out = (_dot(y_ssd, wout_ref[0:D_SSD, lo:hi])
                       + _dot(y_pool, wout_ref[D_SSD:D_SSD + D_POOL, lo:hi]))
                o_ref[rs, lo:hi] = xc_ref[rs, lo:hi] + gt[:, lo:hi] * out
            return run

        return ([pool_piece(gi) for gi in range(len(POOL_WINDOWS))]
                + [out_piece(k, k + MXU_COLS) for k in range(0, D_MODEL, MXU_COLS)])

    def hand_over(vals):
        diffs, zs, dt_c = vals
        for gi in range(len(POOL_WINDOWS)):
            diff_s[:, gi * POOL_GROUP_DIM:(gi + 1) * POOL_GROUP_DIM] = diffs[gi]
        zs_s[...] = zs
        dtc_s[...] = dt_c

    @pl.when(step == 0)
    def _():
        hand_over(pre(0, tile_in_seq))

    diffs = [diff_s[:, gi * POOL_GROUP_DIM:(gi + 1) * POOL_GROUP_DIM] for gi in range(len(POOL_WINDOWS))]
    zs, dt_c = zs_s[...], dtc_s[...]
    h_n = in_norm(xn_ref[rows(0), :], modn_ref)
    pending.extend(proj_pieces(0, h_n))
    for c in range(n_chunks):
        y_ssd = ssd(c, zs, dt_c)
        pending.extend(post_pieces(c, y_ssd, diffs))
        if c + 1 < n_chunks:
            h_n = in_norm(xn_ref[rows(c + 1), :], modn_ref)
            emit(2)
            diffs, zs, dt_c = pre(c + 1, tile_in_seq)
            pending.extend(proj_pieces(c + 1, h_n))
        else:
            hand_over(pre(0, (step + 1) % tiles_per_seq))
            emit()


def _mixer(x, mod3, norm_w, w_in_t, convw, convb, dtb, alog, dskip, ssdnw,
           poolw, poolb, pools, w_out, *, layer):
    bsz, seq, _ = x.shape
    tiles_per_seq = seq // MIX_ROWS
    n_chunks = MIX_ROWS // CHUNK
    steps = bsz * tiles_per_seq
    assert steps >= X_SLOTS
    nxt = lambda s: jnp.minimum(s + 1, steps - 1)
    hbm = pl.BlockSpec(memory_space=pl.ANY)
    vmem_consts = [convw, convb, dtb, alog, dskip, ssdnw, poolw, poolb, pools]
    x2 = x.reshape(bsz * seq, D_MODEL)
    out = pl.pallas_call(
        functools.partial(_mixer_kernel, layer=layer, tiles_per_seq=tiles_per_seq),
        grid=(steps,),
        in_specs=[
            hbm,
            pl.BlockSpec((1, N_MOD, D_MODEL), lambda s: (s // tiles_per_seq, 0, 0)),
            pl.BlockSpec((1, N_MOD, D_MODEL), lambda s: (nxt(s) // tiles_per_seq, 0, 0)),
            _const_spec(norm_w.shape), hbm,
        ] + [_const_spec(a.shape) for a in vmem_consts] + [hbm],
        out_specs=pl.BlockSpec((MIX_ROWS, D_MODEL), lambda s: (s, 0)),
        out_shape=jax.ShapeDtypeStruct((bsz * seq, D_MODEL), F32),
        scratch_shapes=[
            pltpu.VMEM((D_MODEL, D_SSD), BF16),
            pltpu.VMEM((D_MODEL, D_XBC), BF16),
            pltpu.VMEM((D_MODEL, LANES), BF16),
            pltpu.VMEM((D_MODEL, D_POOL), BF16),
            pltpu.VMEM(poolw.shape, BF16),
            pltpu.VMEM((D_SSD + D_POOL, D_MODEL), BF16),
            pltpu.VMEM((W_SLOTS, MIX_W_ROWS, D_MODEL), F32),
            pltpu.SemaphoreType.DMA((W_SLOTS,)),
            pltpu.VMEM((MIX_ROWS, D_SSD), F32),
            pltpu.VMEM((D_XBC // LANES, n_chunks * (CONV_PAD + CHUNK), LANES), F32),
            pltpu.VMEM((MIX_ROWS, LANES), F32),
            pltpu.VMEM((D_POOL // LANES, n_chunks * (POOL_PAD + CHUNK), LANES), F32),
            pltpu.VMEM((MIX_ROWS, D_XBC), F32),
            pltpu.VMEM((D_STATE, D_SSD), F32),
            pltpu.VMEM((CHUNK, D_POOL), BF16),
            pltpu.VMEM((CHUNK, D_SSD), F32),
            pltpu.VMEM((CHUNK, LANES), F32),
            pltpu.VMEM((X_SLOTS, MIX_ROWS, D_MODEL), F32),
            pltpu.SemaphoreType.DMA((X_SLOTS,)),
        ],
        compiler_params=pltpu.CompilerParams(
            dimension_semantics=("arbitrary",), vmem_limit_bytes=VMEM_LIMIT_BYTES),
        name="mixer",
    )(x2, mod3, mod3, norm_w, w_in_t, *vmem_consts, w_out)
    return out.reshape(bsz, seq, D_MODEL)


def _pad_lanes(v):
    return jnp.pad(v.reshape(1, -1), ((0, 0), (0, LANES - v.shape[0])))


def kernel(x, c, w_ada, b_ada, ffn1_norm, ffn1_w_gate, ffn1_w_up, ffn1_w_down, mix_norm, w_in, conv_w, conv_b, dt_bias, a_log, d_skip, ssd_norm_w, pool_w, pool_b, pool_scale, w_out, ffn2_norm, ffn2_w_gate, ffn2_w_up, ffn2_w_down, final_norm):
    bsz = x.shape[0]
    depth = w_ada.shape[0]
    row = lambda v: v.reshape(1, -1)
    for i in range(depth):
        mod3 = _adaln(c, w_ada[i], row(b_ada[i])).reshape(bsz, N_MOD, D_MODEL)
        last = i == depth - 1

        x = _ffn(x, mod3, row(ffn1_norm[i]), ffn1_w_gate, ffn1_w_up, ffn1_w_down, row(final_norm),
                 layer=i, mod_idx=0, final=False)

        x = _mixer(
            x, mod3, row(mix_norm[i]), jnp.swapaxes(w_in, 1, 2),
            conv_w[i], row(conv_b[i]), _pad_lanes(dt_bias[i]), _pad_lanes(a_log[i]),
            row(jnp.repeat(d_skip[i], HEAD_DIM)), row(ssd_norm_w[i]),
            pool_w[i].astype(BF16), row(pool_b[i]), row(pool_scale[i]), w_out, layer=i)

        x = _ffn(x, mod3, row(ffn2_norm[i]), ffn2_w_gate, ffn2_w_up, ffn2_w_down, row(final_norm),
                 layer=i, mod_idx=6, final=last)
    return x
```

```python
import functools

import jax
import jax.numpy as jnp
from jax import lax
from jax.experimental import pallas as pl
from jax.experimental.pallas import tpu as pltpu

F32 = jnp.float32
BF16 = jnp.bfloat16

D_MODEL = 1024
D_FF = 2816
N_MOD = 9
FFN_RES = 0.5
EPS = 1e-6
D_SSD = 1024
HEAD_DIM = 64
N_HEADS = 16
N_GROUPS = 4
HEADS_PER_GROUP = N_HEADS // N_GROUPS
D_STATE = 128
D_CONV = 4
CHUNK = 128
D_POOL = 1024
POOL_WINDOWS = (2, 4, 8, 16)
POOL_GROUP_DIM = 256
D_XBC = D_SSD + 2 * N_GROUPS * D_STATE
DT_COL = D_SSD + D_XBC

LANES = 128
SUBLANES = 8
MXU_COLS = 256
VMEM_LIMIT_BYTES = 60 * 1024 * 1024

FFN_ROWS = 512
FFN_SUBTILES = 2
NORM_SLICES = 8
FFN_W_ROWS_OUT = 256
MIX_W_ROWS = 256
W_SLOTS = 8
FFN_W_SLOTS = 8
X_SLOTS = 3
MIX_ROWS = 512
ADA_COLS = 1024
CONV_PAD = SUBLANES
POOL_PAD = 16

assert HEAD_DIM * 2 == LANES and D_STATE == LANES and CHUNK == LANES


def _dot(a, b):
    return jnp.dot(a, b, preferred_element_type=F32)


def _split3(v):
    hi = v.astype(BF16)
    r1 = v - hi.astype(F32)
    mid = r1.astype(BF16)
    lo = (r1 - mid.astype(F32)).astype(BF16)
    return hi, mid, lo


def _silu(v):
    return v * jax.nn.sigmoid(v)


def _rms(v, w):
    ms = jnp.mean(v * v, axis=-1, keepdims=True)
    return v * lax.rsqrt(ms + EPS) * w


def _const_spec(shape):
    zeros = (0,) * len(shape)
    return pl.BlockSpec(shape, lambda *_: zeros, pipeline_mode=pl.Buffered(1))


def _adaln_kernel(c_ref, w_ref, b_ref, o_ref):
    a_hi, a_mid, _ = _split3(_silu(c_ref[...]))
    w_hi, w_mid, _ = _split3(w_ref[...])
    acc = _dot(a_hi, w_hi) + _dot(a_mid, w_hi) + _dot(a_hi, w_mid)
    o_ref[...] = acc + b_ref[...]


def _adaln(c, w_ada, b_ada):
    bsz = c.shape[0]
    n = w_ada.shape[1]
    c_pad = jnp.pad(c, ((0, SUBLANES - bsz), (0, 0)))
    mod = pl.pallas_call(
        _adaln_kernel,
        grid=(n // ADA_COLS,),
        in_specs=[
            pl.BlockSpec((SUBLANES, D_MODEL), lambda j: (0, 0)),
            pl.BlockSpec((D_MODEL, ADA_COLS), lambda j: (0, j)),
            pl.BlockSpec((1, ADA_COLS), lambda j: (0, j)),
        ],
        out_specs=pl.BlockSpec((SUBLANES, ADA_COLS), lambda j: (0, j)),
        out_shape=jax.ShapeDtypeStruct((SUBLANES, n), F32),
        compiler_params=pltpu.CompilerParams(
            dimension_semantics=("arbitrary",), vmem_limit_bytes=VMEM_LIMIT_BYTES),
        name="adaln",
    )(c_pad, w_ada, b_ada)
    return mod[:bsz]


def _ring_tiles(x_hbm, ring, sem):
    step, n_steps = pl.program_id(0), pl.num_programs(0)
    n_slots, rows = ring.shape[0], ring.shape[1]

    def copy(t):
        row0 = pl.multiple_of(t * rows, rows)
        return pltpu.make_async_copy(x_hbm.at[pl.ds(row0, rows), :], ring.at[t % n_slots], sem.at[t % n_slots])

    @pl.when(step == 0)
    def _():
        copy(0).start()
        copy(1).start()
        copy(0).wait()
        copy(1).wait()

    @pl.when(jnp.logical_and(step > 0, step + 1 < n_steps))
    def _():
        copy(step + 1).wait()

    @pl.when(step + 2 < n_steps)
    def _():
        copy(step + 2).start()

    return ring.at[step % n_slots], ring.at[(step + 1) % n_slots]


class _BlockStream:
    def __init__(self, stage_ref, sem):
        self.stage, self.sem, self.blocks = stage_ref, sem, []
        self.n_slots = stage_ref.shape[0]
        self.done = 0

    def add(self, src, sink):
        self.blocks.append((src, sink))

    def _slot(self, i):
        return self.stage.at[i % self.n_slots, pl.ds(0, self.blocks[i][0].shape[0])]

    def _copy(self, i):
        return pltpu.make_async_copy(self.blocks[i][0], self._slot(i), self.sem.at[i % self.n_slots])

    def prime(self):
        for i in range(min(self.n_slots, len(self.blocks))):
            self._copy(i).start()

    def advance(self, n):
        for i in range(self.done, self.done + n):
            self._copy(i).wait()
            self.blocks[i][1](self._slot(i)[...])
            if i + self.n_slots < len(self.blocks):
                self._copy(i + self.n_slots).start()
        self.done += n

    def drain(self):
        self.advance(len(self.blocks) - self.done)


def _stream_rows_bf16(stream, w_hbm, layer, dst_ref):
    rows = stream.stage.shape[1]

    def sink(r):
        def run(block):
            dst_ref[r:r + rows, :] = block.astype(BF16)
        return run

    for r in range(0, dst_ref.shape[0], rows):
        stream.add(w_hbm.at[layer, pl.ds(r, rows), :], sink(r))


def _stream_cols_bf16(stream, w_hbm, layer, dst_ref, c0):
    cols = stream.stage.shape[2]

    def run(block):
        dst_ref[:, c0:c0 + cols] = block.astype(BF16)

    stream.add(w_hbm.at[layer, :, pl.ds(c0, cols)], run)


def _ffn_kernel(x_hbm, modc_ref, modn_ref, nw_ref, wg_hbm, wu_hbm, wd_hbm, fw_ref, o_ref,
                h_s, wg_ref, wu_ref, wd_ref, stage_in, stage_out, sem_in, sem_out, x_ring, x_sem,
                *, layer, mod_idx, final):
    def in_norm(x, mod_ref):
        sh = mod_ref[0, pl.ds(mod_idx, 1), :]
        sc = mod_ref[0, pl.ds(mod_idx + 1, 1), :]
        return (_rms(x, nw_ref[...]) * (1.0 + sc) + sh).astype(BF16)

    step = pl.program_id(0)
    xc_ref, xn_ref = _ring_tiles(x_hbm, x_ring, x_sem)
    cur = h_s.at[step % 2]
    nxt = h_s.at[(step + 1) % 2]

    n_chunks = D_FF // MXU_COLS
    slice_rows = FFN_ROWS // NORM_SLICES
    sub_rows = FFN_ROWS // FFN_SUBTILES
    slices_per_sub = NORM_SLICES // FFN_SUBTILES

    def body(first):
        if first:
            s_in, s_out = _BlockStream(stage_in, sem_in), _BlockStream(stage_out, sem_out)
            for k in range(n_chunks):
                _stream_cols_bf16(s_in, wg_hbm, layer, wg_ref, k * MXU_COLS)
                _stream_cols_bf16(s_in, wu_hbm, layer, wu_ref, k * MXU_COLS)
            _stream_rows_bf16(s_out, wd_hbm, layer, wd_ref)
            s_in.prime()
            s_out.prime()
            cur[...] = in_norm(xc_ref[...], modc_ref)

        gt = modc_ref[0, pl.ds(mod_idx + 2, 1), :]
        for sub in range(FFN_SUBTILES):
            rows = slice(sub * sub_rows, (sub + 1) * sub_rows)
            acts = []
            for k in range(n_chunks):
                if first and sub == 0:
                    s_in.advance(2)
                if k < slices_per_sub:
                    j = sub * slices_per_sub + k
                    rs = slice(j * slice_rows, (j + 1) * slice_rows)
                    nxt[rs, :] = in_norm(xn_ref[rs, :], modn_ref)
                cols = slice(k * MXU_COLS, (k + 1) * MXU_COLS)
                h = cur[rows, :]
                g = _dot(h, wg_ref[:, cols])
                u = _dot(h, wu_ref[:, cols])
                acts.append((_silu(g) * u).astype(BF16))
            if first and sub == 0:
                s_out.drain()
            a = jnp.concatenate(acts, axis=1)
            o = _dot(a, wd_ref[...])
            out = xc_ref[rows, :] + (FFN_RES * gt) * o
            if final:
                out = _rms(out, fw_ref[...])
            o_ref[rows, :] = out

    @pl.when(step == 0)
    def _():
        body(True)

    @pl.when(step > 0)
    def _():
        body(False)


def _ffn(x, mod3, norm_w, wg, wu, wd, final_w, *, layer, mod_idx, final):
    bsz, seq, _ = x.shape
    tiles_per_seq = seq // FFN_ROWS
    steps = bsz * tiles_per_seq
    nxt = lambda s: jnp.minimum(s + 1, steps - 1)
    x2 = x.reshape(bsz * seq, D_MODEL)
    hbm = pl.BlockSpec(memory_space=pl.ANY)
    out = pl.pallas_call(
        functools.partial(_ffn_kernel, layer=layer, mod_idx=mod_idx, final=final),
        grid=(steps,),
        in_specs=[
            hbm,
            pl.BlockSpec((1, N_MOD, D_MODEL), lambda s: (s // tiles_per_seq, 0, 0)),
            pl.BlockSpec((1, N_MOD, D_MODEL), lambda s: (nxt(s) // tiles_per_seq, 0, 0)),
            _const_spec((1, D_MODEL)),
            hbm, hbm, hbm,
            _const_spec((1, D_MODEL)),
        ],
        out_specs=pl.BlockSpec((FFN_ROWS, D_MODEL), lambda s: (s, 0)),
        out_shape=jax.ShapeDtypeStruct((bsz * seq, D_MODEL), F32),
        scratch_shapes=[
            pltpu.VMEM((2, FFN_ROWS, D_MODEL), BF16),
            pltpu.VMEM((D_MODEL, D_FF), BF16),
            pltpu.VMEM((D_MODEL, D_FF), BF16),
            pltpu.VMEM((D_FF, D_MODEL), BF16),
            pltpu.VMEM((FFN_W_SLOTS, D_MODEL, MXU_COLS), F32),
            pltpu.VMEM((FFN_W_SLOTS, FFN_W_ROWS_OUT, D_MODEL), F32),
            pltpu.SemaphoreType.DMA((FFN_W_SLOTS,)),
            pltpu.SemaphoreType.DMA((FFN_W_SLOTS,)),
            pltpu.VMEM((X_SLOTS, FFN_ROWS, D_MODEL), F32),
            pltpu.SemaphoreType.DMA((X_SLOTS,)),
        ],
        compiler_params=pltpu.CompilerParams(
            dimension_semantics=("arbitrary",), vmem_limit_bytes=VMEM_LIMIT_BYTES),
        name="ffn_final" if final else "ffn",
    )(x2, mod3, mod3, norm_w, wg, wu, wd, final_w)
    return out.reshape(bsz, seq, D_MODEL)


def _mixer_kernel(x_hbm, modc_ref, modn_ref, nw_ref, wint_hbm,
                  convw_ref, convb_ref, dtb_ref, alog_ref, dskip_ref, ssdnw_ref,
                  poolw_in, poolb_ref, pools_ref, wout_hbm,
                  o_ref,
                  wz_ref, wxbc_ref, wdt_ref, wu_ref, poolw_ref, wout_ref, stage, sem,
                  z_s, raw_s, dtr_s, u_s, xbc_s, st_s, diff_s, zs_s, dtc_s, x_ring, x_sem,
                  *, layer, tiles_per_seq):
    step = pl.program_id(0)
    tile_in_seq = step % tiles_per_seq
    next_starts_seq = tile_in_seq == tiles_per_seq - 1
    n_chunks = MIX_ROWS // CHUNK
    xc_ref, xn_ref = _ring_tiles(x_hbm, x_ring, x_sem)

    def in_norm(x, mod_ref):
        sh = mod_ref[0, 3:4, :]
        sc = mod_ref[0, 4:5, :]
        return (_rms(x, nw_ref[...]) * (1.0 + sc) + sh).astype(BF16)

    def rows(c):
        return slice(c * CHUNK, (c + 1) * CHUNK)

    def conv_base(c):
        return (c % n_chunks) * (CONV_PAD + CHUNK) + CONV_PAD

    def pool_base(c):
        return (c % n_chunks) * (POOL_PAD + CHUNK) + POOL_PAD

    pending = []

    def emit(k=None):
        n = len(pending) if k is None else min(k, len(pending))
        for _ in range(n):
            pending.pop(0)()

    def proj_pieces(c, h_n):
        rs = rows(c)

        def dense(dst, w_ref, lo, hi):
            def run():
                dst[rs, lo:hi] = _dot(h_n, w_ref[:, lo:hi])
            return run

        def slabs(dst, base, w_ref, lo, hi):
            def run():
                res = _dot(h_n, w_ref[:, lo:hi])
                for j in range(lo // LANES, hi // LANES):
                    dst[j, base:base + CHUNK, :] = res[:, j * LANES - lo:(j + 1) * LANES - lo]
            return run

        out = [slabs(raw_s, conv_base(c), wxbc_ref, k, k + MXU_COLS) for k in range(0, D_XBC, MXU_COLS)]
        out += [slabs(u_s, pool_base(c), wu_ref, k, k + MXU_COLS) for k in range(0, D_POOL, MXU_COLS)]
        out += [dense(z_s, wz_ref, k, k + MXU_COLS) for k in range(0, D_SSD, MXU_COLS)]
        out.append(dense(dtr_s, wdt_ref, 0, LANES))
        return out

    @pl.when(step == 0)
    def _():
        stream = _BlockStream(stage, sem)
        w_rows, w_cols = stage.shape[1], stage.shape[2]

        def cast_into(dst, r, c0):
            def run(block):
                dst[r:r + w_rows, c0:c0 + w_cols] = block.astype(BF16)
            return run

        def transposed_into(dst, c0):
            def run(block):
                dst[:, c0:c0 + block.shape[0]] = block.T.astype(BF16)
            return run

        def dt_into(block):
            lane = lax.broadcasted_iota(jnp.int32, (D_MODEL, LANES), 1)
            wdt_ref[...] = jnp.where(lane < N_HEADS, block.T, 0.0).astype(BF16)

        def feature_rows(f0):
            return lambda r, n: wint_hbm.at[layer, pl.ds(f0 + r, n), :]

        for src, dst, width in ((feature_rows(0), wz_ref, D_SSD), (feature_rows(D_SSD), wxbc_ref, D_XBC),
                                (feature_rows(DT_COL + N_HEADS), wu_ref, D_POOL)):
            for r in range(0, width, w_rows):
                stream.add(src(r, w_rows), transposed_into(dst, r))
        stream.add(feature_rows(DT_COL)(0, LANES), dt_into)
        for r in range(0, D_SSD + D_POOL, w_rows):
            stream.add(wout_hbm.at[layer, pl.ds(r, w_rows), :], cast_into(wout_ref, r, 0))
        stream.prime()
        poolw_ref[...] = poolw_in[...]
        stream.drain()
        raw_s[:, 0:CONV_PAD, :] = jnp.zeros((D_XBC // LANES, CONV_PAD, LANES), F32)
        u_s[:, 0:POOL_PAD, :] = jnp.zeros((D_POOL // LANES, POOL_PAD, LANES), F32)
        for c in range(n_chunks):
            for run in proj_pieces(c, in_norm(xc_ref[rows(c), :], modc_ref)):
                run()

    @pl.when(tile_in_seq == 0)
    def _():
        st_s[...] = jnp.zeros(st_s.shape, F32)

    gt = modc_ref[0, 5:6, :]
    a_neg = -jnp.exp(alog_ref[...])
    ri = lax.broadcasted_iota(jnp.int32, (CHUNK, CHUNK), 0)
    ci = lax.broadcasted_iota(jnp.int32, (CHUNK, CHUNK), 1)
    causal = ri >= ci
    tril = jnp.where(causal, 1.0, 0.0).astype(BF16)
    low_half = ci < HEAD_DIM
    low_half_row = lax.broadcasted_iota(jnp.int32, (1, LANES), 1) < HEAD_DIM

    def pre(c, seq_tile):
        rs = rows(c)
        last = c == n_chunks - 1

        def carry(hist):
            return jnp.where(next_starts_seq, jnp.zeros_like(hist), hist) if last else hist

        base, nbase = conv_base(c), conv_base(c + 1)
        for j in range(D_XBC // LANES):
            cols = slice(j * LANES, (j + 1) * LANES)
            xc = convb_ref[:, cols]
            for k in range(D_CONV):
                xc = xc + convw_ref[k:k + 1, cols] * raw_s[j, pl.ds(base - (D_CONV - 1) + k, CHUNK), :]
            raw_s[j, nbase - CONV_PAD:nbase, :] = carry(raw_s[j, base + CHUNK - CONV_PAD:base + CHUNK, :])
            xbc_s[rs, cols] = _silu(xc)
            if j % 2 == 1:
                emit(1)
        base, nbase = pool_base(c), pool_base(c + 1)
        diffs = []
        for gi, w in enumerate(POOL_WINDOWS):
            halves = []
            for j in range(gi * POOL_GROUP_DIM // LANES, (gi + 1) * POOL_GROUP_DIM // LANES):
                u_g = u_s[j, base:base + CHUNK, :]
                s = u_g
                for k in range(1, w):
                    s = s + u_s[j, pl.ds(base - k, CHUNK), :]
                if c == 0:
                    pos = seq_tile * MIX_ROWS + 1 + lax.broadcasted_iota(jnp.int32, (CHUNK, LANES), 0)
                    pooled = s / jnp.minimum(pos, w).astype(F32)
                else:
                    pooled = s * (1.0 / w)
                halves.append(pooled - u_g)
                u_s[j, nbase - POOL_PAD:nbase, :] = carry(u_s[j, base + CHUNK - POOL_PAD:base + CHUNK, :])
            diffs.append(jnp.concatenate(halves, axis=1).astype(BF16))
            emit(1)
        zs = _silu(z_s[rs, :])
        dt_c = jax.nn.softplus(dtr_s[rs, :] + dtb_ref[...])
        emit()
        return diffs, zs, dt_c

    def ssd(c, zs, dt_c):
        rs = rows(c)
        p_hi, p_mid, p_lo = _split3(dt_c * a_neg)
        cs = _dot(tril, p_hi) + _dot(tril, p_mid) + _dot(tril, p_lo)
        cs_t = cs.T
        dt_t = dt_c.T
        w_t = dt_t * jnp.exp(cs_t[:, CHUNK - 1:CHUNK] - cs_t)
        xs_c = xbc_s[rs, 0:D_SSD]
        emit(1)
        y_pairs = []
        for g in range(N_GROUPS):
            b0 = D_SSD + g * D_STATE
            c0 = D_SSD + N_GROUPS * D_STATE + g * D_STATE
            b_t = xbc_s[rs, b0:b0 + D_STATE].T
            c_g = xbc_s[rs, c0:c0 + D_STATE]
            cb = _dot(c_g.astype(BF16), b_t.astype(BF16))
            for pr in range(HEADS_PER_GROUP // 2):
                lhs, bw_t, cd = [], [], []
                for r in (2 * pr, 2 * pr + 1):
                    hd = g * HEADS_PER_GROUP + r
                    cs_bc = jnp.broadcast_to(cs[:, hd:hd + 1], (CHUNK, CHUNK))
                    decay = jnp.exp(jnp.where(causal, cs_bc - cs_t[hd:hd + 1, :], -jnp.inf))
                    scores = decay * (cb * dt_t[hd:hd + 1, :])
                    sd = jnp.exp(cs_bc)
                    lhs.append(jnp.concatenate([scores, c_g * sd], axis=1).astype(BF16))
                    bw_t.append((b_t * w_t[hd:hd + 1, :]).astype(BF16))
                    cd.append(sd[CHUNK - 1:CHUNK, :])
                col = (g * HEADS_PER_GROUP + 2 * pr) * HEAD_DIM
                x_p = xs_c[:, col:col + LANES]
                st = st_s[:, col:col + LANES]
                rhs = jnp.concatenate([x_p, st], axis=0).astype(BF16)
                y_pairs.append(jnp.where(low_half, _dot(lhs[0], rhs), _dot(lhs[1], rhs)))
                zero = jnp.zeros_like(x_p)
                x_split = jnp.concatenate(
                    [jnp.where(low_half, x_p, zero), jnp.where(low_half, zero, x_p)], axis=0)
                upd = _dot(jnp.concatenate(bw_t, axis=1), x_split.astype(BF16))
                st_s[:, col:col + LANES] = st * jnp.where(low_half_row, cd[0], cd[1]) + upd
                emit(1)
        y = jnp.concatenate(y_pairs, axis=1) + dskip_ref[...] * xs_c
        yz = y * zs
        gw = D_SSD // N_GROUPS
        parts = []
        for g in range(N_GROUPS):
            blk = yz[:, g * gw:(g + 1) * gw]
            ms = jnp.mean(blk * blk, axis=-1, keepdims=True)
            parts.append(blk * lax.rsqrt(ms + EPS))
        return (jnp.concatenate(parts, axis=1) * ssdnw_ref[...]).astype(BF16)

    def post_pieces(c, y_ssd, diffs):
        rs = rows(c)
        outs = [None] * len(POOL_WINDOWS)

        def pool_piece(gi):
            def run():
                lo, hi = gi * POOL_GROUP_DIM, (gi + 1) * POOL_GROUP_DIM
                o = (_dot(diffs[gi], poolw_ref[gi]) + poolb_ref[:, lo:hi]) * pools_ref[:, lo:hi]
                outs[gi] = o.astype(BF16)
            return run

        def out_piece(lo, hi):
            def run():
                y_pool = jnp.concatenate(outs, axis=1)
                out = (_dot(y_ssd, wout_ref[0:D_SSD, lo:hi])
                       + _dot(y_pool, wout_ref[D_SSD:D_SSD + D_POOL, lo:hi]))
                o_ref[rs, lo:hi] = xc_ref[rs, lo:hi] + gt[:, lo:hi] * out
            return run

        return ([pool_piece(gi) for gi in range(len(POOL_WINDOWS))]
                + [out_piece(k, k + MXU_COLS) for k in range(0, D_MODEL, MXU_COLS)])

    def hand_over(vals):
        diffs, zs, dt_c = vals
        for gi in range(len(POOL_WINDOWS)):
            diff_s[:, gi * POOL_GROUP_DIM:(gi + 1) * POOL_GROUP_DIM] = diffs[gi]
        zs_s[...] = zs
        dtc_s[...] = dt_c

    @pl.when(step == 0)
    def _():
        hand_over(pre(0, tile_in_seq))

    diffs = [diff_s[:, gi * POOL_GROUP_DIM:(gi + 1) * POOL_GROUP_DIM] for gi in range(len(POOL_WINDOWS))]
    zs, dt_c = zs_s[...], dtc_s[...]
    h_n = in_norm(xn_ref[rows(0), :], modn_ref)
    pending.extend(proj_pieces(0, h_n))
    for c in range(n_chunks):
        y_ssd = ssd(c, zs, dt_c)
        pending.extend(post_pieces(c, y_ssd, diffs))
        if c + 1 < n_chunks:
            h_n = in_norm(xn_ref[rows(c + 1), :], modn_ref)
            emit(2)
            diffs, zs, dt_c = pre(c + 1, tile_in_seq)
            pending.extend(proj_pieces(c + 1, h_n))
        else:
            hand_over(pre(0, (step + 1) % tiles_per_seq))
            emit()


def _mixer(x, mod3, norm_w, w_in_t, convw, convb, dtb, alog, dskip, ssdnw,
           poolw, poolb, pools, w_out, *, layer):
    bsz, seq, _ = x.shape
    tiles_per_seq = seq // MIX_ROWS
    n_chunks = MIX_ROWS // CHUNK
    steps = bsz * tiles_per_seq
    assert steps >= X_SLOTS
    nxt = lambda s: jnp.minimum(s + 1, steps - 1)
    hbm = pl.BlockSpec(memory_space=pl.ANY)
    vmem_consts = [convw, convb, dtb, alog, dskip, ssdnw, poolw, poolb, pools]
    x2 = x.reshape(bsz * seq, D_MODEL)
    out = pl.pallas_call(
        functools.partial(_mixer_kernel, layer=layer, tiles_per_seq=tiles_per_seq),
        grid=(steps,),
        in_specs=[
            hbm,
            pl.BlockSpec((1, N_MOD, D_MODEL), lambda s: (s // tiles_per_seq, 0, 0)),
            pl.BlockSpec((1, N_MOD, D_MODEL), lambda s: (nxt(s) // tiles_per_seq, 0, 0)),
            _const_spec(norm_w.shape), hbm,
        ] + [_const_spec(a.shape) for a in vmem_consts] + [hbm],
        out_specs=pl.BlockSpec((MIX_ROWS, D_MODEL), lambda s: (s, 0)),
        out_shape=jax.ShapeDtypeStruct((bsz * seq, D_MODEL), F32),
        scratch_shapes=[
            pltpu.VMEM((D_MODEL, D_SSD), BF16),
            pltpu.VMEM((D_MODEL, D_XBC), BF16),
            pltpu.VMEM((D_MODEL, LANES), BF16),
            pltpu.VMEM((D_MODEL, D_POOL), BF16),
            pltpu.VMEM(poolw.shape, BF16),
            pltpu.VMEM((D_SSD + D_POOL, D_MODEL), BF16),
            pltpu.VMEM((W_SLOTS, MIX_W_ROWS, D_MODEL), F32),
            pltpu.SemaphoreType.DMA((W_SLOTS,)),
            pltpu.VMEM((MIX_ROWS, D_SSD), F32),
            pltpu.VMEM((D_XBC // LANES, n_chunks * (CONV_PAD + CHUNK), LANES), F32),
            pltpu.VMEM((MIX_ROWS, LANES), F32),
            pltpu.VMEM((D_POOL // LANES, n_chunks * (POOL_PAD + CHUNK), LANES), F32),
            pltpu.VMEM((MIX_ROWS, D_XBC), F32),
            pltpu.VMEM((D_STATE, D_SSD), F32),
            pltpu.VMEM((CHUNK, D_POOL), BF16),
            pltpu.VMEM((CHUNK, D_SSD), F32),
            pltpu.VMEM((CHUNK, LANES), F32),
            pltpu.VMEM((X_SLOTS, MIX_ROWS, D_MODEL), F32),
            pltpu.SemaphoreType.DMA((X_SLOTS,)),
        ],
        compiler_params=pltpu.CompilerParams(
            dimension_semantics=("arbitrary",), vmem_limit_bytes=VMEM_LIMIT_BYTES),
        name="mixer",
    )(x2, mod3, mod3, norm_w, w_in_t, *vmem_consts, w_out)
    return out.reshape(bsz, seq, D_MODEL)


def _pad_lanes(v):
    return jnp.pad(v.reshape(1, -1), ((0, 0), (0, LANES - v.shape[0])))


def kernel(x, c, w_ada, b_ada, ffn1_norm, ffn1_w_gate, ffn1_w_up, ffn1_w_down, mix_norm, w_in, conv_w, conv_b, dt_bias, a_log, d_skip, ssd_norm_w, pool_w, pool_b, pool_scale, w_out, ffn2_norm, ffn2_w_gate, ffn2_w_up, ffn2_w_down, final_norm):
    bsz = x.shape[0]
    depth = w_ada.shape[0]
    row = lambda v: v.reshape(1, -1)
    for i in range(depth):
        mod3 = _adaln(c, w_ada[i], row(b_ada[i])).reshape(bsz, N_MOD, D_MODEL)
        last = i == depth - 1

        x = _ffn(x, mod3, row(ffn1_norm[i]), ffn1_w_gate, ffn1_w_up, ffn1_w_down, row(final_norm),
                 layer=i, mod_idx=0, final=False)

        x = _mixer(
            x, mod3, row(mix_norm[i]), jnp.swapaxes(w_in, 1, 2),
            conv_w[i], row(conv_b[i]), _pad_lanes(dt_bias[i]), _pad_lanes(a_log[i]),
            row(jnp.repeat(d_skip[i], HEAD_DIM)), row(ssd_norm_w[i]),
            pool_w[i].astype(BF16), row(pool_b[i]), row(pool_scale[i]), w_out, layer=i)

        x = _ffn(x, mod3, row(ffn2_norm[i]), ffn2_w_gate, ffn2_w_up, ffn2_w_down, row(final_norm),
                 layer=i, mod_idx=6, final=last)
    return x
```

```python
import functools

import jax
import jax.numpy as jnp
from jax import lax
from jax.experimental import pallas as pl
from jax.experimental.pallas import tpu as pltpu

F32 = jnp.float32
BF16 = jnp.bfloat16

D_MODEL = 1024
D_FF = 2816
N_MOD = 9
FFN_RES = 0.5
EPS = 1e-6
D_SSD = 1024
HEAD_DIM = 64
N_HEADS = 16
N_GROUPS = 4
HEADS_PER_GROUP = N_HEADS // N_GROUPS
D_STATE = 128
D_CONV = 4
CHUNK = 128
D_POOL = 1024
POOL_WINDOWS = (2, 4, 8, 16)
POOL_GROUP_DIM = 256
D_XBC = D_SSD + 2 * N_GROUPS * D_STATE
DT_COL = D_SSD + D_XBC

LANES = 128
SUBLANES = 8
MXU_COLS = 256
VMEM_LIMIT_BYTES = 60 * 1024 * 1024

FFN_ROWS = 512
NORM_SLICES = 8
FFN_W_ROWS_IN = 128
FFN_W_ROWS_OUT = 256
MIX_W_ROWS = 256
W_SLOTS = 8
MIX_ROWS = 512
ADA_COLS = 1024
CONV_PAD = SUBLANES
POOL_PAD = 16

assert HEAD_DIM * 2 == LANES and D_STATE == LANES and CHUNK == LANES


def _dot(a, b):
    return jnp.dot(a, b, preferred_element_type=F32)


def _split3(v):
    hi = v.astype(BF16)
    r1 = v - hi.astype(F32)
    mid = r1.astype(BF16)
    lo = (r1 - mid.astype(F32)).astype(BF16)
    return hi, mid, lo


def _silu(v):
    return v * jax.nn.sigmoid(v)


def _rms(v, w):
    ms = jnp.mean(v * v, axis=-1, keepdims=True)
    return v * lax.rsqrt(ms + EPS) * w


def _const_spec(shape):
    zeros = (0,) * len(shape)
    return pl.BlockSpec(shape, lambda *_: zeros, pipeline_mode=pl.Buffered(1))


def _adaln_kernel(c_ref, w_ref, b_ref, o_ref):
    a_hi, a_mid, _ = _split3(_silu(c_ref[...]))
    res = _dot(jnp.concatenate([a_hi, a_mid], axis=0), w_ref[...].astype(BF16))
    o_ref[...] = res[0:SUBLANES] + res[SUBLANES:2 * SUBLANES] + b_ref[...]


def _adaln(c, w_ada, b_ada):
    bsz = c.shape[0]
    n = w_ada.shape[1]
    c_pad = jnp.pad(c, ((0, SUBLANES - bsz), (0, 0)))
    mod = pl.pallas_call(
        _adaln_kernel,
        grid=(n // ADA_COLS,),
        in_specs=[
            pl.BlockSpec((SUBLANES, D_MODEL), lambda j: (0, 0)),
            pl.BlockSpec((D_MODEL, ADA_COLS), lambda j: (0, j)),
            pl.BlockSpec((1, ADA_COLS), lambda j: (0, j)),
        ],
        out_specs=pl.BlockSpec((SUBLANES, ADA_COLS), lambda j: (0, j)),
        out_shape=jax.ShapeDtypeStruct((SUBLANES, n), F32),
        compiler_params=pltpu.CompilerParams(
            dimension_semantics=("arbitrary",), vmem_limit_bytes=VMEM_LIMIT_BYTES),
        name="adaln",
    )(c_pad, w_ada, b_ada)
    return mod[:bsz]


class _BlockStream:
    def __init__(self, stage_ref, sem):
        self.stage, self.sem, self.blocks = stage_ref, sem, []
        self.n_slots = stage_ref.shape[0]

    def add(self, src, sink):
        self.blocks.append((src, sink))

    def _slot(self, i):
        return self.stage.at[i % self.n_slots, pl.ds(0, self.blocks[i][0].shape[0])]

    def _copy(self, i):
        return pltpu.make_async_copy(self.blocks[i][0], self._slot(i), self.sem.at[i % self.n_slots])

    def prime(self):
        for i in range(min(self.n_slots, len(self.blocks))):
            self._copy(i).start()

    def drain(self):
        for i in range(len(self.blocks)):
            self._copy(i).wait()
            self.blocks[i][1](self._slot(i)[...])
            if i + self.n_slots < len(self.blocks):
                self._copy(i + self.n_slots).start()


def _stream_rows_bf16(stream, w_hbm, layer, dst_ref):
    rows = stream.stage.shape[1]

    def sink(r):
        def run(block):
            dst_ref[r:r + rows, :] = block.astype(BF16)
        return run

    for r in range(0, dst_ref.shape[0], rows):
        stream.add(w_hbm.at[layer, pl.ds(r, rows), :], sink(r))


def _ffn_kernel(xc_ref, xn_ref, modc_ref, modn_ref, nw_ref, wg_hbm, wu_hbm, wd_hbm, fw_ref, o_ref,
                h_s, wg_ref, wu_ref, wd_ref, stage_in, stage_out, sem_in, sem_out,
                *, layer, mod_idx, final):
    def in_norm(x, mod_ref):
        sh = mod_ref[0, pl.ds(mod_idx, 1), :]
        sc = mod_ref[0, pl.ds(mod_idx + 1, 1), :]
        return (_rms(x, nw_ref[...]) * (1.0 + sc) + sh).astype(BF16)

    step = pl.program_id(0)
    cur = h_s.at[step % 2]
    nxt = h_s.at[(step + 1) % 2]

    @pl.when(step == 0)
    def _():
        s_in, s_out = _BlockStream(stage_in, sem_in), _BlockStream(stage_out, sem_out)
        _stream_rows_bf16(s_in, wg_hbm, layer, wg_ref)
        _stream_rows_bf16(s_in, wu_hbm, layer, wu_ref)
        _stream_rows_bf16(s_out, wd_hbm, layer, wd_ref)
        s_in.prime()
        s_out.prime()
        s_in.drain()
        s_out.drain()
        cur[...] = in_norm(xc_ref[...], modc_ref)

    n_chunks = D_FF // MXU_COLS
    slice_rows = FFN_ROWS // NORM_SLICES
    acts = []
    for k in range(n_chunks):
        if k < NORM_SLICES:
            rs = slice(k * slice_rows, (k + 1) * slice_rows)
            nxt[rs, :] = in_norm(xn_ref[rs, :], modn_ref)
        cols = slice(k * MXU_COLS, (k + 1) * MXU_COLS)
        h = cur[...]
        g = _dot(h, wg_ref[:, cols])
        u = _dot(h, wu_ref[:, cols])
        acts.append((_silu(g) * u).astype(BF16))
    a = jnp.concatenate(acts, axis=1)
    o = _dot(a, wd_ref[...])
    gt = modc_ref[0, pl.ds(mod_idx + 2, 1), :]
    out = xc_ref[...] + (FFN_RES * gt) * o
    if final:
        out = _rms(out, fw_ref[...])
    o_ref[...] = out


def _ffn(x, mod3, norm_w, wg, wu, wd, final_w, *, layer, mod_idx, final):
    bsz, seq, _ = x.shape
    tiles_per_seq = seq // FFN_ROWS
    steps = bsz * tiles_per_seq
    nxt = lambda s: jnp.minimum(s + 1, steps - 1)
    x2 = x.reshape(bsz * seq, D_MODEL)
    hbm = pl.BlockSpec(memory_space=pl.ANY)
    out = pl.pallas_call(
        functools.partial(_ffn_kernel, layer=layer, mod_idx=mod_idx, final=final),
        grid=(steps,),
        in_specs=[
            pl.BlockSpec((FFN_ROWS, D_MODEL), lambda s: (s, 0)),
            pl.BlockSpec((FFN_ROWS, D_MODEL), lambda s: (nxt(s), 0)),
            pl.BlockSpec((1, N_MOD, D_MODEL), lambda s: (s // tiles_per_seq, 0, 0)),
            pl.BlockSpec((1, N_MOD, D_MODEL), lambda s: (nxt(s) // tiles_per_seq, 0, 0)),
            _const_spec((1, D_MODEL)),
            hbm, hbm, hbm,
            _const_spec((1, D_MODEL)),
        ],
        out_specs=pl.BlockSpec((FFN_ROWS, D_MODEL), lambda s: (s, 0)),
        out_shape=jax.ShapeDtypeStruct((bsz * seq, D_MODEL), F32),
        scratch_shapes=[
            pltpu.VMEM((2, FFN_ROWS, D_MODEL), BF16),
            pltpu.VMEM((D_MODEL, D_FF), BF16),
            pltpu.VMEM((D_MODEL, D_FF), BF16),
            pltpu.VMEM((D_FF, D_MODEL), BF16),
            pltpu.VMEM((W_SLOTS, FFN_W_ROWS_IN, D_FF), F32),
            pltpu.VMEM((W_SLOTS, FFN_W_ROWS_OUT, D_MODEL), F32),
            pltpu.SemaphoreType.DMA((W_SLOTS,)),
            pltpu.SemaphoreType.DMA((W_SLOTS,)),
        ],
        compiler_params=pltpu.CompilerParams(
            dimension_semantics=("arbitrary",), vmem_limit_bytes=VMEM_LIMIT_BYTES),
        name="ffn_final" if final else "ffn",
    )(x2, x2, mod3, mod3, norm_w, wg, wu, wd, final_w)
    return out.reshape(bsz, seq, D_MODEL)


def _mixer_kernel(xc_ref, xn_ref, modc_ref, modn_ref, nw_ref, wint_hbm,
                  convw_ref, convb_ref, dtb_ref, alog_ref, dskip_ref, ssdnw_ref,
                  poolw_in, poolb_ref, pools_ref, wout_hbm,
                  o_ref,
                  wz_ref, wxbc_ref, wdt_ref, wu_ref, poolw_ref, wout_ref, stage, sem,
                  z_s, raw_s, dtr_s, u_s, xbc_s, st_s, diff_s, zs_s, dtc_s, *, layer, tiles_per_seq):
    step = pl.program_id(0)
    tile_in_seq = step % tiles_per_seq
    next_starts_seq = tile_in_seq == tiles_per_seq - 1
    n_chunks = MIX_ROWS // CHUNK

    def in_norm(x, mod_ref):
        sh = mod_ref[0, 3:4, :]
        sc = mod_ref[0, 4:5, :]
        return (_rms(x, nw_ref[...]) * (1.0 + sc) + sh).astype(BF16)

    def rows(c):
        return slice(c * CHUNK, (c + 1) * CHUNK)

    def conv_base(c):
        return (c % n_chunks) * (CONV_PAD + CHUNK) + CONV_PAD

    def pool_base(c):
        return (c % n_chunks) * (POOL_PAD + CHUNK) + POOL_PAD

    pending = []

    def emit(k=None):
        n = len(pending) if k is None else min(k, len(pending))
        for _ in range(n):
            pending.pop(0)()

    def proj_pieces(c, h_n):
        rs = rows(c)

        def dense(dst, w_ref, lo, hi):
            def run():
                dst[rs, lo:hi] = _dot(h_n, w_ref[:, lo:hi])
            return run

        def slabs(dst, base, w_ref, lo, hi):
            def run():
                res = _dot(h_n, w_ref[:, lo:hi])
                for j in range(lo // LANES, hi // LANES):
                    dst[j, base:base + CHUNK, :] = res[:, j * LANES - lo:(j + 1) * LANES - lo]
            return run

        out = [slabs(raw_s, conv_base(c), wxbc_ref, k, k + MXU_COLS) for k in range(0, D_XBC, MXU_COLS)]
        out += [slabs(u_s, pool_base(c), wu_ref, k, k + MXU_COLS) for k in range(0, D_POOL, MXU_COLS)]
        out += [dense(z_s, wz_ref, k, k + MXU_COLS) for k in range(0, D_SSD, MXU_COLS)]
        out.append(dense(dtr_s, wdt_ref, 0, LANES))
        return out

    @pl.when(step == 0)
    def _():
        stream = _BlockStream(stage, sem)
        w_rows, w_cols = stage.shape[1], stage.shape[2]

        def cast_into(dst, r, c0):
            def run(block):
                dst[r:r + w_rows, c0:c0 + w_cols] = block.astype(BF16)
            return run

        def transposed_into(dst, c0):
            def run(block):
                dst[:, c0:c0 + block.shape[0]] = block.T.astype(BF16)
            return run

        def dt_into(block):
            lane = lax.broadcasted_iota(jnp.int32, (D_MODEL, LANES), 1)
            wdt_ref[...] = jnp.where(lane < N_HEADS, block.T, 0.0).astype(BF16)

        def feature_rows(f0):
            return lambda r, n: wint_hbm.at[layer, pl.ds(f0 + r, n), :]

        for src, dst, width in ((feature_rows(0), wz_ref, D_SSD), (feature_rows(D_SSD), wxbc_ref, D_XBC),
                                (feature_rows(DT_COL + N_HEADS), wu_ref, D_POOL)):
            for r in range(0, width, w_rows):
                stream.add(src(r, w_rows), transposed_into(dst, r))
        stream.add(feature_rows(DT_COL)(0, LANES), dt_into)
        for r in range(0, D_SSD + D_POOL, w_rows):
            stream.add(wout_hbm.at[layer, pl.ds(r, w_rows), :], cast_into(wout_ref, r, 0))
        stream.prime()
        poolw_ref[...] = poolw_in[...]
        stream.drain()
        raw_s[:, 0:CONV_PAD, :] = jnp.zeros((D_XBC // LANES, CONV_PAD, LANES), F32)
        u_s[:, 0:POOL_PAD, :] = jnp.zeros((D_POOL // LANES, POOL_PAD, LANES), F32)
        for c in range(n_chunks):
            for run in proj_pieces(c, in_norm(xc_ref[rows(c), :], modc_ref)):
                run()

    @pl.when(tile_in_seq == 0)
    def _():
        st_s[...] = jnp.zeros(st_s.shape, F32)

    gt = modc_ref[0, 5:6, :]
    a_neg = -jnp.exp(alog_ref[...])
    ri = lax.broadcasted_iota(jnp.int32, (CHUNK, CHUNK), 0)
    ci = lax.broadcasted_iota(jnp.int32, (CHUNK, CHUNK), 1)
    causal = ri >= ci
    tril = jnp.where(causal, 1.0, 0.0).astype(BF16)
    low_half = ci < HEAD_DIM
    low_half_row = lax.broadcasted_iota(jnp.int32, (1, LANES), 1) < HEAD_DIM

    def pre(c, seq_tile):
        rs = rows(c)
        last = c == n_chunks - 1

        def carry(hist):
            return jnp.where(next_starts_seq, jnp.zeros_like(hist), hist) if last else hist

        base, nbase = conv_base(c), conv_base(c + 1)
        for j in range(D_XBC // LANES):
            cols = slice(j * LANES, (j + 1) * LANES)
            xc = convb_ref[:, cols]
            for k in range(D_CONV):
                xc = xc + convw_ref[k:k + 1, cols] * raw_s[j, pl.ds(base - (D_CONV - 1) + k, CHUNK), :]
            raw_s[j, nbase - CONV_PAD:nbase, :] = carry(raw_s[j, base + CHUNK - CONV_PAD:base + CHUNK, :])
            xbc_s[rs, cols] = _silu(xc)
            if j % 2 == 1:
                emit(1)
        base, nbase = pool_base(c), pool_base(c + 1)
        diffs = []
        for gi, w in enumerate(POOL_WINDOWS):
            halves = []
            for j in range(gi * POOL_GROUP_DIM // LANES, (gi + 1) * POOL_GROUP_DIM // LANES):
                u_g = u_s[j, base:base + CHUNK, :]
                s = u_g
                for k in range(1, w):
                    s = s + u_s[j, pl.ds(base - k, CHUNK), :]
                if c == 0:
                    pos = seq_tile * MIX_ROWS + 1 + lax.broadcasted_iota(jnp.int32, (CHUNK, LANES), 0)
                    pooled = s / jnp.minimum(pos, w).astype(F32)
                else:
                    pooled = s * (1.0 / w)
                halves.append(pooled - u_g)
                u_s[j, nbase - POOL_PAD:nbase, :] = carry(u_s[j, base + CHUNK - POOL_PAD:base + CHUNK, :])
            diffs.append(jnp.concatenate(halves, axis=1).astype(BF16))
            emit(1)
        zs = _silu(z_s[rs, :])
        dt_c = jax.nn.softplus(dtr_s[rs, :] + dtb_ref[...])
        emit()
        return diffs, zs, dt_c

    def ssd(c, zs, dt_c):
        rs = rows(c)
        p_hi, p_mid, p_lo = _split3(dt_c * a_neg)
        cs = _dot(tril, p_hi) + _dot(tril, p_mid) + _dot(tril, p_lo)
        cs_t = cs.T
        dt_t = dt_c.T
        w_t = dt_t * jnp.exp(cs_t[:, CHUNK - 1:CHUNK] - cs_t)
        xs_c = xbc_s[rs, 0:D_SSD]
        emit(1)
        y_pairs = []
        for g in range(N_GROUPS):
            b0 = D_SSD + g * D_STATE
            c0 = D_SSD + N_GROUPS * D_STATE + g * D_STATE
            b_t = xbc_s[rs, b0:b0 + D_STATE].T
            c_g = xbc_s[rs, c0:c0 + D_STATE]
            cb = _dot(c_g.astype(BF16), b_t.astype(BF16))
            for pr in range(HEADS_PER_GROUP // 2):
                lhs, bw_t, cd = [], [], []
                for r in (2 * pr, 2 * pr + 1):
                    hd = g * HEADS_PER_GROUP + r
                    cs_bc = jnp.broadcast_to(cs[:, hd:hd + 1], (CHUNK, CHUNK))
                    decay = jnp.exp(jnp.where(causal, cs_bc - cs_t[hd:hd + 1, :], -jnp.inf))
                    scores = decay * (cb * dt_t[hd:hd + 1, :])
                    sd = jnp.exp(cs_bc)
                    lhs.append(jnp.concatenate([scores, c_g * sd], axis=1).astype(BF16))
                    bw_t.append((b_t * w_t[hd:hd + 1, :]).astype(BF16))
                    cd.append(sd[CHUNK - 1:CHUNK, :])
                col = (g * HEADS_PER_GROUP + 2 * pr) * HEAD_DIM
                x_p = xs_c[:, col:col + LANES]
                st = st_s[:, col:col + LANES]
                rhs = jnp.concatenate([x_p, st], axis=0).astype(BF16)
                y_pairs.append(jnp.where(low_half, _dot(lhs[0], rhs), _dot(lhs[1], rhs)))
                zero = jnp.zeros_like(x_p)
                x_split = jnp.concatenate(
                    [jnp.where(low_half, x_p, zero), jnp.where(low_half, zero, x_p)], axis=0)
                upd = _dot(jnp.concatenate(bw_t, axis=1), x_split.astype(BF16))
                st_s[:, col:col + LANES] = st * jnp.where(low_half_row, cd[0], cd[1]) + upd
                emit(1)
        y = jnp.concatenate(y_pairs, axis=1) + dskip_ref[...] * xs_c
        yz = y * zs
        gw = D_SSD // N_GROUPS
        parts = []
        for g in range(N_GROUPS):
            blk = yz[:, g * gw:(g + 1) * gw]
            ms = jnp.mean(blk * blk, axis=-1, keepdims=True)
            parts.append(blk * lax.rsqrt(ms + EPS))
        return (jnp.concatenate(parts, axis=1) * ssdnw_ref[...]).astype(BF16)

    def post_pieces(c, y_ssd, diffs):
        rs = rows(c)
        outs = [None] * len(POOL_WINDOWS)

        def pool_piece(gi):
            def run():
                lo, hi = gi * POOL_GROUP_DIM, (gi + 1) * POOL_GROUP_DIM
                o = (_dot(diffs[gi], poolw_ref[gi]) + poolb_ref[:, lo:hi]) * pools_ref[:, lo:hi]
                outs[gi] = o.astype(BF16)
            return run

        def out_piece(lo, hi):
            def run():
                y_pool = jnp.concatenate(outs, axis=1)
                out = (_dot(y_ssd, wout_ref[0:D_SSD, lo:hi])
                       + _dot(y_pool, wout_ref[D_SSD:D_SSD + D_POOL, lo:hi]))
                o_ref[rs, lo:hi] = xc_ref[rs, lo:hi] + gt[:, lo:hi] * out
            return run

        return ([pool_piece(gi) for gi in range(len(POOL_WINDOWS))]
                + [out_piece(k, k + MXU_COLS) for k in range(0, D_MODEL, MXU_COLS)])

    def hand_over(vals):
        diffs, zs, dt_c = vals
        for gi in range(len(POOL_WINDOWS)):
            diff_s[:, gi * POOL_GROUP_DIM:(gi + 1) * POOL_GROUP_DIM] = diffs[gi]
        zs_s[...] = zs
        dtc_s[...] = dt_c

    @pl.when(step == 0)
    def _():
        hand_over(pre(0, tile_in_seq))

    diffs = [diff_s[:, gi * POOL_GROUP_DIM:(gi + 1) * POOL_GROUP_DIM] for gi in range(len(POOL_WINDOWS))]
    zs, dt_c = zs_s[...], dtc_s[...]
    h_n = in_norm(xn_ref[rows(0), :], modn_ref)
    pending.extend(proj_pieces(0, h_n))
    for c in range(n_chunks):
        y_ssd = ssd(c, zs, dt_c)
        pending.extend(post_pieces(c, y_ssd, diffs))
        if c + 1 < n_chunks:
            h_n = in_norm(xn_ref[rows(c + 1), :], modn_ref)
            emit(2)
            diffs, zs, dt_c = pre(c + 1, tile_in_seq)
            pending.extend(proj_pieces(c + 1, h_n))
        else:
            hand_over(pre(0, (step + 1) % tiles_per_seq))
            emit()


def _mixer(x, mod3, norm_w, w_in_t, convw, convb, dtb, alog, dskip, ssdnw,
           poolw, poolb, pools, w_out, *, layer):
    bsz, seq, _ = x.shape
    tiles_per_seq = seq // MIX_ROWS
    n_chunks = MIX_ROWS // CHUNK
    steps = bsz * tiles_per_seq
    nxt = lambda s: jnp.minimum(s + 1, steps - 1)
    hbm = pl.BlockSpec(memory_space=pl.ANY)
    vmem_consts = [convw, convb, dtb, alog, dskip, ssdnw, poolw, poolb, pools]
    x2 = x.reshape(bsz * seq, D_MODEL)
    out = pl.pallas_call(
        functools.partial(_mixer_kernel, layer=layer, tiles_per_seq=tiles_per_seq),
        grid=(steps,),
        in_specs=[
            pl.BlockSpec((MIX_ROWS, D_MODEL), lambda s: (s, 0)),
            pl.BlockSpec((MIX_ROWS, D_MODEL), lambda s: (nxt(s), 0)),
            pl.BlockSpec((1, N_MOD, D_MODEL), lambda s: (s // tiles_per_seq, 0, 0)),
            pl.BlockSpec((1, N_MOD, D_MODEL), lambda s: (nxt(s) // tiles_per_seq, 0, 0)),
            _const_spec(norm_w.shape), hbm,
        ] + [_const_spec(a.shape) for a in vmem_consts] + [hbm],
        out_specs=pl.BlockSpec((MIX_ROWS, D_MODEL), lambda s: (s, 0)),
        out_shape=jax.ShapeDtypeStruct((bsz * seq, D_MODEL), F32),
        scratch_shapes=[
            pltpu.VMEM((D_MODEL, D_SSD), BF16),
            pltpu.VMEM((D_MODEL, D_XBC), BF16),
            pltpu.VMEM((D_MODEL, LANES), BF16),
            pltpu.VMEM((D_MODEL, D_POOL), BF16),
            pltpu.VMEM(poolw.shape, BF16),
            pltpu.VMEM((D_SSD + D_POOL, D_MODEL), BF16),
            pltpu.VMEM((W_SLOTS, MIX_W_ROWS, D_MODEL), F32),
            pltpu.SemaphoreType.DMA((W_SLOTS,)),
            pltpu.VMEM((MIX_ROWS, D_SSD), F32),
            pltpu.VMEM((D_XBC // LANES, n_chunks * (CONV_PAD + CHUNK), LANES), F32),
            pltpu.VMEM((MIX_ROWS, LANES), F32),
            pltpu.VMEM((D_POOL // LANES, n_chunks * (POOL_PAD + CHUNK), LANES), F32),
            pltpu.VMEM((MIX_ROWS, D_XBC), F32),
            pltpu.VMEM((D_STATE, D_SSD), F32),
            pltpu.VMEM((CHUNK, D_POOL), BF16),
            pltpu.VMEM((CHUNK, D_SSD), F32),
            pltpu.VMEM((CHUNK, LANES), F32),
        ],
        compiler_params=pltpu.CompilerParams(
            dimension_semantics=("arbitrary",), vmem_limit_bytes=VMEM_LIMIT_BYTES),
        name="mixer",
    )(x2, x2, mod3, mod3, norm_w, w_in_t, *vmem_consts, w_out)
    return out.reshape(bsz, seq, D_MODEL)


def _pad_lanes(v):
    return jnp.pad(v.reshape(1, -1), ((0, 0), (0, LANES - v.shape[0])))


def kernel(x, c, w_ada, b_ada, ffn1_norm, ffn1_w_gate, ffn1_w_up, ffn1_w_down, mix_norm, w_in, conv_w, conv_b, dt_bias, a_log, d_skip, ssd_norm_w, pool_w, pool_b, pool_scale, w_out, ffn2_norm, ffn2_w_gate, ffn2_w_up, ffn2_w_down, final_norm):
    bsz = x.shape[0]
    depth = w_ada.shape[0]
    row = lambda v: v.reshape(1, -1)
    for i in range(depth):
        mod3 = _adaln(c, w_ada[i], row(b_ada[i])).reshape(bsz, N_MOD, D_MODEL)
        last = i == depth - 1

        x = _ffn(x, mod3, row(ffn1_norm[i]), ffn1_w_gate, ffn1_w_up, ffn1_w_down, row(final_norm),
                 layer=i, mod_idx=0, final=False)

        x = _mixer(
            x, mod3, row(mix_norm[i]), jnp.swapaxes(w_in, 1, 2),
            conv_w[i], row(conv_b[i]), _pad_lanes(dt_bias[i]), _pad_lanes(a_log[i]),
            row(jnp.repeat(d_skip[i], HEAD_DIM)), row(ssd_norm_w[i]),
            pool_w[i].astype(BF16), row(pool_b[i]), row(pool_scale[i]), w_out, layer=i)

        x = _ffn(x, mod3, row(ffn2_norm[i]), ffn2_w_gate, ffn2_w_up, ffn2_w_down, row(final_norm),
                 layer=i, mod_idx=6, final=last)
    return x
```

```python
import functools

import jax
import jax.numpy as jnp
from jax import lax
from jax.experimental import pallas as pl
from jax.experimental.pallas import tpu as pltpu

F32 = jnp.float32
BF16 = jnp.bfloat16

D_MODEL = 1024
D_FF = 2816
N_MOD = 9
FFN_RES = 0.5
EPS = 1e-6
D_SSD = 1024
HEAD_DIM = 64
N_HEADS = 16
N_GROUPS = 4
HEADS_PER_GROUP = N_HEADS // N_GROUPS
D_STATE = 128
D_CONV = 4
CHUNK = 128
D_POOL = 1024
POOL_WINDOWS = (2, 4, 8, 16)
POOL_GROUP_DIM = 256
D_XBC = D_SSD + 2 * N_GROUPS * D_STATE
DT_COL = D_SSD + D_XBC

LANES = 128
SUBLANES = 8
MXU_COLS = 256
VMEM_LIMIT_BYTES = 60 * 1024 * 1024

FFN_ROWS = 512
NORM_SLICES = 8
FFN_W_ROWS_IN = 128
FFN_W_ROWS_OUT = 256
MIX_W_ROWS = 256
W_SLOTS = 8
MIX_ROWS = 512
PROJ_CHUNKS = 2
ADA_COLS = 1024
CONV_PAD = SUBLANES
POOL_PAD = 16

assert HEAD_DIM * 2 == LANES and D_STATE == LANES and CHUNK == LANES


def _dot(a, b):
    return jnp.dot(a, b, preferred_element_type=F32)


def _split3(v):
    hi = v.astype(BF16)
    r1 = v - hi.astype(F32)
    mid = r1.astype(BF16)
    lo = (r1 - mid.astype(F32)).astype(BF16)
    return hi, mid, lo


def _silu(v):
    return v * jax.nn.sigmoid(v)


def _rms(v, w):
    ms = jnp.mean(v * v, axis=-1, keepdims=True)
    return v * lax.rsqrt(ms + EPS) * w


def _const_spec(shape):
    zeros = (0,) * len(shape)
    return pl.BlockSpec(shape, lambda *_: zeros, pipeline_mode=pl.Buffered(1))


def _adaln_kernel(c_ref, w_ref, b_ref, o_ref):
    a_hi, a_mid, _ = _split3(_silu(c_ref[...]))
    res = _dot(jnp.concatenate([a_hi, a_mid], axis=0), w_ref[...].astype(BF16))
    o_ref[...] = res[0:SUBLANES] + res[SUBLANES:2 * SUBLANES] + b_ref[...]


def _adaln(c, w_ada, b_ada):
    bsz = c.shape[0]
    n = w_ada.shape[1]
    c_pad = jnp.pad(c, ((0, SUBLANES - bsz), (0, 0)))
    mod = pl.pallas_call(
        _adaln_kernel,
        grid=(n // ADA_COLS,),
        in_specs=[
            pl.BlockSpec((SUBLANES, D_MODEL), lambda j: (0, 0)),
            pl.BlockSpec((D_MODEL, ADA_COLS), lambda j: (0, j)),
            pl.BlockSpec((1, ADA_COLS), lambda j: (0, j)),
        ],
        out_specs=pl.BlockSpec((SUBLANES, ADA_COLS), lambda j: (0, j)),
        out_shape=jax.ShapeDtypeStruct((SUBLANES, n), F32),
        compiler_params=pltpu.CompilerParams(
            dimension_semantics=("arbitrary",), vmem_limit_bytes=VMEM_LIMIT_BYTES),
        name="adaln",
    )(c_pad, w_ada, b_ada)
    return mod[:bsz]


class _BlockStream:
    def __init__(self, stage_ref, sem):
        self.stage, self.sem, self.blocks = stage_ref, sem, []
        self.n_slots = stage_ref.shape[0]

    def add(self, src, sink):
        self.blocks.append((src, sink))

    def _slot(self, i):
        return self.stage.at[i % self.n_slots, pl.ds(0, self.blocks[i][0].shape[0])]

    def _copy(self, i):
        return pltpu.make_async_copy(self.blocks[i][0], self._slot(i), self.sem.at[i % self.n_slots])

    def prime(self):
        for i in range(min(self.n_slots, len(self.blocks))):
            self._copy(i).start()

    def drain(self):
        for i in range(len(self.blocks)):
            self._copy(i).wait()
            self.blocks[i][1](self._slot(i)[...])
            if i + self.n_slots < len(self.blocks):
                self._copy(i + self.n_slots).start()


def _stream_rows_bf16(stream, w_hbm, layer, dst_ref):
    rows = stream.stage.shape[1]

    def sink(r):
        def run(block):
            dst_ref[r:r + rows, :] = block.astype(BF16)
        return run

    for r in range(0, dst_ref.shape[0], rows):
        stream.add(w_hbm.at[layer, pl.ds(r, rows), :], sink(r))


def _ffn_kernel(xc_ref, xn_ref, modc_ref, modn_ref, nw_ref, wg_hbm, wu_hbm, wd_hbm, fw_ref, o_ref,
                h_s, wg_ref, wu_ref, wd_ref, stage_in, stage_out, sem_in, sem_out,
                *, layer, mod_idx, final):
    def in_norm(x, mod_ref):
        sh = mod_ref[0, pl.ds(mod_idx, 1), :]
        sc = mod_ref[0, pl.ds(mod_idx + 1, 1), :]
        return (_rms(x, nw_ref[...]) * (1.0 + sc) + sh).astype(BF16)

    step = pl.program_id(0)
    cur = h_s.at[step % 2]
    nxt = h_s.at[(step + 1) % 2]

    @pl.when(step == 0)
    def _():
        s_in, s_out = _BlockStream(stage_in, sem_in), _BlockStream(stage_out, sem_out)
        _stream_rows_bf16(s_in, wg_hbm, layer, wg_ref)
        _stream_rows_bf16(s_in, wu_hbm, layer, wu_ref)
        _stream_rows_bf16(s_out, wd_hbm, layer, wd_ref)
        s_in.prime()
        s_out.prime()
        s_in.drain()
        s_out.drain()
        cur[...] = in_norm(xc_ref[...], modc_ref)

    n_chunks = D_FF // MXU_COLS
    slice_rows = FFN_ROWS // NORM_SLICES
    acts = []
    for k in range(n_chunks):
        if k < NORM_SLICES:
            rs = slice(k * slice_rows, (k + 1) * slice_rows)
            nxt[rs, :] = in_norm(xn_ref[rs, :], modn_ref)
        cols = slice(k * MXU_COLS, (k + 1) * MXU_COLS)
        h = cur[...]
        g = _dot(h, wg_ref[:, cols])
        u = _dot(h, wu_ref[:, cols])
        acts.append((_silu(g) * u).astype(BF16))
    a = jnp.concatenate(acts, axis=1)
    o = _dot(a, wd_ref[...])
    gt = modc_ref[0, pl.ds(mod_idx + 2, 1), :]
    out = xc_ref[...] + (FFN_RES * gt) * o
    if final:
        out = _rms(out, fw_ref[...])
    o_ref[...] = out


def _ffn(x, mod3, norm_w, wg, wu, wd, final_w, *, layer, mod_idx, final):
    bsz, seq, _ = x.shape
    tiles_per_seq = seq // FFN_ROWS
    steps = bsz * tiles_per_seq
    nxt = lambda s: jnp.minimum(s + 1, steps - 1)
    x2 = x.reshape(bsz * seq, D_MODEL)
    hbm = pl.BlockSpec(memory_space=pl.ANY)
    out = pl.pallas_call(
        functools.partial(_ffn_kernel, layer=layer, mod_idx=mod_idx, final=final),
        grid=(steps,),
        in_specs=[
            pl.BlockSpec((FFN_ROWS, D_MODEL), lambda s: (s, 0)),
            pl.BlockSpec((FFN_ROWS, D_MODEL), lambda s: (nxt(s), 0)),
            pl.BlockSpec((1, N_MOD, D_MODEL), lambda s: (s // tiles_per_seq, 0, 0)),
            pl.BlockSpec((1, N_MOD, D_MODEL), lambda s: (nxt(s) // tiles_per_seq, 0, 0)),
            _const_spec((1, D_MODEL)),
            hbm, hbm, hbm,
            _const_spec((1, D_MODEL)),
        ],
        out_specs=pl.BlockSpec((FFN_ROWS, D_MODEL), lambda s: (s, 0)),
        out_shape=jax.ShapeDtypeStruct((bsz * seq, D_MODEL), F32),
        scratch_shapes=[
            pltpu.VMEM((2, FFN_ROWS, D_MODEL), BF16),
            pltpu.VMEM((D_MODEL, D_FF), BF16),
            pltpu.VMEM((D_MODEL, D_FF), BF16),
            pltpu.VMEM((D_FF, D_MODEL), BF16),
            pltpu.VMEM((W_SLOTS, FFN_W_ROWS_IN, D_FF), F32),
            pltpu.VMEM((W_SLOTS, FFN_W_ROWS_OUT, D_MODEL), F32),
            pltpu.SemaphoreType.DMA((W_SLOTS,)),
            pltpu.SemaphoreType.DMA((W_SLOTS,)),
        ],
        compiler_params=pltpu.CompilerParams(
            dimension_semantics=("arbitrary",), vmem_limit_bytes=VMEM_LIMIT_BYTES),
        name="ffn_final" if final else "ffn",
    )(x2, x2, mod3, mod3, norm_w, wg, wu, wd, final_w)
    return out.reshape(bsz, seq, D_MODEL)


def _mixer_kernel(xc_ref, xn_ref, modc_ref, modn_ref, nw_ref, wint_hbm,
                  convw_ref, convb_ref, dtb_ref, alog_ref, dskip_ref, ssdnw_ref,
                  poolw_in, poolb_ref, pools_ref, wout_hbm,
                  o_ref,
                  wz_ref, wxbc_ref, wdt_ref, wu_ref, poolw_ref, wout_ref, stage, sem,
                  z_s, raw_s, dtr_s, u_s, xbc_s, st_s, diff_s, zs_s, dtc_s, *, layer, tiles_per_seq):
    step = pl.program_id(0)
    tile_in_seq = step % tiles_per_seq
    next_starts_seq = tile_in_seq == tiles_per_seq - 1
    n_chunks = MIX_ROWS // CHUNK

    def in_norm(x, mod_ref):
        sh = mod_ref[0, 3:4, :]
        sc = mod_ref[0, 4:5, :]
        return (_rms(x, nw_ref[...]) * (1.0 + sc) + sh).astype(BF16)

    def rows(c):
        return slice(c * CHUNK, (c + 1) * CHUNK)

    def conv_base(c):
        return (c % n_chunks) * (CONV_PAD + CHUNK) + CONV_PAD

    def pool_base(c):
        return (c % n_chunks) * (POOL_PAD + CHUNK) + POOL_PAD

    pending = []

    def emit(k=None):
        n = len(pending) if k is None else min(k, len(pending))
        for _ in range(n):
            pending.pop(0)()

    def proj_pieces(c, n, h_n):
        rs = slice(c * CHUNK, (c + n) * CHUNK)

        def dense(dst, w_ref, lo, hi):
            def run():
                dst[rs, lo:hi] = _dot(h_n, w_ref[:, lo:hi])
            return run

        def slabs(dst, base_of, w_ref, lo, hi):
            def run():
                res = _dot(h_n, w_ref[:, lo:hi])
                for i in range(n):
                    base = base_of(c + i)
                    for j in range(lo // LANES, hi // LANES):
                        dst[j, base:base + CHUNK, :] = res[i * CHUNK:(i + 1) * CHUNK,
                                                           j * LANES - lo:(j + 1) * LANES - lo]
            return run

        out = [slabs(raw_s, conv_base, wxbc_ref, k, k + MXU_COLS) for k in range(0, D_XBC, MXU_COLS)]
        out += [slabs(u_s, pool_base, wu_ref, k, k + MXU_COLS) for k in range(0, D_POOL, MXU_COLS)]
        out += [dense(z_s, wz_ref, k, k + MXU_COLS) for k in range(0, D_SSD, MXU_COLS)]
        out.append(dense(dtr_s, wdt_ref, 0, LANES))
        return out

    @pl.when(step == 0)
    def _():
        stream = _BlockStream(stage, sem)
        w_rows, w_cols = stage.shape[1], stage.shape[2]

        def cast_into(dst, r, c0):
            def run(block):
                dst[r:r + w_rows, c0:c0 + w_cols] = block.astype(BF16)
            return run

        def transposed_into(dst, c0):
            def run(block):
                dst[:, c0:c0 + block.shape[0]] = block.T.astype(BF16)
            return run

        def dt_into(block):
            lane = lax.broadcasted_iota(jnp.int32, (D_MODEL, LANES), 1)
            wdt_ref[...] = jnp.where(lane < N_HEADS, block.T, 0.0).astype(BF16)

        def feature_rows(f0):
            return lambda r, n: wint_hbm.at[layer, pl.ds(f0 + r, n), :]

        for src, dst, width in ((feature_rows(0), wz_ref, D_SSD), (feature_rows(D_SSD), wxbc_ref, D_XBC),
                                (feature_rows(DT_COL + N_HEADS), wu_ref, D_POOL)):
            for r in range(0, width, w_rows):
                stream.add(src(r, w_rows), transposed_into(dst, r))
        stream.add(feature_rows(DT_COL)(0, LANES), dt_into)
        for r in range(0, D_SSD + D_POOL, w_rows):
            stream.add(wout_hbm.at[layer, pl.ds(r, w_rows), :], cast_into(wout_ref, r, 0))
        stream.prime()
        poolw_ref[...] = poolw_in[...]
        stream.drain()
        raw_s[:, 0:CONV_PAD, :] = jnp.zeros((D_XBC // LANES, CONV_PAD, LANES), F32)
        u_s[:, 0:POOL_PAD, :] = jnp.zeros((D_POOL // LANES, POOL_PAD, LANES), F32)
        for c in range(n_chunks):
            for run in proj_pieces(c, 1, in_norm(xc_ref[rows(c), :], modc_ref)):
                run()

    @pl.when(tile_in_seq == 0)
    def _():
        st_s[...] = jnp.zeros(st_s.shape, F32)

    gt = modc_ref[0, 5:6, :]
    a_neg = -jnp.exp(alog_ref[...])
    ri = lax.broadcasted_iota(jnp.int32, (CHUNK, CHUNK), 0)
    ci = lax.broadcasted_iota(jnp.int32, (CHUNK, CHUNK), 1)
    causal = ri >= ci
    tril = jnp.where(causal, 1.0, 0.0).astype(BF16)
    low_half = ci < HEAD_DIM
    low_half_row = lax.broadcasted_iota(jnp.int32, (1, LANES), 1) < HEAD_DIM

    def pre(c, seq_tile):
        rs = rows(c)
        last = c == n_chunks - 1

        def carry(hist):
            return jnp.where(next_starts_seq, jnp.zeros_like(hist), hist) if last else hist

        base, nbase = conv_base(c), conv_base(c + 1)
        for j in range(D_XBC // LANES):
            cols = slice(j * LANES, (j + 1) * LANES)
            xc = convb_ref[:, cols]
            for k in range(D_CONV):
                xc = xc + convw_ref[k:k + 1, cols] * raw_s[j, pl.ds(base - (D_CONV - 1) + k, CHUNK), :]
            raw_s[j, nbase - CONV_PAD:nbase, :] = carry(raw_s[j, base + CHUNK - CONV_PAD:base + CHUNK, :])
            xbc_s[rs, cols] = _silu(xc)
            if j % 2 == 1:
                emit(1)
        base, nbase = pool_base(c), pool_base(c + 1)
        diffs = []
        for gi, w in enumerate(POOL_WINDOWS):
            halves = []
            for j in range(gi * POOL_GROUP_DIM // LANES, (gi + 1) * POOL_GROUP_DIM // LANES):
                u_g = u_s[j, base:base + CHUNK, :]
                s = u_g
                for k in range(1, w):
                    s = s + u_s[j, pl.ds(base - k, CHUNK), :]
                if c == 0:
                    pos = seq_tile * MIX_ROWS + 1 + lax.broadcasted_iota(jnp.int32, (CHUNK, LANES), 0)
                    pooled = s / jnp.minimum(pos, w).astype(F32)
                else:
                    pooled = s * (1.0 / w)
                halves.append(pooled - u_g)
                u_s[j, nbase - POOL_PAD:nbase, :] = carry(u_s[j, base + CHUNK - POOL_PAD:base + CHUNK, :])
            diffs.append(jnp.concatenate(halves, axis=1).astype(BF16))
            emit(1)
        zs = _silu(z_s[rs, :])
        dt_c = jax.nn.softplus(dtr_s[rs, :] + dtb_ref[...])
        emit()
        return diffs, zs, dt_c

    def ssd(c, zs, dt_c):
        rs = rows(c)
        p_hi, p_mid, p_lo = _split3(dt_c * a_neg)
        cs = _dot(tril, p_hi) + _dot(tril, p_mid) + _dot(tril, p_lo)
        cs_t = cs.T
        dt_t = dt_c.T
        w_t = dt_t * jnp.exp(cs_t[:, CHUNK - 1:CHUNK] - cs_t)
        xs_c = xbc_s[rs, 0:D_SSD]
        emit(1)
        y_pairs = []
        for g in range(N_GROUPS):
            b0 = D_SSD + g * D_STATE
            c0 = D_SSD + N_GROUPS * D_STATE + g * D_STATE
            b_t = xbc_s[rs, b0:b0 + D_STATE].T
            c_g = xbc_s[rs, c0:c0 + D_STATE]
            cb = _dot(c_g.astype(BF16), b_t.astype(BF16))
            for pr in range(HEADS_PER_GROUP // 2):
                lhs, bw_t, cd = [], [], []
                for r in (2 * pr, 2 * pr + 1):
                    hd = g * HEADS_PER_GROUP + r
                    cs_bc = jnp.broadcast_to(cs[:, hd:hd + 1], (CHUNK, CHUNK))
                    decay = jnp.exp(jnp.where(causal, cs_bc - cs_t[hd:hd + 1, :], -jnp.inf))
                    scores = decay * (cb * dt_t[hd:hd + 1, :])
                    sd = jnp.exp(cs_bc)
                    lhs.append(jnp.concatenate([scores, c_g * sd], axis=1).astype(BF16))
                    bw_t.append((b_t * w_t[hd:hd + 1, :]).astype(BF16))
                    cd.append(sd[CHUNK - 1:CHUNK, :])
                col = (g * HEADS_PER_GROUP + 2 * pr) * HEAD_DIM
                x_p = xs_c[:, col:col + LANES]
                st = st_s[:, col:col + LANES]
                rhs = jnp.concatenate([x_p, st], axis=0).astype(BF16)
                y_pairs.append(jnp.where(low_half, _dot(lhs[0], rhs), _dot(lhs[1], rhs)))
                zero = jnp.zeros_like(x_p)
                x_split = jnp.concatenate(
                    [jnp.where(low_half, x_p, zero), jnp.where(low_half, zero, x_p)], axis=0)
                upd = _dot(jnp.concatenate(bw_t, axis=1), x_split.astype(BF16))
                st_s[:, col:col + LANES] = st * jnp.where(low_half_row, cd[0], cd[1]) + upd
                emit(1)
        y = jnp.concatenate(y_pairs, axis=1) + dskip_ref[...] * xs_c
        yz = y * zs
        gw = D_SSD // N_GROUPS
        parts = []
        for g in range(N_GROUPS):
            blk = yz[:, g * gw:(g + 1) * gw]
            ms = jnp.mean(blk * blk, axis=-1, keepdims=True)
            parts.append(blk * lax.rsqrt(ms + EPS))
        return (jnp.concatenate(parts, axis=1) * ssdnw_ref[...]).astype(BF16)

    def post_pieces(c, y_ssd, diffs):
        rs = rows(c)
        outs = [None] * len(POOL_WINDOWS)

        def pool_piece(gi):
            def run():
                lo, hi = gi * POOL_GROUP_DIM, (gi + 1) * POOL_GROUP_DIM
                o = (_dot(diffs[gi], poolw_ref[gi]) + poolb_ref[:, lo:hi]) * pools_ref[:, lo:hi]
                outs[gi] = o.astype(BF16)
            return run

        def out_piece(lo, hi):
            def run():
                y_pool = jnp.concatenate(outs, axis=1)
                out = (_dot(y_ssd, wout_ref[0:D_SSD, lo:hi])
                       + _dot(y_pool, wout_ref[D_SSD:D_SSD + D_POOL, lo:hi]))
                o_ref[rs, lo:hi] = xc_ref[rs, lo:hi] + gt[:, lo:hi] * out
            return run

        return ([pool_piece(gi) for gi in range(len(POOL_WINDOWS))]
                + [out_piece(k, k + MXU_COLS) for k in range(0, D_MODEL, MXU_COLS)])

    def hand_over(vals):
        diffs, zs, dt_c = vals
        for gi in range(len(POOL_WINDOWS)):
            diff_s[:, gi * POOL_GROUP_DIM:(gi + 1) * POOL_GROUP_DIM] = diffs[gi]
        zs_s[...] = zs
        dtc_s[...] = dt_c

    @pl.when(step == 0)
    def _():
        hand_over(pre(0, tile_in_seq))

    diffs = [diff_s[:, gi * POOL_GROUP_DIM:(gi + 1) * POOL_GROUP_DIM] for gi in range(len(POOL_WINDOWS))]
    zs, dt_c = zs_s[...], dtc_s[...]
    for c in range(n_chunks):
        y_ssd = ssd(c, zs, dt_c)
        pending.extend(post_pieces(c, y_ssd, diffs))
        if c + 1 < n_chunks:
            emit(2)
            diffs, zs, dt_c = pre(c + 1, tile_in_seq)
        else:
            hand_over(pre(0, (step + 1) % tiles_per_seq))
        if (c + 1) % PROJ_CHUNKS == 0:
            c0 = c + 1 - PROJ_CHUNKS
            rs = slice(c0 * CHUNK, (c + 1) * CHUNK)
            pending.extend(proj_pieces(c0, PROJ_CHUNKS, in_norm(xn_ref[rs, :], modn_ref)))
    emit()


def _mixer(x, mod3, norm_w, w_in_t, convw, convb, dtb, alog, dskip, ssdnw,
           poolw, poolb, pools, w_out, *, layer):
    bsz, seq, _ = x.shape
    tiles_per_seq = seq // MIX_ROWS
    n_chunks = MIX_ROWS // CHUNK
    steps = bsz * tiles_per_seq
    nxt = lambda s: jnp.minimum(s + 1, steps - 1)
    hbm = pl.BlockSpec(memory_space=pl.ANY)
    vmem_consts = [convw, convb, dtb, alog, dskip, ssdnw, poolw, poolb, pools]
    x2 = x.reshape(bsz * seq, D_MODEL)
    out = pl.pallas_call(
        functools.partial(_mixer_kernel, layer=layer, tiles_per_seq=tiles_per_seq),
        grid=(steps,),
        in_specs=[
            pl.BlockSpec((MIX_ROWS, D_MODEL), lambda s: (s, 0)),
            pl.BlockSpec((MIX_ROWS, D_MODEL), lambda s: (nxt(s), 0)),
            pl.BlockSpec((1, N_MOD, D_MODEL), lambda s: (s // tiles_per_seq, 0, 0)),
            pl.BlockSpec((1, N_MOD, D_MODEL), lambda s: (nxt(s) // tiles_per_seq, 0, 0)),
            _const_spec(norm_w.shape), hbm,
        ] + [_const_spec(a.shape) for a in vmem_consts] + [hbm],
        out_specs=pl.BlockSpec((MIX_ROWS, D_MODEL), lambda s: (s, 0)),
        out_shape=jax.ShapeDtypeStruct((bsz * seq, D_MODEL), F32),
        scratch_shapes=[
            pltpu.VMEM((D_MODEL, D_SSD), BF16),
            pltpu.VMEM((D_MODEL, D_XBC), BF16),
            pltpu.VMEM((D_MODEL, LANES), BF16),
            pltpu.VMEM((D_MODEL, D_POOL), BF16),
            pltpu.VMEM(poolw.shape, BF16),
            pltpu.VMEM((D_SSD + D_POOL, D_MODEL), BF16),
            pltpu.VMEM((W_SLOTS, MIX_W_ROWS, D_MODEL), F32),
            pltpu.SemaphoreType.DMA((W_SLOTS,)),
            pltpu.VMEM((MIX_ROWS, D_SSD), F32),
            pltpu.VMEM((D_XBC // LANES, n_chunks * (CONV_PAD + CHUNK), LANES), F32),
            pltpu.VMEM((MIX_ROWS, LANES), F32),
            pltpu.VMEM((D_POOL // LANES, n_chunks * (POOL_PAD + CHUNK), LANES), F32),
            pltpu.VMEM((MIX_ROWS, D_XBC), F32),
            pltpu.VMEM((D_STATE, D_SSD), F32),
            pltpu.VMEM((CHUNK, D_POOL), BF16),
            pltpu.VMEM((CHUNK, D_SSD), F32),
            pltpu.VMEM((CHUNK, LANES), F32),
        ],
        compiler_params=pltpu.CompilerParams(
            dimension_semantics=("arbitrary",), vmem_limit_bytes=VMEM_LIMIT_BYTES),
        name="mixer",
    )(x2, x2, mod3, mod3, norm_w, w_in_t, *vmem_consts, w_out)
    return out.reshape(bsz, seq, D_MODEL)


def _pad_lanes(v):
    return jnp.pad(v.reshape(1, -1), ((0, 0), (0, LANES - v.shape[0])))


def kernel(x, c, w_ada, b_ada, ffn1_norm, ffn1_w_gate, ffn1_w_up, ffn1_w_down, mix_norm, w_in, conv_w, conv_b, dt_bias, a_log, d_skip, ssd_norm_w, pool_w, pool_b, pool_scale, w_out, ffn2_norm, ffn2_w_gate, ffn2_w_up, ffn2_w_down, final_norm):
    bsz = x.shape[0]
    depth = w_ada.shape[0]
    row = lambda v: v.reshape(1, -1)
    for i in range(depth):
        mod3 = _adaln(c, w_ada[i], row(b_ada[i])).reshape(bsz, N_MOD, D_MODEL)
        last = i == depth - 1

        x = _ffn(x, mod3, row(ffn1_norm[i]), ffn1_w_gate, ffn1_w_up, ffn1_w_down, row(final_norm),
                 layer=i, mod_idx=0, final=False)

        x = _mixer(
            x, mod3, row(mix_norm[i]), jnp.swapaxes(w_in, 1, 2),
            conv_w[i], row(conv_b[i]), _pad_lanes(dt_bias[i]), _pad_lanes(a_log[i]),
            row(jnp.repeat(d_skip[i], HEAD_DIM)), row(ssd_norm_w[i]),
            pool_w[i].astype(BF16), row(pool_b[i]), row(pool_scale[i]), w_out, layer=i)

        x = _ffn(x, mod3, row(ffn2_norm[i]), ffn2_w_gate, ffn2_w_up, ffn2_w_down, row(final_norm),
                 layer=i, mod_idx=6, final=last)
    return x
```

```python
import functools

import jax
import jax.numpy as jnp
from jax import lax
from jax.experimental import pallas as pl
from jax.experimental.pallas import tpu as pltpu

F32 = jnp.float32
BF16 = jnp.bfloat16

D_MODEL = 1024
D_FF = 2816
N_MOD = 9
FFN_RES = 0.5
EPS = 1e-6
D_SSD = 1024
HEAD_DIM = 64
N_HEADS = 16
N_GROUPS = 4
HEADS_PER_GROUP = N_HEADS // N_GROUPS
D_STATE = 128
D_CONV = 4
CHUNK = 128
D_POOL = 1024
POOL_WINDOWS = (2, 4, 8, 16)
POOL_GROUP_DIM = 256
D_XBC = D_SSD + 2 * N_GROUPS * D_STATE
DT_COL = D_SSD + D_XBC

LANES = 128
SUBLANES = 8
MXU_COLS = 256
VMEM_LIMIT_BYTES = 60 * 1024 * 1024

FFN_ROWS = 512
NORM_SLICES = 8
FFN_W_ROWS_IN = 128
FFN_W_ROWS_OUT = 256
MIX_W_ROWS = 256
W_SLOTS = 8
MIX_ROWS = 512
ADA_COLS = 1024
CONV_PAD = SUBLANES
POOL_PAD = 16

assert HEAD_DIM * 2 == LANES and D_STATE == LANES and CHUNK == LANES


def _dot(a, b):
    return jnp.dot(a, b, preferred_element_type=F32)


def _split3(v):
    hi = v.astype(BF16)
    r1 = v - hi.astype(F32)
    mid = r1.astype(BF16)
    lo = (r1 - mid.astype(F32)).astype(BF16)
    return hi, mid, lo


def _silu(v):
    half = 0.5 * v
    return half + half * jnp.tanh(half)


def _rms(v, w):
    ms = jnp.mean(v * v, axis=-1, keepdims=True)
    return v * lax.rsqrt(ms + EPS) * w


def _const_spec(shape):
    zeros = (0,) * len(shape)
    return pl.BlockSpec(shape, lambda *_: zeros, pipeline_mode=pl.Buffered(1))


def _adaln_kernel(c_ref, w_ref, b_ref, o_ref):
    a_hi, a_mid, _ = _split3(_silu(c_ref[...]))
    res = _dot(jnp.concatenate([a_hi, a_mid], axis=0), w_ref[...].astype(BF16))
    o_ref[...] = res[0:SUBLANES] + res[SUBLANES:2 * SUBLANES] + b_ref[...]


def _adaln(c, w_ada, b_ada):
    bsz = c.shape[0]
    n = w_ada.shape[1]
    c_pad = jnp.pad(c, ((0, SUBLANES - bsz), (0, 0)))
    mod = pl.pallas_call(
        _adaln_kernel,
        grid=(n // ADA_COLS,),
        in_specs=[
            pl.BlockSpec((SUBLANES, D_MODEL), lambda j: (0, 0)),
            pl.BlockSpec((D_MODEL, ADA_COLS), lambda j: (0, j)),
            pl.BlockSpec((1, ADA_COLS), lambda j: (0, j)),
        ],
        out_specs=pl.BlockSpec((SUBLANES, ADA_COLS), lambda j: (0, j)),
        out_shape=jax.ShapeDtypeStruct((SUBLANES, n), F32),
        compiler_params=pltpu.CompilerParams(
            dimension_semantics=("arbitrary",), vmem_limit_bytes=VMEM_LIMIT_BYTES),
        name="adaln",
    )(c_pad, w_ada, b_ada)
    return mod[:bsz]


class _BlockStream:
    def __init__(self, stage_ref, sem):
        self.stage, self.sem, self.blocks = stage_ref, sem, []
        self.n_slots = stage_ref.shape[0]

    def add(self, src, sink):
        self.blocks.append((src, sink))

    def _slot(self, i):
        return self.stage.at[i % self.n_slots, pl.ds(0, self.blocks[i][0].shape[0])]

    def _copy(self, i):
        return pltpu.make_async_copy(self.blocks[i][0], self._slot(i), self.sem.at[i % self.n_slots])

    def prime(self):
        for i in range(min(self.n_slots, len(self.blocks))):
            self._copy(i).start()

    def drain(self):
        for i in range(len(self.blocks)):
            self._copy(i).wait()
            self.blocks[i][1](self._slot(i)[...])
            if i + self.n_slots < len(self.blocks):
                self._copy(i + self.n_slots).start()


def _stream_rows_bf16(stream, w_hbm, layer, dst_ref):
    rows = stream.stage.shape[1]

    def sink(r):
        def run(block):
            dst_ref[r:r + rows, :] = block.astype(BF16)
        return run

    for r in range(0, dst_ref.shape[0], rows):
        stream.add(w_hbm.at[layer, pl.ds(r, rows), :], sink(r))


def _ffn_kernel(xc_ref, xn_ref, modc_ref, modn_ref, nw_ref, wg_hbm, wu_hbm, wd_hbm, fw_ref, o_ref,
                h_s, wg_ref, wu_ref, wd_ref, stage_in, stage_out, sem_in, sem_out,
                *, layer, mod_idx, final):
    def in_norm(x, mod_ref):
        sh = mod_ref[0, pl.ds(mod_idx, 1), :]
        sc = mod_ref[0, pl.ds(mod_idx + 1, 1), :]
        return (_rms(x, nw_ref[...]) * (1.0 + sc) + sh).astype(BF16)

    step = pl.program_id(0)
    cur = h_s.at[step % 2]
    nxt = h_s.at[(step + 1) % 2]

    @pl.when(step == 0)
    def _():
        s_in, s_out = _BlockStream(stage_in, sem_in), _BlockStream(stage_out, sem_out)
        _stream_rows_bf16(s_in, wg_hbm, layer, wg_ref)
        _stream_rows_bf16(s_in, wu_hbm, layer, wu_ref)
        _stream_rows_bf16(s_out, wd_hbm, layer, wd_ref)
        s_in.prime()
        s_out.prime()
        s_in.drain()
        s_out.drain()
        cur[...] = in_norm(xc_ref[...], modc_ref)

    n_chunks = D_FF // MXU_COLS
    slice_rows = FFN_ROWS // NORM_SLICES
    acts = []
    for k in range(n_chunks):
        if k < NORM_SLICES:
            rs = slice(k * slice_rows, (k + 1) * slice_rows)
            nxt[rs, :] = in_norm(xn_ref[rs, :], modn_ref)
        cols = slice(k * MXU_COLS, (k + 1) * MXU_COLS)
        h = cur[...]
        g = _dot(h, wg_ref[:, cols])
        u = _dot(h, wu_ref[:, cols])
        acts.append((_silu(g) * u).astype(BF16))
    a = jnp.concatenate(acts, axis=1)
    o = _dot(a, wd_ref[...])
    gt = modc_ref[0, pl.ds(mod_idx + 2, 1), :]
    out = xc_ref[...] + (FFN_RES * gt) * o
    if final:
        out = _rms(out, fw_ref[...])
    o_ref[...] = out


def _ffn(x, mod3, norm_w, wg, wu, wd, final_w, *, layer, mod_idx, final):
    bsz, seq, _ = x.shape
    tiles_per_seq = seq // FFN_ROWS
    steps = bsz * tiles_per_seq
    nxt = lambda s: jnp.minimum(s + 1, steps - 1)
    x2 = x.reshape(bsz * seq, D_MODEL)
    hbm = pl.BlockSpec(memory_space=pl.ANY)
    out = pl.pallas_call(
        functools.partial(_ffn_kernel, layer=layer, mod_idx=mod_idx, final=final),
        grid=(steps,),
        in_specs=[
            pl.BlockSpec((FFN_ROWS, D_MODEL), lambda s: (s, 0)),
            pl.BlockSpec((FFN_ROWS, D_MODEL), lambda s: (nxt(s), 0)),
            pl.BlockSpec((1, N_MOD, D_MODEL), lambda s: (s // tiles_per_seq, 0, 0)),
            pl.BlockSpec((1, N_MOD, D_MODEL), lambda s: (nxt(s) // tiles_per_seq, 0, 0)),
            _const_spec((1, D_MODEL)),
            hbm, hbm, hbm,
            _const_spec((1, D_MODEL)),
        ],
        out_specs=pl.BlockSpec((FFN_ROWS, D_MODEL), lambda s: (s, 0)),
        out_shape=jax.ShapeDtypeStruct((bsz * seq, D_MODEL), F32),
        scratch_shapes=[
            pltpu.VMEM((2, FFN_ROWS, D_MODEL), BF16),
            pltpu.VMEM((D_MODEL, D_FF), BF16),
            pltpu.VMEM((D_MODEL, D_FF), BF16),
            pltpu.VMEM((D_FF, D_MODEL), BF16),
            pltpu.VMEM((W_SLOTS, FFN_W_ROWS_IN, D_FF), F32),
            pltpu.VMEM((W_SLOTS, FFN_W_ROWS_OUT, D_MODEL), F32),
            pltpu.SemaphoreType.DMA((W_SLOTS,)),
            pltpu.SemaphoreType.DMA((W_SLOTS,)),
        ],
        compiler_params=pltpu.CompilerParams(
            dimension_semantics=("arbitrary",), vmem_limit_bytes=VMEM_LIMIT_BYTES),
        name="ffn_final" if final else "ffn",
    )(x2, x2, mod3, mod3, norm_w, wg, wu, wd, final_w)
    return out.reshape(bsz, seq, D_MODEL)


def _mixer_kernel(xc_ref, xn_ref, modc_ref, modn_ref, nw_ref, wint_hbm,
                  convw_ref, convb_ref, dtb_ref, alog_ref, dskip_ref, ssdnw_ref,
                  poolw_in, poolb_ref, pools_ref, wout_hbm,
                  o_ref,
                  wz_ref, wxbc_ref, wdt_ref, wu_ref, poolw_ref, wout_ref, stage, sem,
                  z_s, raw_s, dtr_s, u_s, xbc_s, st_s, diff_s, zs_s, dtc_s, *, layer, tiles_per_seq):
    step = pl.program_id(0)
    tile_in_seq = step % tiles_per_seq
    next_starts_seq = tile_in_seq == tiles_per_seq - 1
    n_chunks = MIX_ROWS // CHUNK

    def in_norm(x, mod_ref):
        sh = mod_ref[0, 3:4, :]
        sc = mod_ref[0, 4:5, :]
        return (_rms(x, nw_ref[...]) * (1.0 + sc) + sh).astype(BF16)

    def rows(c):
        return slice(c * CHUNK, (c + 1) * CHUNK)

    def conv_base(c):
        return (c % n_chunks) * (CONV_PAD + CHUNK) + CONV_PAD

    def pool_base(c):
        return (c % n_chunks) * (POOL_PAD + CHUNK) + POOL_PAD

    pending = []

    def emit(k=None):
        n = len(pending) if k is None else min(k, len(pending))
        for _ in range(n):
            pending.pop(0)()

    def proj_pieces(c, h_n):
        rs = rows(c)

        def dense(dst, w_ref, lo, hi):
            def run():
                dst[rs, lo:hi] = _dot(h_n, w_ref[:, lo:hi])
            return run

        def slabs(dst, base, w_ref, lo, hi):
            def run():
                res = _dot(h_n, w_ref[:, lo:hi])
                for j in range(lo // LANES, hi // LANES):
                    dst[j, base:base + CHUNK, :] = res[:, j * LANES - lo:(j + 1) * LANES - lo]
            return run

        out = [slabs(raw_s, conv_base(c), wxbc_ref, k, k + MXU_COLS) for k in range(0, D_XBC, MXU_COLS)]
        out += [slabs(u_s, pool_base(c), wu_ref, k, k + MXU_COLS) for k in range(0, D_POOL, MXU_COLS)]
        out += [dense(z_s, wz_ref, k, k + MXU_COLS) for k in range(0, D_SSD, MXU_COLS)]
        out.append(dense(dtr_s, wdt_ref, 0, LANES))
        return out

    @pl.when(step == 0)
    def _():
        stream = _BlockStream(stage, sem)
        w_rows, w_cols = stage.shape[1], stage.shape[2]

        def cast_into(dst, r, c0):
            def run(block):
                dst[r:r + w_rows, c0:c0 + w_cols] = block.astype(BF16)
            return run

        def transposed_into(dst, c0):
            def run(block):
                dst[:, c0:c0 + block.shape[0]] = block.T.astype(BF16)
            return run

        def dt_into(block):
            lane = lax.broadcasted_iota(jnp.int32, (D_MODEL, LANES), 1)
            wdt_ref[...] = jnp.where(lane < N_HEADS, block.T, 0.0).astype(BF16)

        def feature_rows(f0):
            return lambda r, n: wint_hbm.at[layer, pl.ds(f0 + r, n), :]

        for src, dst, width in ((feature_rows(0), wz_ref, D_SSD), (feature_rows(D_SSD), wxbc_ref, D_XBC),
                                (feature_rows(DT_COL + N_HEADS), wu_ref, D_POOL)):
            for r in range(0, width, w_rows):
                stream.add(src(r, w_rows), transposed_into(dst, r))
        stream.add(feature_rows(DT_COL)(0, LANES), dt_into)
        for r in range(0, D_SSD + D_POOL, w_rows):
            stream.add(wout_hbm.at[layer, pl.ds(r, w_rows), :], cast_into(wout_ref, r, 0))
        stream.prime()
        poolw_ref[...] = poolw_in[...]
        stream.drain()
        raw_s[:, 0:CONV_PAD, :] = jnp.zeros((D_XBC // LANES, CONV_PAD, LANES), F32)
        u_s[:, 0:POOL_PAD, :] = jnp.zeros((D_POOL // LANES, POOL_PAD, LANES), F32)
        for c in range(n_chunks):
            for run in proj_pieces(c, in_norm(xc_ref[rows(c), :], modc_ref)):
                run()

    @pl.when(tile_in_seq == 0)
    def _():
        st_s[...] = jnp.zeros(st_s.shape, F32)

    gt = modc_ref[0, 5:6, :]
    a_neg = -jnp.exp(alog_ref[...])
    ri = lax.broadcasted_iota(jnp.int32, (CHUNK, CHUNK), 0)
    ci = lax.broadcasted_iota(jnp.int32, (CHUNK, CHUNK), 1)
    causal = ri >= ci
    tril = jnp.where(causal, 1.0, 0.0).astype(BF16)
    low_half = ci < HEAD_DIM
    low_half_row = lax.broadcasted_iota(jnp.int32, (1, LANES), 1) < HEAD_DIM

    def pre(c, seq_tile):
        rs = rows(c)
        last = c == n_chunks - 1

        def carry(hist):
            return jnp.where(next_starts_seq, jnp.zeros_like(hist), hist) if last else hist

        base, nbase = conv_base(c), conv_base(c + 1)
        for j in range(D_XBC // LANES):
            cols = slice(j * LANES, (j + 1) * LANES)
            xc = convb_ref[:, cols]
            for k in range(D_CONV):
                xc = xc + convw_ref[k:k + 1, cols] * raw_s[j, pl.ds(base - (D_CONV - 1) + k, CHUNK), :]
            raw_s[j, nbase - CONV_PAD:nbase, :] = carry(raw_s[j, base + CHUNK - CONV_PAD:base + CHUNK, :])
            xbc_s[rs, cols] = _silu(xc)
            if j % 2 == 1:
                emit(1)
        base, nbase = pool_base(c), pool_base(c + 1)
        diffs = []
        for gi, w in enumerate(POOL_WINDOWS):
            halves = []
            for j in range(gi * POOL_GROUP_DIM // LANES, (gi + 1) * POOL_GROUP_DIM // LANES):
                u_g = u_s[j, base:base + CHUNK, :]
                s = u_g
                for k in range(1, w):
                    s = s + u_s[j, pl.ds(base - k, CHUNK), :]
                if c == 0:
                    pos = seq_tile * MIX_ROWS + 1 + lax.broadcasted_iota(jnp.int32, (CHUNK, LANES), 0)
                    pooled = s / jnp.minimum(pos, w).astype(F32)
                else:
                    pooled = s * (1.0 / w)
                halves.append(pooled - u_g)
                u_s[j, nbase - POOL_PAD:nbase, :] = carry(u_s[j, base + CHUNK - POOL_PAD:base + CHUNK, :])
            diffs.append(jnp.concatenate(halves, axis=1).astype(BF16))
            emit(1)
        zs = _silu(z_s[rs, :])
        dt_c = jax.nn.softplus(dtr_s[rs, :] + dtb_ref[...])
        emit()
        return diffs, zs, dt_c

    def ssd(c, zs, dt_c):
        rs = rows(c)
        p_hi, p_mid, p_lo = _split3(dt_c * a_neg)
        cs = _dot(tril, p_hi) + _dot(tril, p_mid) + _dot(tril, p_lo)
        cs_t = cs.T
        dt_t = dt_c.T
        w_t = dt_t * jnp.exp(cs_t[:, CHUNK - 1:CHUNK] - cs_t)
        xs_c = xbc_s[rs, 0:D_SSD]
        emit(1)
        y_pairs = []
        for g in range(N_GROUPS):
            b0 = D_SSD + g * D_STATE
            c0 = D_SSD + N_GROUPS * D_STATE + g * D_STATE
            b_t = xbc_s[rs, b0:b0 + D_STATE].T
            c_g = xbc_s[rs, c0:c0 + D_STATE]
            cb = _dot(c_g.astype(BF16), b_t.astype(BF16))
            for pr in range(HEADS_PER_GROUP // 2):
                lhs, bw_t, cd = [], [], []
                for r in (2 * pr, 2 * pr + 1):
                    hd = g * HEADS_PER_GROUP + r
                    cs_bc = jnp.broadcast_to(cs[:, hd:hd + 1], (CHUNK, CHUNK))
                    decay = jnp.exp(jnp.where(causal, cs_bc - cs_t[hd:hd + 1, :], -jnp.inf))
                    scores = decay * (cb * dt_t[hd:hd + 1, :])
                    sd = jnp.exp(cs_bc)
                    lhs.append(jnp.concatenate([scores, c_g * sd], axis=1).astype(BF16))
                    bw_t.append((b_t * w_t[hd:hd + 1, :]).astype(BF16))
                    cd.append(sd[CHUNK - 1:CHUNK, :])
                col = (g * HEADS_PER_GROUP + 2 * pr) * HEAD_DIM
                x_p = xs_c[:, col:col + LANES]
                st = st_s[:, col:col + LANES]
                rhs = jnp.concatenate([x_p, st], axis=0).astype(BF16)
                y_pairs.append(jnp.where(low_half, _dot(lhs[0], rhs), _dot(lhs[1], rhs)))
                zero = jnp.zeros_like(x_p)
                x_split = jnp.concatenate(
                    [jnp.where(low_half, x_p, zero), jnp.where(low_half, zero, x_p)], axis=0)
                upd = _dot(jnp.concatenate(bw_t, axis=1), x_split.astype(BF16))
                st_s[:, col:col + LANES] = st * jnp.where(low_half_row, cd[0], cd[1]) + upd
                emit(1)
        y = jnp.concatenate(y_pairs, axis=1) + dskip_ref[...] * xs_c
        yz = y * zs
        gw = D_SSD // N_GROUPS
        parts = []
        for g in range(N_GROUPS):
            blk = yz[:, g * gw:(g + 1) * gw]
            ms = jnp.mean(blk * blk, axis=-1, keepdims=True)
            parts.append(blk * lax.rsqrt(ms + EPS))
        return (jnp.concatenate(parts, axis=1) * ssdnw_ref[...]).astype(BF16)

    def post_pieces(c, y_ssd, diffs):
        rs = rows(c)
        outs = [None] * len(POOL_WINDOWS)

        def pool_piece(gi):
            def run():
                lo, hi = gi * POOL_GROUP_DIM, (gi + 1) * POOL_GROUP_DIM
                o = (_dot(diffs[gi], poolw_ref[gi]) + poolb_ref[:, lo:hi]) * pools_ref[:, lo:hi]
                outs[gi] = o.astype(BF16)
            return run

        def out_piece(lo, hi):
            def run():
                y_pool = jnp.concatenate(outs, axis=1)
                out = (_dot(y_ssd, wout_ref[0:D_SSD, lo:hi])
                       + _dot(y_pool, wout_ref[D_SSD:D_SSD + D_POOL, lo:hi]))
                o_ref[rs, lo:hi] = xc_ref[rs, lo:hi] + gt[:, lo:hi] * out
            return run

        return ([pool_piece(gi) for gi in range(len(POOL_WINDOWS))]
                + [out_piece(k, k + MXU_COLS) for k in range(0, D_MODEL, MXU_COLS)])

    def hand_over(vals):
        diffs, zs, dt_c = vals
        for gi in range(len(POOL_WINDOWS)):
            diff_s[:, gi * POOL_GROUP_DIM:(gi + 1) * POOL_GROUP_DIM] = diffs[gi]
        zs_s[...] = zs
        dtc_s[...] = dt_c

    @pl.when(step == 0)
    def _():
        hand_over(pre(0, tile_in_seq))

    diffs = [diff_s[:, gi * POOL_GROUP_DIM:(gi + 1) * POOL_GROUP_DIM] for gi in range(len(POOL_WINDOWS))]
    zs, dt_c = zs_s[...], dtc_s[...]
    h_n = in_norm(xn_ref[rows(0), :], modn_ref)
    pending.extend(proj_pieces(0, h_n))
    for c in range(n_chunks):
        y_ssd = ssd(c, zs, dt_c)
        pending.extend(post_pieces(c, y_ssd, diffs))
        if c + 1 < n_chunks:
            h_n = in_norm(xn_ref[rows(c + 1), :], modn_ref)
            emit(2)
            diffs, zs, dt_c = pre(c + 1, tile_in_seq)
            pending.extend(proj_pieces(c + 1, h_n))
        else:
            hand_over(pre(0, (step + 1) % tiles_per_seq))
            emit()


def _mixer(x, mod3, norm_w, w_in_t, convw, convb, dtb, alog, dskip, ssdnw,
           poolw, poolb, pools, w_out, *, layer):
    bsz, seq, _ = x.shape
    tiles_per_seq = seq // MIX_ROWS
    n_chunks = MIX_ROWS // CHUNK
    steps = bsz * tiles_per_seq
    nxt = lambda s: jnp.minimum(s + 1, steps - 1)
    hbm = pl.BlockSpec(memory_space=pl.ANY)
    vmem_consts = [convw, convb, dtb, alog, dskip, ssdnw, poolw, poolb, pools]
    x2 = x.reshape(bsz * seq, D_MODEL)
    out = pl.pallas_call(
        functools.partial(_mixer_kernel, layer=layer, tiles_per_seq=tiles_per_seq),
        grid=(steps,),
        in_specs=[
            pl.BlockSpec((MIX_ROWS, D_MODEL), lambda s: (s, 0)),
            pl.BlockSpec((MIX_ROWS, D_MODEL), lambda s: (nxt(s), 0)),
            pl.BlockSpec((1, N_MOD, D_MODEL), lambda s: (s // tiles_per_seq, 0, 0)),
            pl.BlockSpec((1, N_MOD, D_MODEL), lambda s: (nxt(s) // tiles_per_seq, 0, 0)),
            _const_spec(norm_w.shape), hbm,
        ] + [_const_spec(a.shape) for a in vmem_consts] + [hbm],
        out_specs=pl.BlockSpec((MIX_ROWS, D_MODEL), lambda s: (s, 0)),
        out_shape=jax.ShapeDtypeStruct((bsz * seq, D_MODEL), F32),
        scratch_shapes=[
            pltpu.VMEM((D_MODEL, D_SSD), BF16),
            pltpu.VMEM((D_MODEL, D_XBC), BF16),
            pltpu.VMEM((D_MODEL, LANES), BF16),
            pltpu.VMEM((D_MODEL, D_POOL), BF16),
            pltpu.VMEM(poolw.shape, BF16),
            pltpu.VMEM((D_SSD + D_POOL, D_MODEL), BF16),
            pltpu.VMEM((W_SLOTS, MIX_W_ROWS, D_MODEL), F32),
            pltpu.SemaphoreType.DMA((W_SLOTS,)),
            pltpu.VMEM((MIX_ROWS, D_SSD), F32),
            pltpu.VMEM((D_XBC // LANES, n_chunks * (CONV_PAD + CHUNK), LANES), F32),
            pltpu.VMEM((MIX_ROWS, LANES), F32),
            pltpu.VMEM((D_POOL // LANES, n_chunks * (POOL_PAD + CHUNK), LANES), F32),
            pltpu.VMEM((MIX_ROWS, D_XBC), F32),
            pltpu.VMEM((D_STATE, D_SSD), F32),
            pltpu.VMEM((CHUNK, D_POOL), BF16),
            pltpu.VMEM((CHUNK, D_SSD), F32),
            pltpu.VMEM((CHUNK, LANES), F32),
        ],
        compiler_params=pltpu.CompilerParams(
            dimension_semantics=("arbitrary",), vmem_limit_bytes=VMEM_LIMIT_BYTES),
        name="mixer",
    )(x2, x2, mod3, mod3, norm_w, w_in_t, *vmem_consts, w_out)
    return out.reshape(bsz, seq, D_MODEL)


def _pad_lanes(v):
    return jnp.pad(v.reshape(1, -1), ((0, 0), (0, LANES - v.shape[0])))


def kernel(x, c, w_ada, b_ada, ffn1_norm, ffn1_w_gate, ffn1_w_up, ffn1_w_down, mix_norm, w_in, conv_w, conv_b, dt_bias, a_log, d_skip, ssd_norm_w, pool_w, pool_b, pool_scale, w_out, ffn2_norm, ffn2_w_gate, ffn2_w_up, ffn2_w_down, final_norm):
    bsz = x.shape[0]
    depth = w_ada.shape[0]
    row = lambda v: v.reshape(1, -1)
    for i in range(depth):
        mod3 = _adaln(c, w_ada[i], row(b_ada[i])).reshape(bsz, N_MOD, D_MODEL)
        last = i == depth - 1

        x = _ffn(x, mod3, row(ffn1_norm[i]), ffn1_w_gate, ffn1_w_up, ffn1_w_down, row(final_norm),
                 layer=i, mod_idx=0, final=False)

        x = _mixer(
            x, mod3, row(mix_norm[i]), jnp.swapaxes(w_in, 1, 2),
            conv_w[i], row(conv_b[i]), _pad_lanes(dt_bias[i]), _pad_lanes(a_log[i]),
            row(jnp.repeat(d_skip[i], HEAD_DIM)), row(ssd_norm_w[i]),
            pool_w[i].astype(BF16), row(pool_b[i]), row(pool_scale[i]), w_out, layer=i)

        x = _ffn(x, mod3, row(ffn2_norm[i]), ffn2_w_gate, ffn2_w_up, ffn2_w_down, row(final_norm),
                 layer=i, mod_idx=6, final=last)
    return x
```

```python
import functools

import jax
import jax.numpy as jnp
from jax import lax
from jax.experimental import pallas as pl
from jax.experimental.pallas import tpu as pltpu

F32 = jnp.float32
BF16 = jnp.bfloat16

D_MODEL = 1024
D_FF = 2816
N_MOD = 9
FFN_RES = 0.5
EPS = 1e-6
D_SSD = 1024
HEAD_DIM = 64
N_HEADS = 16
N_GROUPS = 4
HEADS_PER_GROUP = N_HEADS // N_GROUPS
D_STATE = 128
D_CONV = 4
CHUNK = 128
D_POOL = 1024
POOL_WINDOWS = (2, 4, 8, 16)
POOL_GROUP_DIM = 256
D_XBC = D_SSD + 2 * N_GROUPS * D_STATE
DT_COL = D_SSD + D_XBC

LANES = 128
SUBLANES = 8
MXU_COLS = 256
VMEM_LIMIT_BYTES = 60 * 1024 * 1024

FFN_ROWS = 512
NORM_SLICES = 8
FFN_W_ROWS_IN = 128
FFN_W_ROWS_OUT = 256
MIX_W_ROWS = 256
W_SLOTS = 8
MIX_ROWS = 512
ADA_COLS = 1024
CONV_PAD = SUBLANES
POOL_PAD = 16

assert HEAD_DIM * 2 == LANES and D_STATE == LANES and CHUNK == LANES


def _dot(a, b):
    return jnp.dot(a, b, preferred_element_type=F32)


def _split3(v):
    hi = v.astype(BF16)
    r1 = v - hi.astype(F32)
    mid = r1.astype(BF16)
    lo = (r1 - mid.astype(F32)).astype(BF16)
    return hi, mid, lo


def _silu_of_twice(half):
    return half + half * jnp.tanh(half)


def _silu(v):
    return _silu_of_twice(0.5 * v)


def _rms(v, w):
    ms = jnp.mean(v * v, axis=-1, keepdims=True)
    return v * lax.rsqrt(ms + EPS) * w


def _const_spec(shape):
    zeros = (0,) * len(shape)
    return pl.BlockSpec(shape, lambda *_: zeros, pipeline_mode=pl.Buffered(1))


def _adaln_kernel(c_ref, w_ref, b_ref, o_ref):
    a_hi, a_mid, _ = _split3(_silu(c_ref[...]))
    res = _dot(jnp.concatenate([a_hi, a_mid], axis=0), w_ref[...].astype(BF16))
    o_ref[...] = res[0:SUBLANES] + res[SUBLANES:2 * SUBLANES] + b_ref[...]


def _adaln(c, w_ada, b_ada):
    bsz = c.shape[0]
    n = w_ada.shape[1]
    c_pad = jnp.pad(c, ((0, SUBLANES - bsz), (0, 0)))
    mod = pl.pallas_call(
        _adaln_kernel,
        grid=(n // ADA_COLS,),
        in_specs=[
            pl.BlockSpec((SUBLANES, D_MODEL), lambda j: (0, 0)),
            pl.BlockSpec((D_MODEL, ADA_COLS), lambda j: (0, j)),
            pl.BlockSpec((1, ADA_COLS), lambda j: (0, j)),
        ],
        out_specs=pl.BlockSpec((SUBLANES, ADA_COLS), lambda j: (0, j)),
        out_shape=jax.ShapeDtypeStruct((SUBLANES, n), F32),
        compiler_params=pltpu.CompilerParams(
            dimension_semantics=("arbitrary",), vmem_limit_bytes=VMEM_LIMIT_BYTES),
        name="adaln",
    )(c_pad, w_ada, b_ada)
    return mod[:bsz]


class _BlockStream:
    def __init__(self, stage_ref, sem):
        self.stage, self.sem, self.blocks = stage_ref, sem, []
        self.n_slots = stage_ref.shape[0]

    def add(self, src, sink):
        self.blocks.append((src, sink))

    def _slot(self, i):
        return self.stage.at[i % self.n_slots, pl.ds(0, self.blocks[i][0].shape[0])]

    def _copy(self, i):
        return pltpu.make_async_copy(self.blocks[i][0], self._slot(i), self.sem.at[i % self.n_slots])

    def prime(self):
        for i in range(min(self.n_slots, len(self.blocks))):
            self._copy(i).start()

    def drain(self):
        for i in range(len(self.blocks)):
            self._copy(i).wait()
            self.blocks[i][1](self._slot(i)[...])
            if i + self.n_slots < len(self.blocks):
                self._copy(i + self.n_slots).start()


def _stream_rows_bf16(stream, w_hbm, layer, dst_ref, scale=1.0):
    rows = stream.stage.shape[1]

    def sink(r):
        def run(block):
            dst_ref[r:r + rows, :] = (block if scale == 1.0 else scale * block).astype(BF16)
        return run

    for r in range(0, dst_ref.shape[0], rows):
        stream.add(w_hbm.at[layer, pl.ds(r, rows), :], sink(r))


def _ffn_kernel(xc_ref, xn_ref, modc_ref, modn_ref, nw_ref, wg_hbm, wu_hbm, wd_hbm, fw_ref, o_ref,
                h_s, wg_ref, wu_ref, wd_ref, stage_in, stage_out, sem_in, sem_out,
                *, layer, mod_idx, final):
    def in_norm(x, mod_ref):
        sh = mod_ref[0, pl.ds(mod_idx, 1), :]
        sc = mod_ref[0, pl.ds(mod_idx + 1, 1), :]
        return (_rms(x, nw_ref[...]) * (1.0 + sc) + sh).astype(BF16)

    step = pl.program_id(0)
    cur = h_s.at[step % 2]
    nxt = h_s.at[(step + 1) % 2]

    @pl.when(step == 0)
    def _():
        s_in, s_out = _BlockStream(stage_in, sem_in), _BlockStream(stage_out, sem_out)
        _stream_rows_bf16(s_in, wg_hbm, layer, wg_ref, scale=0.5)
        _stream_rows_bf16(s_in, wu_hbm, layer, wu_ref)
        _stream_rows_bf16(s_out, wd_hbm, layer, wd_ref)
        s_in.prime()
        s_out.prime()
        s_in.drain()
        s_out.drain()
        cur[...] = in_norm(xc_ref[...], modc_ref)

    n_chunks = D_FF // MXU_COLS
    slice_rows = FFN_ROWS // NORM_SLICES
    acts = []
    for k in range(n_chunks):
        if k < NORM_SLICES:
            rs = slice(k * slice_rows, (k + 1) * slice_rows)
            nxt[rs, :] = in_norm(xn_ref[rs, :], modn_ref)
        cols = slice(k * MXU_COLS, (k + 1) * MXU_COLS)
        h = cur[...]
        g_half = _dot(h, wg_ref[:, cols])
        u = _dot(h, wu_ref[:, cols])
        acts.append((_silu_of_twice(g_half) * u).astype(BF16))
    a = jnp.concatenate(acts, axis=1)
    o = _dot(a, wd_ref[...])
    gt = modc_ref[0, pl.ds(mod_idx + 2, 1), :]
    out = xc_ref[...] + (FFN_RES * gt) * o
    if final:
        out = _rms(out, fw_ref[...])
    o_ref[...] = out


def _ffn(x, mod3, norm_w, wg, wu, wd, final_w, *, layer, mod_idx, final):
    bsz, seq, _ = x.shape
    tiles_per_seq = seq // FFN_ROWS
    steps = bsz * tiles_per_seq
    nxt = lambda s: jnp.minimum(s + 1, steps - 1)
    x2 = x.reshape(bsz * seq, D_MODEL)
    hbm = pl.BlockSpec(memory_space=pl.ANY)
    out = pl.pallas_call(
        functools.partial(_ffn_kernel, layer=layer, mod_idx=mod_idx, final=final),
        grid=(steps,),
        in_specs=[
            pl.BlockSpec((FFN_ROWS, D_MODEL), lambda s: (s, 0)),
            pl.BlockSpec((FFN_ROWS, D_MODEL), lambda s: (nxt(s), 0)),
            pl.BlockSpec((1, N_MOD, D_MODEL), lambda s: (s // tiles_per_seq, 0, 0)),
            pl.BlockSpec((1, N_MOD, D_MODEL), lambda s: (nxt(s) // tiles_per_seq, 0, 0)),
            _const_spec((1, D_MODEL)),
            hbm, hbm, hbm,
            _const_spec((1, D_MODEL)),
        ],
        out_specs=pl.BlockSpec((FFN_ROWS, D_MODEL), lambda s: (s, 0)),
        out_shape=jax.ShapeDtypeStruct((bsz * seq, D_MODEL), F32),
        scratch_shapes=[
            pltpu.VMEM((2, FFN_ROWS, D_MODEL), BF16),
            pltpu.VMEM((D_MODEL, D_FF), BF16),
            pltpu.VMEM((D_MODEL, D_FF), BF16),
            pltpu.VMEM((D_FF, D_MODEL), BF16),
            pltpu.VMEM((W_SLOTS, FFN_W_ROWS_IN, D_FF), F32),
            pltpu.VMEM((W_SLOTS, FFN_W_ROWS_OUT, D_MODEL), F32),
            pltpu.SemaphoreType.DMA((W_SLOTS,)),
            pltpu.SemaphoreType.DMA((W_SLOTS,)),
        ],
        compiler_params=pltpu.CompilerParams(
            dimension_semantics=("arbitrary",), vmem_limit_bytes=VMEM_LIMIT_BYTES),
        name="ffn_final" if final else "ffn",
    )(x2, x2, mod3, mod3, norm_w, wg, wu, wd, final_w)
    return out.reshape(bsz, seq, D_MODEL)


def _mixer_kernel(xc_ref, xn_ref, modc_ref, modn_ref, nw_ref, wint_hbm,
                  convw_ref, convb_ref, dtb_ref, alog_ref, dskip_ref, ssdnw_ref,
                  poolw_in, poolb_ref, pools_ref, wout_hbm,
                  o_ref,
                  wz_ref, wxbc_ref, wdt_ref, wu_ref, poolw_ref, wout_ref, stage, sem,
                  z_s, raw_s, dtr_s, u_s, xbc_s, st_s, diff_s, zs_s, dtc_s, *, layer, tiles_per_seq):
    step = pl.program_id(0)
    tile_in_seq = step % tiles_per_seq
    next_starts_seq = tile_in_seq == tiles_per_seq - 1
    n_chunks = MIX_ROWS // CHUNK

    def in_norm(x, mod_ref):
        sh = mod_ref[0, 3:4, :]
        sc = mod_ref[0, 4:5, :]
        return (_rms(x, nw_ref[...]) * (1.0 + sc) + sh).astype(BF16)

    def rows(c):
        return slice(c * CHUNK, (c + 1) * CHUNK)

    def conv_base(c):
        return (c % n_chunks) * (CONV_PAD + CHUNK) + CONV_PAD

    def pool_base(c):
        return (c % n_chunks) * (POOL_PAD + CHUNK) + POOL_PAD

    pending = []

    def emit(k=None):
        n = len(pending) if k is None else min(k, len(pending))
        for _ in range(n):
            pending.pop(0)()

    def proj_pieces(c, h_n):
        rs = rows(c)

        def dense(dst, w_ref, lo, hi):
            def run():
                dst[rs, lo:hi] = _dot(h_n, w_ref[:, lo:hi])
            return run

        def slabs(dst, base, w_ref, lo, hi):
            def run():
                res = _dot(h_n, w_ref[:, lo:hi])
                for j in range(lo // LANES, hi // LANES):
                    dst[j, base:base + CHUNK, :] = res[:, j * LANES - lo:(j + 1) * LANES - lo]
            return run

        out = [slabs(raw_s, conv_base(c), wxbc_ref, k, k + MXU_COLS) for k in range(0, D_XBC, MXU_COLS)]
        out += [slabs(u_s, pool_base(c), wu_ref, k, k + MXU_COLS) for k in range(0, D_POOL, MXU_COLS)]
        out += [dense(z_s, wz_ref, k, k + MXU_COLS) for k in range(0, D_SSD, MXU_COLS)]
        out.append(dense(dtr_s, wdt_ref, 0, LANES))
        return out

    @pl.when(step == 0)
    def _():
        stream = _BlockStream(stage, sem)
        w_rows, w_cols = stage.shape[1], stage.shape[2]

        def cast_into(dst, r, c0):
            def run(block):
                dst[r:r + w_rows, c0:c0 + w_cols] = block.astype(BF16)
            return run

        def transposed_into(dst, c0, scale):
            def run(block):
                dst[:, c0:c0 + block.shape[0]] = (block if scale == 1.0 else scale * block).T.astype(BF16)
            return run

        def dt_into(block):
            lane = lax.broadcasted_iota(jnp.int32, (D_MODEL, LANES), 1)
            wdt_ref[...] = jnp.where(lane < N_HEADS, block.T, 0.0).astype(BF16)

        def feature_rows(f0):
            return lambda r, n: wint_hbm.at[layer, pl.ds(f0 + r, n), :]

        for src, dst, width, scale in ((feature_rows(0), wz_ref, D_SSD, 0.5),
                                       (feature_rows(D_SSD), wxbc_ref, D_XBC, 1.0),
                                       (feature_rows(DT_COL + N_HEADS), wu_ref, D_POOL, 1.0)):
            for r in range(0, width, w_rows):
                stream.add(src(r, w_rows), transposed_into(dst, r, scale))
        stream.add(feature_rows(DT_COL)(0, LANES), dt_into)
        for r in range(0, D_SSD + D_POOL, w_rows):
            stream.add(wout_hbm.at[layer, pl.ds(r, w_rows), :], cast_into(wout_ref, r, 0))
        stream.prime()
        poolw_ref[...] = poolw_in[...]
        stream.drain()
        raw_s[:, 0:CONV_PAD, :] = jnp.zeros((D_XBC // LANES, CONV_PAD, LANES), F32)
        u_s[:, 0:POOL_PAD, :] = jnp.zeros((D_POOL // LANES, POOL_PAD, LANES), F32)
        for c in range(n_chunks):
            for run in proj_pieces(c, in_norm(xc_ref[rows(c), :], modc_ref)):
                run()

    @pl.when(tile_in_seq == 0)
    def _():
        st_s[...] = jnp.zeros(st_s.shape, F32)

    gt = modc_ref[0, 5:6, :]
    a_neg = -jnp.exp(alog_ref[...])
    conv_w_half = 0.5 * convw_ref[...]
    conv_b_half = 0.5 * convb_ref[...]
    ri = lax.broadcasted_iota(jnp.int32, (CHUNK, CHUNK), 0)
    ci = lax.broadcasted_iota(jnp.int32, (CHUNK, CHUNK), 1)
    causal = ri >= ci
    tril = jnp.where(causal, 1.0, 0.0).astype(BF16)
    low_half = ci < HEAD_DIM
    low_half_row = lax.broadcasted_iota(jnp.int32, (1, LANES), 1) < HEAD_DIM

    def pre(c, seq_tile):
        rs = rows(c)
        last = c == n_chunks - 1

        def carry(hist):
            return jnp.where(next_starts_seq, jnp.zeros_like(hist), hist) if last else hist

        base, nbase = conv_base(c), conv_base(c + 1)
        for j in range(D_XBC // LANES):
            cols = slice(j * LANES, (j + 1) * LANES)
            xc_half = conv_b_half[:, cols]
            for k in range(D_CONV):
                xc_half = xc_half + conv_w_half[k:k + 1, cols] * raw_s[j, pl.ds(base - (D_CONV - 1) + k, CHUNK), :]
            raw_s[j, nbase - CONV_PAD:nbase, :] = carry(raw_s[j, base + CHUNK - CONV_PAD:base + CHUNK, :])
            xbc_s[rs, cols] = _silu_of_twice(xc_half)
            if j % 2 == 1:
                emit(1)
        base, nbase = pool_base(c), pool_base(c + 1)
        diffs = []
        for gi, w in enumerate(POOL_WINDOWS):
            halves = []
            for j in range(gi * POOL_GROUP_DIM // LANES, (gi + 1) * POOL_GROUP_DIM // LANES):
                u_g = u_s[j, base:base + CHUNK, :]
                s = u_g
                for k in range(1, w):
                    s = s + u_s[j, pl.ds(base - k, CHUNK), :]
                if c == 0:
                    pos = seq_tile * MIX_ROWS + 1 + lax.broadcasted_iota(jnp.int32, (CHUNK, LANES), 0)
                    pooled = s / jnp.minimum(pos, w).astype(F32)
                else:
                    pooled = s * (1.0 / w)
                halves.append(pooled - u_g)
                u_s[j, nbase - POOL_PAD:nbase, :] = carry(u_s[j, base + CHUNK - POOL_PAD:base + CHUNK, :])
            diffs.append(jnp.concatenate(halves, axis=1).astype(BF16))
            emit(1)
        zs = _silu_of_twice(z_s[rs, :])
        dt_c = jax.nn.softplus(dtr_s[rs, :] + dtb_ref[...])
        emit()
        return diffs, zs, dt_c

    def ssd(c, zs, dt_c):
        rs = rows(c)
        p_hi, p_mid, p_lo = _split3(dt_c * a_neg)
        cs = _dot(tril, p_hi) + _dot(tril, p_mid) + _dot(tril, p_lo)
        cs_t = cs.T
        dt_t = dt_c.T
        w_t = dt_t * jnp.exp(cs_t[:, CHUNK - 1:CHUNK] - cs_t)
        xs_c = xbc_s[rs, 0:D_SSD]
        emit(1)
        y_pairs = []
        for g in range(N_GROUPS):
            b0 = D_SSD + g * D_STATE
            c0 = D_SSD + N_GROUPS * D_STATE + g * D_STATE
            b_t = xbc_s[rs, b0:b0 + D_STATE].T
            c_g = xbc_s[rs, c0:c0 + D_STATE]
            cb = _dot(c_g.astype(BF16), b_t.astype(BF16))
            for pr in range(HEADS_PER_GROUP // 2):
                lhs, bw_t, cd = [], [], []
                for r in (2 * pr, 2 * pr + 1):
                    hd = g * HEADS_PER_GROUP + r
                    cs_bc = jnp.broadcast_to(cs[:, hd:hd + 1], (CHUNK, CHUNK))
                    decay = jnp.exp(jnp.where(causal, cs_bc - cs_t[hd:hd + 1, :], -jnp.inf))
                    scores = decay * (cb * dt_t[hd:hd + 1, :])
                    sd = jnp.exp(cs_bc)
                    lhs.append(jnp.concatenate([scores, c_g * sd], axis=1).astype(BF16))
                    bw_t.append((b_t * w_t[hd:hd + 1, :]).astype(BF16))
                    cd.append(sd[CHUNK - 1:CHUNK, :])
                col = (g * HEADS_PER_GROUP + 2 * pr) * HEAD_DIM
                x_p = xs_c[:, col:col + LANES]
                st = st_s[:, col:col + LANES]
                rhs = jnp.concatenate([x_p, st], axis=0).astype(BF16)
                y_pairs.append(jnp.where(low_half, _dot(lhs[0], rhs), _dot(lhs[1], rhs)))
                zero = jnp.zeros_like(x_p)
                x_split = jnp.concatenate(
                    [jnp.where(low_half, x_p, zero), jnp.where(low_half, zero, x_p)], axis=0)
                upd = _dot(jnp.concatenate(bw_t, axis=1), x_split.astype(BF16))
                st_s[:, col:col + LANES] = st * jnp.where(low_half_row, cd[0], cd[1]) + upd
                emit(1)
        y = jnp.concatenate(y_pairs, axis=1) + dskip_ref[...] * xs_c
        yz = y * zs
        gw = D_SSD // N_GROUPS
        parts = []
        for g in range(N_GROUPS):
            blk = yz[:, g * gw:(g + 1) * gw]
            ms = jnp.mean(blk * blk, axis=-1, keepdims=True)
            parts.append(blk * lax.rsqrt(ms + EPS))
        return (jnp.concatenate(parts, axis=1) * ssdnw_ref[...]).astype(BF16)

    def post_pieces(c, y_ssd, diffs):
        rs = rows(c)
        outs = [None] * len(POOL_WINDOWS)

        def pool_piece(gi):
            def run():
                lo, hi = gi * POOL_GROUP_DIM, (gi + 1) * POOL_GROUP_DIM
                o = (_dot(diffs[gi], poolw_ref[gi]) + poolb_ref[:, lo:hi]) * pools_ref[:, lo:hi]
                outs[gi] = o.astype(BF16)
            return run

        def out_piece(lo, hi):
            def run():
                y_pool = jnp.concatenate(outs, axis=1)
                out = (_dot(y_ssd, wout_ref[0:D_SSD, lo:hi])
                       + _dot(y_pool, wout_ref[D_SSD:D_SSD + D_POOL, lo:hi]))
                o_ref[rs, lo:hi] = xc_ref[rs, lo:hi] + gt[:, lo:hi] * out
            return run

        return ([pool_piece(gi) for gi in range(len(POOL_WINDOWS))]
                + [out_piece(k, k + MXU_COLS) for k in range(0, D_MODEL, MXU_COLS)])

    def hand_over(vals):
        diffs, zs, dt_c = vals
        for gi in range(len(POOL_WINDOWS)):
            diff_s[:, gi * POOL_GROUP_DIM:(gi + 1) * POOL_GROUP_DIM] = diffs[gi]
        zs_s[...] = zs
        dtc_s[...] = dt_c

    @pl.when(step == 0)
    def _():
        hand_over(pre(0, tile_in_seq))

    diffs = [diff_s[:, gi * POOL_GROUP_DIM:(gi + 1) * POOL_GROUP_DIM] for gi in range(len(POOL_WINDOWS))]
    zs, dt_c = zs_s[...], dtc_s[...]
    h_n = in_norm(xn_ref[rows(0), :], modn_ref)
    pending.extend(proj_pieces(0, h_n))
    for c in range(n_chunks):
        y_ssd = ssd(c, zs, dt_c)
        pending.extend(post_pieces(c, y_ssd, diffs))
        if c + 1 < n_chunks:
            h_n = in_norm(xn_ref[rows(c + 1), :], modn_ref)
            emit(2)
            diffs, zs, dt_c = pre(c + 1, tile_in_seq)
            pending.extend(proj_pieces(c + 1, h_n))
        else:
            hand_over(pre(0, (step + 1) % tiles_per_seq))
            emit()


def _mixer(x, mod3, norm_w, w_in_t, convw, convb, dtb, alog, dskip, ssdnw,
           poolw, poolb, pools, w_out, *, layer):
    bsz, seq, _ = x.shape
    tiles_per_seq = seq // MIX_ROWS
    n_chunks = MIX_ROWS // CHUNK
    steps = bsz * tiles_per_seq
    nxt = lambda s: jnp.minimum(s + 1, steps - 1)
    hbm = pl.BlockSpec(memory_space=pl.ANY)
    vmem_consts = [convw, convb, dtb, alog, dskip, ssdnw, poolw, poolb, pools]
    x2 = x.reshape(bsz * seq, D_MODEL)
    out = pl.pallas_call(
        functools.partial(_mixer_kernel, layer=layer, tiles_per_seq=tiles_per_seq),
        grid=(steps,),
        in_specs=[
            pl.BlockSpec((MIX_ROWS, D_MODEL), lambda s: (s, 0)),
            pl.BlockSpec((MIX_ROWS, D_MODEL), lambda s: (nxt(s), 0)),
            pl.BlockSpec((1, N_MOD, D_MODEL), lambda s: (s // tiles_per_seq, 0, 0)),
            pl.BlockSpec((1, N_MOD, D_MODEL), lambda s: (nxt(s) // tiles_per_seq, 0, 0)),
            _const_spec(norm_w.shape), hbm,
        ] + [_const_spec(a.shape) for a in vmem_consts] + [hbm],
        out_specs=pl.BlockSpec((MIX_ROWS, D_MODEL), lambda s: (s, 0)),
        out_shape=jax.ShapeDtypeStruct((bsz * seq, D_MODEL), F32),
        scratch_shapes=[
            pltpu.VMEM((D_MODEL, D_SSD), BF16),
            pltpu.VMEM((D_MODEL, D_XBC), BF16),
            pltpu.VMEM((D_MODEL, LANES), BF16),
            pltpu.VMEM((D_MODEL, D_POOL), BF16),
            pltpu.VMEM(poolw.shape, BF16),
            pltpu.VMEM((D_SSD + D_POOL, D_MODEL), BF16),
            pltpu.VMEM((W_SLOTS, MIX_W_ROWS, D_MODEL), F32),
            pltpu.SemaphoreType.DMA((W_SLOTS,)),
            pltpu.VMEM((MIX_ROWS, D_SSD), F32),
            pltpu.VMEM((D_XBC // LANES, n_chunks * (CONV_PAD + CHUNK), LANES), F32),
            pltpu.VMEM((MIX_ROWS, LANES), F32),
            pltpu.VMEM((D_POOL // LANES, n_chunks * (POOL_PAD + CHUNK), LANES), F32),
            pltpu.VMEM((MIX_ROWS, D_XBC), F32),
            pltpu.VMEM((D_STATE, D_SSD), F32),
            pltpu.VMEM((CHUNK, D_POOL), BF16),
            pltpu.VMEM((CHUNK, D_SSD), F32),
            pltpu.VMEM((CHUNK, LANES), F32),
        ],
        compiler_params=pltpu.CompilerParams(
            dimension_semantics=("arbitrary",), vmem_limit_bytes=VMEM_LIMIT_BYTES),
        name="mixer",
    )(x2, x2, mod3, mod3, norm_w, w_in_t, *vmem_consts, w_out)
    return out.reshape(bsz, seq, D_MODEL)


def _pad_lanes(v):
    return jnp.pad(v.reshape(1, -1), ((0, 0), (0, LANES - v.shape[0])))


def kernel(x, c, w_ada, b_ada, ffn1_norm, ffn1_w_gate, ffn1_w_up, ffn1_w_down, mix_norm, w_in, conv_w, conv_b, dt_bias, a_log, d_skip, ssd_norm_w, pool_w, pool_b, pool_scale, w_out, ffn2_norm, ffn2_w_gate, ffn2_w_up, ffn2_w_down, final_norm):
    bsz = x.shape[0]
    depth = w_ada.shape[0]
    row = lambda v: v.reshape(1, -1)
    for i in range(depth):
        mod3 = _adaln(c, w_ada[i], row(b_ada[i])).reshape(bsz, N_MOD, D_MODEL)
        last = i == depth - 1

        x = _ffn(x, mod3, row(ffn1_norm[i]), ffn1_w_gate, ffn1_w_up, ffn1_w_down, row(final_norm),
                 layer=i, mod_idx=0, final=False)

        x = _mixer(
            x, mod3, row(mix_norm[i]), jnp.swapaxes(w_in, 1, 2),
            conv_w[i], row(conv_b[i]), _pad_lanes(dt_bias[i]), _pad_lanes(a_log[i]),
            row(jnp.repeat(d_skip[i], HEAD_DIM)), row(ssd_norm_w[i]),
            pool_w[i].astype(BF16), row(pool_b[i]), row(pool_scale[i]), w_out, layer=i)

        x = _ffn(x, mod3, row(ffn2_norm[i]), ffn2_w_gate, ffn2_w_up, ffn2_w_down, row(final_norm),
                 layer=i, mod_idx=6, final=last)
    return x
```

```python
import functools

import jax
import jax.numpy as jnp
from jax import lax
from jax.experimental import pallas as pl
from jax.experimental.pallas import tpu as pltpu

F32 = jnp.float32
BF16 = jnp.bfloat16

D_MODEL = 1024
D_FF = 2816
N_MOD = 9
FFN_RES = 0.5
EPS = 1e-6
LOG2_E = 1.4426950408889634
D_SSD = 1024
HEAD_DIM = 64
N_HEADS = 16
N_GROUPS = 4
HEADS_PER_GROUP = N_HEADS // N_GROUPS
D_STATE = 128
D_CONV = 4
CHUNK = 128
D_POOL = 1024
POOL_WINDOWS = (2, 4, 8, 16)
POOL_GROUP_DIM = 256
D_XBC = D_SSD + 2 * N_GROUPS * D_STATE
DT_COL = D_SSD + D_XBC

LANES = 128
SUBLANES = 8
MXU_COLS = 256
VMEM_LIMIT_BYTES = 60 * 1024 * 1024

FFN_ROWS = 512
NORM_SLICES = 8
FFN_W_ROWS_IN = 128
FFN_W_ROWS_OUT = 256
MIX_W_ROWS = 256
W_SLOTS = 8
MIX_ROWS = 512
ADA_COLS = 1024
CONV_PAD = SUBLANES
POOL_PAD = 16

assert HEAD_DIM * 2 == LANES and D_STATE == LANES and CHUNK == LANES


def _dot(a, b):
    return jnp.dot(a, b, preferred_element_type=F32)


def _split3(v):
    hi = v.astype(BF16)
    r1 = v - hi.astype(F32)
    mid = r1.astype(BF16)
    lo = (r1 - mid.astype(F32)).astype(BF16)
    return hi, mid, lo


def _silu_of_twice(half):
    return half + half * jnp.tanh(half)


def _silu(v):
    return _silu_of_twice(0.5 * v)


def _rms(v, w):
    ms = jnp.mean(v * v, axis=-1, keepdims=True)
    return v * lax.rsqrt(ms + EPS) * w


def _const_spec(shape):
    zeros = (0,) * len(shape)
    return pl.BlockSpec(shape, lambda *_: zeros, pipeline_mode=pl.Buffered(1))


def _adaln_kernel(c_ref, w_ref, b_ref, o_ref):
    a_hi, a_mid, _ = _split3(_silu(c_ref[...]))
    res = _dot(jnp.concatenate([a_hi, a_mid], axis=0), w_ref[...].astype(BF16))
    o_ref[...] = res[0:SUBLANES] + res[SUBLANES:2 * SUBLANES] + b_ref[...]


def _adaln(c, w_ada, b_ada):
    bsz = c.shape[0]
    n = w_ada.shape[1]
    c_pad = jnp.pad(c, ((0, SUBLANES - bsz), (0, 0)))
    mod = pl.pallas_call(
        _adaln_kernel,
        grid=(n // ADA_COLS,),
        in_specs=[
            pl.BlockSpec((SUBLANES, D_MODEL), lambda j: (0, 0)),
            pl.BlockSpec((D_MODEL, ADA_COLS), lambda j: (0, j)),
            pl.BlockSpec((1, ADA_COLS), lambda j: (0, j)),
        ],
        out_specs=pl.BlockSpec((SUBLANES, ADA_COLS), lambda j: (0, j)),
        out_shape=jax.ShapeDtypeStruct((SUBLANES, n), F32),
        compiler_params=pltpu.CompilerParams(
            dimension_semantics=("arbitrary",), vmem_limit_bytes=VMEM_LIMIT_BYTES),
        name="adaln",
    )(c_pad, w_ada, b_ada)
    return mod[:bsz]


class _BlockStream:
    def __init__(self, stage_ref, sem):
        self.stage, self.sem, self.blocks = stage_ref, sem, []
        self.n_slots = stage_ref.shape[0]

    def add(self, src, sink):
        self.blocks.append((src, sink))

    def _slot(self, i):
        return self.stage.at[i % self.n_slots, pl.ds(0, self.blocks[i][0].shape[0])]

    def _copy(self, i):
        return pltpu.make_async_copy(self.blocks[i][0], self._slot(i), self.sem.at[i % self.n_slots])

    def prime(self):
        for i in range(min(self.n_slots, len(self.blocks))):
            self._copy(i).start()

    def drain(self):
        for i in range(len(self.blocks)):
            self._copy(i).wait()
            self.blocks[i][1](self._slot(i)[...])
            if i + self.n_slots < len(self.blocks):
                self._copy(i + self.n_slots).start()


def _stream_rows_bf16(stream, w_hbm, layer, dst_ref, scale=1.0):
    rows = stream.stage.shape[1]

    def sink(r):
        def run(block):
            dst_ref[r:r + rows, :] = (block if scale == 1.0 else scale * block).astype(BF16)
        return run

    for r in range(0, dst_ref.shape[0], rows):
        stream.add(w_hbm.at[layer, pl.ds(r, rows), :], sink(r))


def _ffn_kernel(xc_ref, xn_ref, modc_ref, modn_ref, nw_ref, wg_hbm, wu_hbm, wd_hbm, fw_ref, o_ref,
                h_s, wg_ref, wu_ref, wd_ref, stage_in, stage_out, sem_in, sem_out,
                *, layer, mod_idx, final):
    def in_norm(x, mod_ref):
        sh = mod_ref[0, pl.ds(mod_idx, 1), :]
        sc = mod_ref[0, pl.ds(mod_idx + 1, 1), :]
        gain = nw_ref[...] * (1.0 + sc)
        ms = jnp.mean(x * x, axis=-1, keepdims=True)
        return (x * lax.rsqrt(ms + EPS) * gain + sh).astype(BF16)

    step = pl.program_id(0)
    cur = h_s.at[step % 2]
    nxt = h_s.at[(step + 1) % 2]

    @pl.when(step == 0)
    def _():
        s_in, s_out = _BlockStream(stage_in, sem_in), _BlockStream(stage_out, sem_out)
        _stream_rows_bf16(s_in, wg_hbm, layer, wg_ref, scale=0.5)
        _stream_rows_bf16(s_in, wu_hbm, layer, wu_ref)
        _stream_rows_bf16(s_out, wd_hbm, layer, wd_ref)
        s_in.prime()
        s_out.prime()
        s_in.drain()
        s_out.drain()
        cur[...] = in_norm(xc_ref[...], modc_ref)

    n_chunks = D_FF // MXU_COLS
    slice_rows = FFN_ROWS // NORM_SLICES
    acts = []
    for k in range(n_chunks):
        if k < NORM_SLICES:
            rs = slice(k * slice_rows, (k + 1) * slice_rows)
            nxt[rs, :] = in_norm(xn_ref[rs, :], modn_ref)
        cols = slice(k * MXU_COLS, (k + 1) * MXU_COLS)
        h = cur[...]
        g_half = _dot(h, wg_ref[:, cols])
        u = _dot(h, wu_ref[:, cols])
        acts.append((_silu_of_twice(g_half) * u).astype(BF16))
    a = jnp.concatenate(acts, axis=1)
    o = _dot(a, wd_ref[...])
    gt = modc_ref[0, pl.ds(mod_idx + 2, 1), :]
    out = xc_ref[...] + (FFN_RES * gt) * o
    if final:
        out = _rms(out, fw_ref[...])
    o_ref[...] = out


def _ffn(x, mod3, norm_w, wg, wu, wd, final_w, *, layer, mod_idx, final):
    bsz, seq, _ = x.shape
    tiles_per_seq = seq // FFN_ROWS
    steps = bsz * tiles_per_seq
    nxt = lambda s: jnp.minimum(s + 1, steps - 1)
    x2 = x.reshape(bsz * seq, D_MODEL)
    hbm = pl.BlockSpec(memory_space=pl.ANY)
    out = pl.pallas_call(
        functools.partial(_ffn_kernel, layer=layer, mod_idx=mod_idx, final=final),
        grid=(steps,),
        in_specs=[
            pl.BlockSpec((FFN_ROWS, D_MODEL), lambda s: (s, 0)),
            pl.BlockSpec((FFN_ROWS, D_MODEL), lambda s: (nxt(s), 0)),
            pl.BlockSpec((1, N_MOD, D_MODEL), lambda s: (s // tiles_per_seq, 0, 0)),
            pl.BlockSpec((1, N_MOD, D_MODEL), lambda s: (nxt(s) // tiles_per_seq, 0, 0)),
            _const_spec((1, D_MODEL)),
            hbm, hbm, hbm,
            _const_spec((1, D_MODEL)),
        ],
        out_specs=pl.BlockSpec((FFN_ROWS, D_MODEL), lambda s: (s, 0)),
        out_shape=jax.ShapeDtypeStruct((bsz * seq, D_MODEL), F32),
        scratch_shapes=[
            pltpu.VMEM((2, FFN_ROWS, D_MODEL), BF16),
            pltpu.VMEM((D_MODEL, D_FF), BF16),
            pltpu.VMEM((D_MODEL, D_FF), BF16),
            pltpu.VMEM((D_FF, D_MODEL), BF16),
            pltpu.VMEM((W_SLOTS, FFN_W_ROWS_IN, D_FF), F32),
            pltpu.VMEM((W_SLOTS, FFN_W_ROWS_OUT, D_MODEL), F32),
            pltpu.SemaphoreType.DMA((W_SLOTS,)),
            pltpu.SemaphoreType.DMA((W_SLOTS,)),
        ],
        compiler_params=pltpu.CompilerParams(
            dimension_semantics=("arbitrary",), vmem_limit_bytes=VMEM_LIMIT_BYTES),
        name="ffn_final" if final else "ffn",
    )(x2, x2, mod3, mod3, norm_w, wg, wu, wd, final_w)
    return out.reshape(bsz, seq, D_MODEL)


def _mixer_kernel(xc_ref, xn_ref, modc_ref, modn_ref, nw_ref, wint_hbm,
                  convw_ref, convb_ref, dtb_ref, alog_ref, dskip_ref, ssdnw_ref,
                  poolw_in, poolb_ref, pools_ref, wout_hbm,
                  o_ref,
                  wz_ref, wxbc_ref, wdt_ref, wu_ref, poolw_ref, wout_ref, stage, sem,
                  z_s, raw_s, dtr_s, u_s, xbc_s, st_s, diff_s, zs_s, dtc_s, *, layer, tiles_per_seq):
    step = pl.program_id(0)
    tile_in_seq = step % tiles_per_seq
    next_starts_seq = tile_in_seq == tiles_per_seq - 1
    n_chunks = MIX_ROWS // CHUNK

    def in_norm(x, mod_ref):
        sh = mod_ref[0, 3:4, :]
        sc = mod_ref[0, 4:5, :]
        gain = nw_ref[...] * (1.0 + sc)
        ms = jnp.mean(x * x, axis=-1, keepdims=True)
        return (x * lax.rsqrt(ms + EPS) * gain + sh).astype(BF16)

    def rows(c):
        return slice(c * CHUNK, (c + 1) * CHUNK)

    def conv_base(c):
        return (c % n_chunks) * (CONV_PAD + CHUNK) + CONV_PAD

    def pool_base(c):
        return (c % n_chunks) * (POOL_PAD + CHUNK) + POOL_PAD

    pending = []

    def emit(k=None):
        n = len(pending) if k is None else min(k, len(pending))
        for _ in range(n):
            pending.pop(0)()

    def proj_pieces(c, h_n):
        rs = rows(c)

        def dense(dst, w_ref, lo, hi):
            def run():
                dst[rs, lo:hi] = _dot(h_n, w_ref[:, lo:hi])
            return run

        def slabs(dst, base, w_ref, lo, hi):
            def run():
                res = _dot(h_n, w_ref[:, lo:hi])
                for j in range(lo // LANES, hi // LANES):
                    dst[j, base:base + CHUNK, :] = res[:, j * LANES - lo:(j + 1) * LANES - lo]
            return run

        out = [slabs(raw_s, conv_base(c), wxbc_ref, k, k + MXU_COLS) for k in range(0, D_XBC, MXU_COLS)]
        out += [slabs(u_s, pool_base(c), wu_ref, k, k + MXU_COLS) for k in range(0, D_POOL, MXU_COLS)]
        out += [dense(z_s, wz_ref, k, k + MXU_COLS) for k in range(0, D_SSD, MXU_COLS)]
        out.append(dense(dtr_s, wdt_ref, 0, LANES))
        return out

    @pl.when(step == 0)
    def _():
        stream = _BlockStream(stage, sem)
        w_rows, w_cols = stage.shape[1], stage.shape[2]

        def cast_into(dst, r, c0):
            def run(block):
                dst[r:r + w_rows, c0:c0 + w_cols] = block.astype(BF16)
            return run

        def transposed_into(dst, c0, scale):
            def run(block):
                dst[:, c0:c0 + block.shape[0]] = (block if scale == 1.0 else scale * block).T.astype(BF16)
            return run

        def dt_into(block):
            lane = lax.broadcasted_iota(jnp.int32, (D_MODEL, LANES), 1)
            wdt_ref[...] = jnp.where(lane < N_HEADS, block.T, 0.0).astype(BF16)

        def feature_rows(f0):
            return lambda r, n: wint_hbm.at[layer, pl.ds(f0 + r, n), :]

        for src, dst, width, scale in ((feature_rows(0), wz_ref, D_SSD, 0.5),
                                       (feature_rows(D_SSD), wxbc_ref, D_XBC, 1.0),
                                       (feature_rows(DT_COL + N_HEADS), wu_ref, D_POOL, 1.0)):
            for r in range(0, width, w_rows):
                stream.add(src(r, w_rows), transposed_into(dst, r, scale))
        stream.add(feature_rows(DT_COL)(0, LANES), dt_into)
        for r in range(0, D_SSD + D_POOL, w_rows):
            stream.add(wout_hbm.at[layer, pl.ds(r, w_rows), :], cast_into(wout_ref, r, 0))
        stream.prime()
        poolw_ref[...] = poolw_in[...]
        stream.drain()
        raw_s[:, 0:CONV_PAD, :] = jnp.zeros((D_XBC // LANES, CONV_PAD, LANES), F32)
        u_s[:, 0:POOL_PAD, :] = jnp.zeros((D_POOL // LANES, POOL_PAD, LANES), F32)
        for c in range(n_chunks):
            for run in proj_pieces(c, in_norm(xc_ref[rows(c), :], modc_ref)):
                run()

    @pl.when(tile_in_seq == 0)
    def _():
        st_s[...] = jnp.zeros(st_s.shape, F32)

    gt = modc_ref[0, 5:6, :]
    a_neg = -jnp.exp(alog_ref[...]) * LOG2_E
    conv_w_half = 0.5 * convw_ref[...]
    conv_b_half = 0.5 * convb_ref[...]
    ri = lax.broadcasted_iota(jnp.int32, (CHUNK, CHUNK), 0)
    ci = lax.broadcasted_iota(jnp.int32, (CHUNK, CHUNK), 1)
    causal = ri >= ci
    tril = jnp.where(causal, 1.0, 0.0).astype(BF16)
    low_half = ci < HEAD_DIM
    low_half_row = lax.broadcasted_iota(jnp.int32, (1, LANES), 1) < HEAD_DIM

    def pre(c, seq_tile):
        rs = rows(c)
        last = c == n_chunks - 1

        def carry(hist):
            return jnp.where(next_starts_seq, jnp.zeros_like(hist), hist) if last else hist

        base, nbase = conv_base(c), conv_base(c + 1)
        for j in range(D_XBC // LANES):
            cols = slice(j * LANES, (j + 1) * LANES)
            xc_half = conv_b_half[:, cols]
            for k in range(D_CONV):
                xc_half = xc_half + conv_w_half[k:k + 1, cols] * raw_s[j, pl.ds(base - (D_CONV - 1) + k, CHUNK), :]
            raw_s[j, nbase - CONV_PAD:nbase, :] = carry(raw_s[j, base + CHUNK - CONV_PAD:base + CHUNK, :])
            xbc_s[rs, cols] = _silu_of_twice(xc_half)
            if j % 2 == 1:
                emit(1)
        base, nbase = pool_base(c), pool_base(c + 1)
        diffs = []
        for gi, w in enumerate(POOL_WINDOWS):
            halves = []
            for j in range(gi * POOL_GROUP_DIM // LANES, (gi + 1) * POOL_GROUP_DIM // LANES):
                u_g = u_s[j, base:base + CHUNK, :]
                s = u_g
                for k in range(1, w):
                    s = s + u_s[j, pl.ds(base - k, CHUNK), :]
                if c == 0:
                    pos = seq_tile * MIX_ROWS + 1 + lax.broadcasted_iota(jnp.int32, (CHUNK, LANES), 0)
                    pooled = s / jnp.minimum(pos, w).astype(F32)
                else:
                    pooled = s * (1.0 / w)
                halves.append(pooled - u_g)
                u_s[j, nbase - POOL_PAD:nbase, :] = carry(u_s[j, base + CHUNK - POOL_PAD:base + CHUNK, :])
            diffs.append(jnp.concatenate(halves, axis=1).astype(BF16))
            emit(1)
        zs = _silu_of_twice(z_s[rs, :])
        dt_c = jax.nn.softplus(dtr_s[rs, :] + dtb_ref[...])
        emit()
        return diffs, zs, dt_c

    def ssd(c, zs, dt_c):
        rs = rows(c)
        p_hi, p_mid, p_lo = _split3(dt_c * a_neg)
        cs = _dot(tril, p_hi) + _dot(tril, p_mid) + _dot(tril, p_lo)
        cs_t = cs.T
        dt_t = dt_c.T
        w_t = dt_t * jnp.exp2(cs_t[:, CHUNK - 1:CHUNK] - cs_t)
        cs_dt_t = cs_t - jnp.log2(dt_t)
        xs_c = xbc_s[rs, 0:D_SSD]
        emit(1)
        y_pairs = []
        for g in range(N_GROUPS):
            b0 = D_SSD + g * D_STATE
            c0 = D_SSD + N_GROUPS * D_STATE + g * D_STATE
            b_t = xbc_s[rs, b0:b0 + D_STATE].T
            c_g = xbc_s[rs, c0:c0 + D_STATE]
            cb = _dot(c_g.astype(BF16), b_t.astype(BF16))
            for pr in range(HEADS_PER_GROUP // 2):
                lhs, bw_t, cd = [], [], []
                for r in (2 * pr, 2 * pr + 1):
                    hd = g * HEADS_PER_GROUP + r
                    cs_bc = jnp.broadcast_to(cs[:, hd:hd + 1], (CHUNK, CHUNK))
                    decay_dt = jnp.exp2(jnp.where(causal, cs_bc - cs_dt_t[hd:hd + 1, :], -jnp.inf))
                    scores = decay_dt * cb
                    sd = jnp.exp2(cs_bc)
                    lhs.append(jnp.concatenate([scores, c_g * sd], axis=1).astype(BF16))
                    bw_t.append((b_t * w_t[hd:hd + 1, :]).astype(BF16))
                    cd.append(sd[CHUNK - 1:CHUNK, :])
                col = (g * HEADS_PER_GROUP + 2 * pr) * HEAD_DIM
                x_p = xs_c[:, col:col + LANES]
                st = st_s[:, col:col + LANES]
                rhs = jnp.concatenate([x_p, st], axis=0).astype(BF16)
                y_pairs.append(jnp.where(low_half, _dot(lhs[0], rhs), _dot(lhs[1], rhs)))
                zero = jnp.zeros_like(x_p)
                x_split = jnp.concatenate(
                    [jnp.where(low_half, x_p, zero), jnp.where(low_half, zero, x_p)], axis=0)
                upd = _dot(jnp.concatenate(bw_t, axis=1), x_split.astype(BF16))
                st_s[:, col:col + LANES] = st * jnp.where(low_half_row, cd[0], cd[1]) + upd
                emit(1)
        y = jnp.concatenate(y_pairs, axis=1) + dskip_ref[...] * xs_c
        yz = y * zs
        gw = D_SSD // N_GROUPS
        parts = []
        for g in range(N_GROUPS):
            blk = yz[:, g * gw:(g + 1) * gw]
            ms = jnp.mean(blk * blk, axis=-1, keepdims=True)
            parts.append(blk * lax.rsqrt(ms + EPS))
        return (jnp.concatenate(parts, axis=1) * ssdnw_ref[...]).astype(BF16)

    def post_pieces(c, y_ssd, diffs):
        rs = rows(c)
        outs = [None] * len(POOL_WINDOWS)

        def pool_piece(gi):
            def run():
                lo, hi = gi * POOL_GROUP_DIM, (gi + 1) * POOL_GROUP_DIM
                o = (_dot(diffs[gi], poolw_ref[gi]) + poolb_ref[:, lo:hi]) * pools_ref[:, lo:hi]
                outs[gi] = o.astype(BF16)
            return run

        def out_piece(lo, hi):
            def run():
                y_pool = jnp.concatenate(outs, axis=1)
                out = (_dot(y_ssd, wout_ref[0:D_SSD, lo:hi])
                       + _dot(y_pool, wout_ref[D_SSD:D_SSD + D_POOL, lo:hi]))
                o_ref[rs, lo:hi] = xc_ref[rs, lo:hi] + gt[:, lo:hi] * out
            return run

        return ([pool_piece(gi) for gi in range(len(POOL_WINDOWS))]
                + [out_piece(k, k + MXU_COLS) for k in range(0, D_MODEL, MXU_COLS)])

    def hand_over(vals):
        diffs, zs, dt_c = vals
        for gi in range(len(POOL_WINDOWS)):
            diff_s[:, gi * POOL_GROUP_DIM:(gi + 1) * POOL_GROUP_DIM] = diffs[gi]
        zs_s[...] = zs
        dtc_s[...] = dt_c

    @pl.when(step == 0)
    def _():
        hand_over(pre(0, tile_in_seq))

    diffs = [diff_s[:, gi * POOL_GROUP_DIM:(gi + 1) * POOL_GROUP_DIM] for gi in range(len(POOL_WINDOWS))]
    zs, dt_c = zs_s[...], dtc_s[...]
    h_n = in_norm(xn_ref[rows(0), :], modn_ref)
    pending.extend(proj_pieces(0, h_n))
    for c in range(n_chunks):
        y_ssd = ssd(c, zs, dt_c)
        pending.extend(post_pieces(c, y_ssd, diffs))
        if c + 1 < n_chunks:
            h_n = in_norm(xn_ref[rows(c + 1), :], modn_ref)
            emit(2)
            diffs, zs, dt_c = pre(c + 1, tile_in_seq)
            pending.extend(proj_pieces(c + 1, h_n))
        else:
            hand_over(pre(0, (step + 1) % tiles_per_seq))
            emit()


def _mixer(x, mod3, norm_w, w_in_t, convw, convb, dtb, alog, dskip, ssdnw,
           poolw, poolb, pools, w_out, *, layer):
    bsz, seq, _ = x.shape
    tiles_per_seq = seq // MIX_ROWS
    n_chunks = MIX_ROWS // CHUNK
    steps = bsz * tiles_per_seq
    nxt = lambda s: jnp.minimum(s + 1, steps - 1)
    hbm = pl.BlockSpec(memory_space=pl.ANY)
    vmem_consts = [convw, convb, dtb, alog, dskip, ssdnw, poolw, poolb, pools]
    x2 = x.reshape(bsz * seq, D_MODEL)
    out = pl.pallas_call(
        functools.partial(_mixer_kernel, layer=layer, tiles_per_seq=tiles_per_seq),
        grid=(steps,),
        in_specs=[
            pl.BlockSpec((MIX_ROWS, D_MODEL), lambda s: (s, 0)),
            pl.BlockSpec((MIX_ROWS, D_MODEL), lambda s: (nxt(s), 0)),
            pl.BlockSpec((1, N_MOD, D_MODEL), lambda s: (s // tiles_per_seq, 0, 0)),
            pl.BlockSpec((1, N_MOD, D_MODEL), lambda s: (nxt(s) // tiles_per_seq, 0, 0)),
            _const_spec(norm_w.shape), hbm,
        ] + [_const_spec(a.shape) for a in vmem_consts] + [hbm],
        out_specs=pl.BlockSpec((MIX_ROWS, D_MODEL), lambda s: (s, 0)),
        out_shape=jax.ShapeDtypeStruct((bsz * seq, D_MODEL), F32),
        scratch_shapes=[
            pltpu.VMEM((D_MODEL, D_SSD), BF16),
            pltpu.VMEM((D_MODEL, D_XBC), BF16),
            pltpu.VMEM((D_MODEL, LANES), BF16),
            pltpu.VMEM((D_MODEL, D_POOL), BF16),
            pltpu.VMEM(poolw.shape, BF16),
            pltpu.VMEM((D_SSD + D_POOL, D_MODEL), BF16),
            pltpu.VMEM((W_SLOTS, MIX_W_ROWS, D_MODEL), F32),
            pltpu.SemaphoreType.DMA((W_SLOTS,)),
            pltpu.VMEM((MIX_ROWS, D_SSD), F32),
            pltpu.VMEM((D_XBC // LANES, n_chunks * (CONV_PAD + CHUNK), LANES), F32),
            pltpu.VMEM((MIX_ROWS, LANES), F32),
            pltpu.VMEM((D_POOL // LANES, n_chunks * (POOL_PAD + CHUNK), LANES), F32),
            pltpu.VMEM((MIX_ROWS, D_XBC), F32),
            pltpu.VMEM((D_STATE, D_SSD), F32),
            pltpu.VMEM((CHUNK, D_POOL), BF16),
            pltpu.VMEM((CHUNK, D_SSD), F32),
            pltpu.VMEM((CHUNK, LANES), F32),
        ],
        compiler_params=pltpu.CompilerParams(
            dimension_semantics=("arbitrary",), vmem_limit_bytes=VMEM_LIMIT_BYTES),
        name="mixer",
    )(x2, x2, mod3, mod3, norm_w, w_in_t, *vmem_consts, w_out)
    return out.reshape(bsz, seq, D_MODEL)


def _pad_lanes(v):
    return jnp.pad(v.reshape(1, -1), ((0, 0), (0, LANES - v.shape[0])))


def kernel(x, c, w_ada, b_ada, ffn1_norm, ffn1_w_gate, ffn1_w_up, ffn1_w_down, mix_norm, w_in, conv_w, conv_b, dt_bias, a_log, d_skip, ssd_norm_w, pool_w, pool_b, pool_scale, w_out, ffn2_norm, ffn2_w_gate, ffn2_w_up, ffn2_w_down, final_norm):
    bsz = x.shape[0]
    depth = w_ada.shape[0]
    row = lambda v: v.reshape(1, -1)
    for i in range(depth):
        mod3 = _adaln(c, w_ada[i], row(b_ada[i])).reshape(bsz, N_MOD, D_MODEL)
        last = i == depth - 1

        x = _ffn(x, mod3, row(ffn1_norm[i]), ffn1_w_gate, ffn1_w_up, ffn1_w_down, row(final_norm),
                 layer=i, mod_idx=0, final=False)

        x = _mixer(
            x, mod3, row(mix_norm[i]), jnp.swapaxes(w_in, 1, 2),
            conv_w[i], row(conv_b[i]), _pad_lanes(dt_bias[i]), _pad_lanes(a_log[i]),
            row(jnp.repeat(d_skip[i], HEAD_DIM)), row(ssd_norm_w[i]),
            pool_w[i].astype(BF16), row(pool_b[i]), row(pool_scale[i]), w_out, layer=i)

        x = _ffn(x, mod3, row(ffn2_norm[i]), ffn2_w_gate, ffn2_w_up, ffn2_w_down, row(final_norm),
                 layer=i, mod_idx=6, final=last)
    return x
```

```python
import functools

import jax
import jax.numpy as jnp
from jax import lax
from jax.experimental import pallas as pl
from jax.experimental.pallas import tpu as pltpu

F32 = jnp.float32
BF16 = jnp.bfloat16

D_MODEL = 1024
D_FF = 2816
N_MOD = 9
FFN_RES = 0.5
EPS = 1e-6
LOG2_E = 1.4426950408889634
D_SSD = 1024
HEAD_DIM = 64
N_HEADS = 16
N_GROUPS = 4
HEADS_PER_GROUP = N_HEADS // N_GROUPS
D_STATE = 128
D_CONV = 4
CHUNK = 128
D_POOL = 1024
POOL_WINDOWS = (2, 4, 8, 16)
POOL_GROUP_DIM = 256
D_XBC = D_SSD + 2 * N_GROUPS * D_STATE
DT_COL = D_SSD + D_XBC

LANES = 128
SUBLANES = 8
MXU_COLS = 256
VMEM_LIMIT_BYTES = 60 * 1024 * 1024

FFN_ROWS = 512
NORM_SLICES = 8
FFN_W_ROWS_IN = 128
FFN_W_ROWS_OUT = 256
MIX_W_ROWS = 256
W_SLOTS = 8
MIX_ROWS = 512
ADA_COLS = 1024
CONV_PAD = SUBLANES
POOL_PAD = 16

assert HEAD_DIM * 2 == LANES and D_STATE == LANES and CHUNK == LANES


def _dot(a, b):
    return jnp.dot(a, b, preferred_element_type=F32)


def _split3(v):
    hi = v.astype(BF16)
    r1 = v - hi.astype(F32)
    mid = r1.astype(BF16)
    lo = (r1 - mid.astype(F32)).astype(BF16)
    return hi, mid, lo


def _silu_of_twice(half):
    return half + half * jnp.tanh(half)


def _silu(v):
    return _silu_of_twice(0.5 * v)


def _rms(v, w):
    ms = jnp.mean(v * v, axis=-1, keepdims=True)
    return v * lax.rsqrt(ms + EPS) * w


def _const_spec(shape):
    zeros = (0,) * len(shape)
    return pl.BlockSpec(shape, lambda *_: zeros, pipeline_mode=pl.Buffered(1))


def _adaln_kernel(c_ref, w_ref, b_ref, o_ref):
    a_hi, a_mid, _ = _split3(_silu(c_ref[...]))
    res = _dot(jnp.concatenate([a_hi, a_mid], axis=0), w_ref[...].astype(BF16))
    o_ref[...] = res[0:SUBLANES] + res[SUBLANES:2 * SUBLANES] + b_ref[...]


def _adaln(c, w_ada, b_ada):
    bsz = c.shape[0]
    n = w_ada.shape[1]
    c_pad = jnp.pad(c, ((0, SUBLANES - bsz), (0, 0)))
    mod = pl.pallas_call(
        _adaln_kernel,
        grid=(n // ADA_COLS,),
        in_specs=[
            pl.BlockSpec((SUBLANES, D_MODEL), lambda j: (0, 0)),
            pl.BlockSpec((D_MODEL, ADA_COLS), lambda j: (0, j)),
            pl.BlockSpec((1, ADA_COLS), lambda j: (0, j)),
        ],
        out_specs=pl.BlockSpec((SUBLANES, ADA_COLS), lambda j: (0, j)),
        out_shape=jax.ShapeDtypeStruct((SUBLANES, n), F32),
        compiler_params=pltpu.CompilerParams(
            dimension_semantics=("arbitrary",), vmem_limit_bytes=VMEM_LIMIT_BYTES),
        name="adaln",
    )(c_pad, w_ada, b_ada)
    return mod[:bsz]


class _BlockStream:
    def __init__(self, stage_ref, sem):
        self.stage, self.sem, self.blocks = stage_ref, sem, []
        self.n_slots = stage_ref.shape[0]

    def add(self, src, sink):
        self.blocks.append((src, sink))

    def _slot(self, i):
        return self.stage.at[i % self.n_slots, pl.ds(0, self.blocks[i][0].shape[0])]

    def _copy(self, i):
        return pltpu.make_async_copy(self.blocks[i][0], self._slot(i), self.sem.at[i % self.n_slots])

    def prime(self):
        for i in range(min(self.n_slots, len(self.blocks))):
            self._copy(i).start()

    def drain(self):
        for i in range(len(self.blocks)):
            self._copy(i).wait()
            self.blocks[i][1](self._slot(i)[...])
            if i + self.n_slots < len(self.blocks):
                self._copy(i + self.n_slots).start()


def _stream_rows_bf16(stream, w_hbm, layer, dst_ref, scale=1.0):
    rows = stream.stage.shape[1]

    def sink(r):
        def run(block):
            dst_ref[r:r + rows, :] = (block if scale == 1.0 else scale * block).astype(BF16)
        return run

    for r in range(0, dst_ref.shape[0], rows):
        stream.add(w_hbm.at[layer, pl.ds(r, rows), :], sink(r))


def _ffn_kernel(xc_ref, xn_ref, modc_ref, modn_ref, nw_ref, wg_hbm, wu_hbm, wd_hbm, fw_ref, o_ref,
                h_s, wg_ref, wu_ref, wd_ref, stage_in, stage_out, sem_in, sem_out,
                *, layer, mod_idx, final):
    def in_norm(x, mod_ref):
        sh = mod_ref[0, pl.ds(mod_idx, 1), :]
        sc = mod_ref[0, pl.ds(mod_idx + 1, 1), :]
        gain = nw_ref[...] * (1.0 + sc)
        ms = jnp.mean(x * x, axis=-1, keepdims=True)
        return (x * lax.rsqrt(ms + EPS) * gain + sh).astype(BF16)

    step = pl.program_id(0)
    cur = h_s.at[step % 2]
    nxt = h_s.at[(step + 1) % 2]

    @pl.when(step == 0)
    def _():
        s_in, s_out = _BlockStream(stage_in, sem_in), _BlockStream(stage_out, sem_out)
        _stream_rows_bf16(s_in, wg_hbm, layer, wg_ref, scale=0.5)
        _stream_rows_bf16(s_in, wu_hbm, layer, wu_ref)
        _stream_rows_bf16(s_out, wd_hbm, layer, wd_ref)
        s_in.prime()
        s_out.prime()
        s_in.drain()
        s_out.drain()
        cur[...] = in_norm(xc_ref[...], modc_ref)

    n_chunks = D_FF // MXU_COLS
    slice_rows = FFN_ROWS // NORM_SLICES
    acts = []
    for k in range(n_chunks):
        if k < NORM_SLICES:
            rs = slice(k * slice_rows, (k + 1) * slice_rows)
            nxt[rs, :] = in_norm(xn_ref[rs, :], modn_ref)
        cols = slice(k * MXU_COLS, (k + 1) * MXU_COLS)
        h = cur[...]
        g_half = _dot(h, wg_ref[:, cols])
        u = _dot(h, wu_ref[:, cols])
        acts.append((_silu_of_twice(g_half) * u).astype(BF16))
    a = jnp.concatenate(acts, axis=1)
    o = _dot(a, wd_ref[...])
    gt = modc_ref[0, pl.ds(mod_idx + 2, 1), :]
    out = xc_ref[...] + (FFN_RES * gt) * o
    if final:
        out = _rms(out, fw_ref[...])
    o_ref[...] = out


def _ffn(x, mod3, norm_w, wg, wu, wd, final_w, *, layer, mod_idx, final):
    bsz, seq, _ = x.shape
    tiles_per_seq = seq // FFN_ROWS
    steps = bsz * tiles_per_seq
    nxt = lambda s: jnp.minimum(s + 1, steps - 1)
    x2 = x.reshape(bsz * seq, D_MODEL)
    hbm = pl.BlockSpec(memory_space=pl.ANY)
    out = pl.pallas_call(
        functools.partial(_ffn_kernel, layer=layer, mod_idx=mod_idx, final=final),
        grid=(steps,),
        in_specs=[
            pl.BlockSpec((FFN_ROWS, D_MODEL), lambda s: (s, 0)),
            pl.BlockSpec((FFN_ROWS, D_MODEL), lambda s: (nxt(s), 0)),
            pl.BlockSpec((1, N_MOD, D_MODEL), lambda s: (s // tiles_per_seq, 0, 0)),
            pl.BlockSpec((1, N_MOD, D_MODEL), lambda s: (nxt(s) // tiles_per_seq, 0, 0)),
            _const_spec((1, D_MODEL)),
            hbm, hbm, hbm,
            _const_spec((1, D_MODEL)),
        ],
        out_specs=pl.BlockSpec((FFN_ROWS, D_MODEL), lambda s: (s, 0)),
        out_shape=jax.ShapeDtypeStruct((bsz * seq, D_MODEL), F32),
        scratch_shapes=[
            pltpu.VMEM((2, FFN_ROWS, D_MODEL), BF16),
            pltpu.VMEM((D_MODEL, D_FF), BF16),
            pltpu.VMEM((D_MODEL, D_FF), BF16),
            pltpu.VMEM((D_FF, D_MODEL), BF16),
            pltpu.VMEM((W_SLOTS, FFN_W_ROWS_IN, D_FF), F32),
            pltpu.VMEM((W_SLOTS, FFN_W_ROWS_OUT, D_MODEL), F32),
            pltpu.SemaphoreType.DMA((W_SLOTS,)),
            pltpu.SemaphoreType.DMA((W_SLOTS,)),
        ],
        compiler_params=pltpu.CompilerParams(
            dimension_semantics=("arbitrary",), vmem_limit_bytes=VMEM_LIMIT_BYTES),
        name="ffn_final" if final else "ffn",
    )(x2, x2, mod3, mod3, norm_w, wg, wu, wd, final_w)
    return out.reshape(bsz, seq, D_MODEL)


def _mixer_kernel(xc_ref, xn_ref, modc_ref, modn_ref, nw_ref, wint_hbm,
                  convw_ref, convb_ref, dtb_ref, alog_ref, dskip_ref, ssdnw_ref,
                  poolw_in, poolb_ref, pools_ref, wout_hbm,
                  o_ref,
                  wz_ref, wxbc_ref, wdt_ref, wu_ref, pool_out_b, wout_ref, stage, sem,
                  z_s, raw_s, dtr_s, u_s, xbc_s, st_s, diff_s, zs_s, dtc_s, *, layer, tiles_per_seq):
    step = pl.program_id(0)
    tile_in_seq = step % tiles_per_seq
    next_starts_seq = tile_in_seq == tiles_per_seq - 1
    n_chunks = MIX_ROWS // CHUNK

    def in_norm(x, mod_ref):
        sh = mod_ref[0, 3:4, :]
        sc = mod_ref[0, 4:5, :]
        gain = nw_ref[...] * (1.0 + sc)
        ms = jnp.mean(x * x, axis=-1, keepdims=True)
        return (x * lax.rsqrt(ms + EPS) * gain + sh).astype(BF16)

    def rows(c):
        return slice(c * CHUNK, (c + 1) * CHUNK)

    def conv_base(c):
        return (c % n_chunks) * (CONV_PAD + CHUNK) + CONV_PAD

    def pool_base(c):
        return (c % n_chunks) * (POOL_PAD + CHUNK) + POOL_PAD

    pending = []

    def emit(k=None):
        n = len(pending) if k is None else min(k, len(pending))
        for _ in range(n):
            pending.pop(0)()

    def proj_pieces(c, h_n):
        rs = rows(c)

        def dense(dst, w_ref, lo, hi):
            def run():
                dst[rs, lo:hi] = _dot(h_n, w_ref[:, lo:hi])
            return run

        def slabs(dst, base, w_ref, lo, hi):
            def run():
                res = _dot(h_n, w_ref[:, lo:hi])
                for j in range(lo // LANES, hi // LANES):
                    dst[j, base:base + CHUNK, :] = res[:, j * LANES - lo:(j + 1) * LANES - lo]
            return run

        out = [slabs(raw_s, conv_base(c), wxbc_ref, k, k + MXU_COLS) for k in range(0, D_XBC, MXU_COLS)]
        out += [slabs(u_s, pool_base(c), wu_ref, k, k + MXU_COLS) for k in range(0, D_POOL, MXU_COLS)]
        out += [dense(z_s, wz_ref, k, k + MXU_COLS) for k in range(0, D_SSD, MXU_COLS)]
        out.append(dense(dtr_s, wdt_ref, 0, LANES))
        return out

    @pl.when(step == 0)
    def _():
        stream = _BlockStream(stage, sem)
        w_rows, w_cols = stage.shape[1], stage.shape[2]

        def cast_into(dst, r, c0):
            def run(block):
                dst[r:r + w_rows, c0:c0 + w_cols] = block.astype(BF16)
            return run

        def transposed_into(dst, c0, scale):
            def run(block):
                dst[:, c0:c0 + block.shape[0]] = (block if scale == 1.0 else scale * block).T.astype(BF16)
            return run

        def dt_into(block):
            lane = lax.broadcasted_iota(jnp.int32, (D_MODEL, LANES), 1)
            wdt_ref[...] = jnp.where(lane < N_HEADS, block.T, 0.0).astype(BF16)

        def feature_rows(f0):
            return lambda r, n: wint_hbm.at[layer, pl.ds(f0 + r, n), :]

        for src, dst, width, scale in ((feature_rows(0), wz_ref, D_SSD, 0.5),
                                       (feature_rows(D_SSD), wxbc_ref, D_XBC, 1.0),
                                       (feature_rows(DT_COL + N_HEADS), wu_ref, D_POOL, 1.0)):
            for r in range(0, width, w_rows):
                stream.add(src(r, w_rows), transposed_into(dst, r, scale))
        stream.add(feature_rows(DT_COL)(0, LANES), dt_into)
        def dot_f32(a, b):
            a_hi, a_mid, _ = _split3(a)
            b_hi, b_mid, _ = _split3(b)
            return _dot(a_hi, b_hi) + _dot(a_mid, b_hi) + _dot(a_hi, b_mid)

        def pool_rows_into(g):
            lo = g * POOL_GROUP_DIM
            r = D_SSD + lo

            def run(block):
                scale = pools_ref[:, lo:lo + POOL_GROUP_DIM]
                wout_ref[r:r + w_rows, :] = dot_f32(poolw_in[g] * scale, block).astype(BF16)
                bias = jnp.broadcast_to(poolb_ref[:, lo:lo + POOL_GROUP_DIM] * scale, (SUBLANES, POOL_GROUP_DIM))
                pool_out_b[...] += dot_f32(bias, block)
            return run

        assert w_rows == POOL_GROUP_DIM
        for r in range(0, D_SSD, w_rows):
            stream.add(wout_hbm.at[layer, pl.ds(r, w_rows), :], cast_into(wout_ref, r, 0))
        for g in range(len(POOL_WINDOWS)):
            stream.add(wout_hbm.at[layer, pl.ds(D_SSD + g * POOL_GROUP_DIM, w_rows), :], pool_rows_into(g))
        pool_out_b[...] = jnp.zeros(pool_out_b.shape, F32)
        stream.prime()
        stream.drain()
        raw_s[:, 0:CONV_PAD, :] = jnp.zeros((D_XBC // LANES, CONV_PAD, LANES), F32)
        u_s[:, 0:POOL_PAD, :] = jnp.zeros((D_POOL // LANES, POOL_PAD, LANES), F32)
        for c in range(n_chunks):
            for run in proj_pieces(c, in_norm(xc_ref[rows(c), :], modc_ref)):
                run()

    @pl.when(tile_in_seq == 0)
    def _():
        st_s[...] = jnp.zeros(st_s.shape, F32)

    gt = modc_ref[0, 5:6, :]
    a_neg = -jnp.exp(alog_ref[...]) * LOG2_E
    conv_w_half = 0.5 * convw_ref[...]
    conv_b_half = 0.5 * convb_ref[...]
    ri = lax.broadcasted_iota(jnp.int32, (CHUNK, CHUNK), 0)
    ci = lax.broadcasted_iota(jnp.int32, (CHUNK, CHUNK), 1)
    causal = ri >= ci
    tril = jnp.where(causal, 1.0, 0.0).astype(BF16)
    low_half = ci < HEAD_DIM
    low_half_row = lax.broadcasted_iota(jnp.int32, (1, LANES), 1) < HEAD_DIM

    def pre(c, seq_tile):
        rs = rows(c)
        last = c == n_chunks - 1

        def carry(hist):
            return jnp.where(next_starts_seq, jnp.zeros_like(hist), hist) if last else hist

        base, nbase = conv_base(c), conv_base(c + 1)
        for j in range(D_XBC // LANES):
            cols = slice(j * LANES, (j + 1) * LANES)
            xc_half = conv_b_half[:, cols]
            for k in range(D_CONV):
                xc_half = xc_half + conv_w_half[k:k + 1, cols] * raw_s[j, pl.ds(base - (D_CONV - 1) + k, CHUNK), :]
            raw_s[j, nbase - CONV_PAD:nbase, :] = carry(raw_s[j, base + CHUNK - CONV_PAD:base + CHUNK, :])
            xbc_s[rs, cols] = _silu_of_twice(xc_half)
            if j % 2 == 1:
                emit(1)
        base, nbase = pool_base(c), pool_base(c + 1)
        diffs = []
        for gi, w in enumerate(POOL_WINDOWS):
            halves = []
            for j in range(gi * POOL_GROUP_DIM // LANES, (gi + 1) * POOL_GROUP_DIM // LANES):
                u_g = u_s[j, base:base + CHUNK, :]
                s = u_g
                for k in range(1, w):
                    s = s + u_s[j, pl.ds(base - k, CHUNK), :]
                if c == 0:
                    pos = seq_tile * MIX_ROWS + 1 + lax.broadcasted_iota(jnp.int32, (CHUNK, LANES), 0)
                    pooled = s / jnp.minimum(pos, w).astype(F32)
                else:
                    pooled = s * (1.0 / w)
                halves.append(pooled - u_g)
                u_s[j, nbase - POOL_PAD:nbase, :] = carry(u_s[j, base + CHUNK - POOL_PAD:base + CHUNK, :])
            diffs.append(jnp.concatenate(halves, axis=1).astype(BF16))
            emit(1)
        zs = _silu_of_twice(z_s[rs, :])
        dt_c = jax.nn.softplus(dtr_s[rs, :] + dtb_ref[...])
        emit()
        return diffs, zs, dt_c

    def ssd(c, zs, dt_c):
        rs = rows(c)
        p_hi, p_mid, p_lo = _split3(dt_c * a_neg)
        cs = _dot(tril, p_hi) + _dot(tril, p_mid) + _dot(tril, p_lo)
        cs_t = cs.T
        dt_t = dt_c.T
        w_t = dt_t * jnp.exp2(cs_t[:, CHUNK - 1:CHUNK] - cs_t)
        cs_dt_t = cs_t - jnp.log2(dt_t)
        xs_c = xbc_s[rs, 0:D_SSD]
        emit(1)
        y_pairs = []
        for g in range(N_GROUPS):
            b0 = D_SSD + g * D_STATE
            c0 = D_SSD + N_GROUPS * D_STATE + g * D_STATE
            b_t = xbc_s[rs, b0:b0 + D_STATE].T
            c_g = xbc_s[rs, c0:c0 + D_STATE]
            cb = _dot(c_g.astype(BF16), b_t.astype(BF16))
            for pr in range(HEADS_PER_GROUP // 2):
                lhs, bw_t, cd = [], [], []
                for r in (2 * pr, 2 * pr + 1):
                    hd = g * HEADS_PER_GROUP + r
                    cs_bc = jnp.broadcast_to(cs[:, hd:hd + 1], (CHUNK, CHUNK))
                    decay_dt = jnp.exp2(jnp.where(causal, cs_bc - cs_dt_t[hd:hd + 1, :], -jnp.inf))
                    scores = decay_dt * cb
                    sd = jnp.exp2(cs_bc)
                    lhs.append(jnp.concatenate([scores, c_g * sd], axis=1).astype(BF16))
                    bw_t.append((b_t * w_t[hd:hd + 1, :]).astype(BF16))
                    cd.append(sd[CHUNK - 1:CHUNK, :])
                col = (g * HEADS_PER_GROUP + 2 * pr) * HEAD_DIM
                x_p = xs_c[:, col:col + LANES]
                st = st_s[:, col:col + LANES]
                rhs = jnp.concatenate([x_p, st], axis=0).astype(BF16)
                y_pairs.append(jnp.where(low_half, _dot(lhs[0], rhs), _dot(lhs[1], rhs)))
                zero = jnp.zeros_like(x_p)
                x_split = jnp.concatenate(
                    [jnp.where(low_half, x_p, zero), jnp.where(low_half, zero, x_p)], axis=0)
                upd = _dot(jnp.concatenate(bw_t, axis=1), x_split.astype(BF16))
                st_s[:, col:col + LANES] = st * jnp.where(low_half_row, cd[0], cd[1]) + upd
                emit(1)
        y = jnp.concatenate(y_pairs, axis=1) + dskip_ref[...] * xs_c
        yz = y * zs
        gw = D_SSD // N_GROUPS
        parts = []
        for g in range(N_GROUPS):
            blk = yz[:, g * gw:(g + 1) * gw]
            ms = jnp.mean(blk * blk, axis=-1, keepdims=True)
            parts.append(blk * lax.rsqrt(ms + EPS))
        return (jnp.concatenate(parts, axis=1) * ssdnw_ref[...]).astype(BF16)

    def post_pieces(c, y_ssd, diffs):
        rs = rows(c)
        pooled_diff = jnp.concatenate(diffs, axis=1)

        def out_piece(lo, hi):
            def run():
                out = (_dot(y_ssd, wout_ref[0:D_SSD, lo:hi])
                       + _dot(pooled_diff, wout_ref[D_SSD:D_SSD + D_POOL, lo:hi]) + pool_out_b[0:1, lo:hi])
                o_ref[rs, lo:hi] = xc_ref[rs, lo:hi] + gt[:, lo:hi] * out
            return run

        return [out_piece(k, k + MXU_COLS) for k in range(0, D_MODEL, MXU_COLS)]

    def hand_over(vals):
        diffs, zs, dt_c = vals
        for gi in range(len(POOL_WINDOWS)):
            diff_s[:, gi * POOL_GROUP_DIM:(gi + 1) * POOL_GROUP_DIM] = diffs[gi]
        zs_s[...] = zs
        dtc_s[...] = dt_c

    @pl.when(step == 0)
    def _():
        hand_over(pre(0, tile_in_seq))

    diffs = [diff_s[:, gi * POOL_GROUP_DIM:(gi + 1) * POOL_GROUP_DIM] for gi in range(len(POOL_WINDOWS))]
    zs, dt_c = zs_s[...], dtc_s[...]
    h_n = in_norm(xn_ref[rows(0), :], modn_ref)
    pending.extend(proj_pieces(0, h_n))
    for c in range(n_chunks):
        y_ssd = ssd(c, zs, dt_c)
        pending.extend(post_pieces(c, y_ssd, diffs))
        if c + 1 < n_chunks:
            h_n = in_norm(xn_ref[rows(c + 1), :], modn_ref)
            emit(2)
            diffs, zs, dt_c = pre(c + 1, tile_in_seq)
            pending.extend(proj_pieces(c + 1, h_n))
        else:
            hand_over(pre(0, (step + 1) % tiles_per_seq))
            emit()


def _mixer(x, mod3, norm_w, w_in_t, convw, convb, dtb, alog, dskip, ssdnw,
           poolw, poolb, pools, w_out, *, layer):
    bsz, seq, _ = x.shape
    tiles_per_seq = seq // MIX_ROWS
    n_chunks = MIX_ROWS // CHUNK
    steps = bsz * tiles_per_seq
    nxt = lambda s: jnp.minimum(s + 1, steps - 1)
    hbm = pl.BlockSpec(memory_space=pl.ANY)
    vmem_consts = [convw, convb, dtb, alog, dskip, ssdnw, poolw, poolb, pools]
    x2 = x.reshape(bsz * seq, D_MODEL)
    out = pl.pallas_call(
        functools.partial(_mixer_kernel, layer=layer, tiles_per_seq=tiles_per_seq),
        grid=(steps,),
        in_specs=[
            pl.BlockSpec((MIX_ROWS, D_MODEL), lambda s: (s, 0)),
            pl.BlockSpec((MIX_ROWS, D_MODEL), lambda s: (nxt(s), 0)),
            pl.BlockSpec((1, N_MOD, D_MODEL), lambda s: (s // tiles_per_seq, 0, 0)),
            pl.BlockSpec((1, N_MOD, D_MODEL), lambda s: (nxt(s) // tiles_per_seq, 0, 0)),
            _const_spec(norm_w.shape), hbm,
        ] + [_const_spec(a.shape) for a in vmem_consts] + [hbm],
        out_specs=pl.BlockSpec((MIX_ROWS, D_MODEL), lambda s: (s, 0)),
        out_shape=jax.ShapeDtypeStruct((bsz * seq, D_MODEL), F32),
        scratch_shapes=[
            pltpu.VMEM((D_MODEL, D_SSD), BF16),
            pltpu.VMEM((D_MODEL, D_XBC), BF16),
            pltpu.VMEM((D_MODEL, LANES), BF16),
            pltpu.VMEM((D_MODEL, D_POOL), BF16),
            pltpu.VMEM((SUBLANES, D_MODEL), F32),
            pltpu.VMEM((D_SSD + D_POOL, D_MODEL), BF16),
            pltpu.VMEM((W_SLOTS, MIX_W_ROWS, D_MODEL), F32),
            pltpu.SemaphoreType.DMA((W_SLOTS,)),
            pltpu.VMEM((MIX_ROWS, D_SSD), F32),
            pltpu.VMEM((D_XBC // LANES, n_chunks * (CONV_PAD + CHUNK), LANES), F32),
            pltpu.VMEM((MIX_ROWS, LANES), F32),
            pltpu.VMEM((D_POOL // LANES, n_chunks * (POOL_PAD + CHUNK), LANES), F32),
            pltpu.VMEM((MIX_ROWS, D_XBC), F32),
            pltpu.VMEM((D_STATE, D_SSD), F32),
            pltpu.VMEM((CHUNK, D_POOL), BF16),
            pltpu.VMEM((CHUNK, D_SSD), F32),
            pltpu.VMEM((CHUNK, LANES), F32),
        ],
        compiler_params=pltpu.CompilerParams(
            dimension_semantics=("arbitrary",), vmem_limit_bytes=VMEM_LIMIT_BYTES),
        name="mixer",
    )(x2, x2, mod3, mod3, norm_w, w_in_t, *vmem_consts, w_out)
    return out.reshape(bsz, seq, D_MODEL)


def _pad_lanes(v):
    return jnp.pad(v.reshape(1, -1), ((0, 0), (0, LANES - v.shape[0])))


def kernel(x, c, w_ada, b_ada, ffn1_norm, ffn1_w_gate, ffn1_w_up, ffn1_w_down, mix_norm, w_in, conv_w, conv_b, dt_bias, a_log, d_skip, ssd_norm_w, pool_w, pool_b, pool_scale, w_out, ffn2_norm, ffn2_w_gate, ffn2_w_up, ffn2_w_down, final_norm):
    bsz = x.shape[0]
    depth = w_ada.shape[0]
    row = lambda v: v.reshape(1, -1)
    for i in range(depth):
        mod3 = _adaln(c, w_ada[i], row(b_ada[i])).reshape(bsz, N_MOD, D_MODEL)
        last = i == depth - 1

        x = _ffn(x, mod3, row(ffn1_norm[i]), ffn1_w_gate, ffn1_w_up, ffn1_w_down, row(final_norm),
                 layer=i, mod_idx=0, final=False)

        x = _mixer(
            x, mod3, row(mix_norm[i]), jnp.swapaxes(w_in, 1, 2),
            conv_w[i], row(conv_b[i]), _pad_lanes(dt_bias[i]), _pad_lanes(a_log[i]),
            row(jnp.repeat(d_skip[i], HEAD_DIM)), row(ssd_norm_w[i]),
            pool_w[i], row(pool_b[i]), row(pool_scale[i]), w_out, layer=i)

        x = _ffn(x, mod3, row(ffn2_norm[i]), ffn2_w_gate, ffn2_w_up, ffn2_w_down, row(final_norm),
                 layer=i, mod_idx=6, final=last)
    return x
```

```python
import functools

import jax
import jax.numpy as jnp
from jax import lax
from jax.experimental import pallas as pl
from jax.experimental.pallas import tpu as pltpu

F32 = jnp.float32
BF16 = jnp.bfloat16

D_MODEL = 1024
D_FF = 2816
N_MOD = 9
FFN_RES = 0.5
EPS = 1e-6
LOG2_E = 1.4426950408889634
D_SSD = 1024
HEAD_DIM = 64
N_HEADS = 16
N_GROUPS = 4
HEADS_PER_GROUP = N_HEADS // N_GROUPS
D_STATE = 128
D_CONV = 4
CHUNK = 128
D_POOL = 1024
POOL_WINDOWS = (2, 4, 8, 16)
POOL_GROUP_DIM = 256
D_XBC = D_SSD + 2 * N_GROUPS * D_STATE
DT_COL = D_SSD + D_XBC

LANES = 128
SUBLANES = 8
MXU_COLS = 256
VMEM_LIMIT_BYTES = 60 * 1024 * 1024

FFN_ROWS = 512
NORM_SLICES = 8
FFN_W_ROWS_IN = 128
FFN_W_ROWS_OUT = 256
MIX_W_ROWS = 256
W_SLOTS = 8
MIX_ROWS = 512
ADA_COLS = 1024
CONV_PAD = SUBLANES
POOL_PAD = 16

assert HEAD_DIM * 2 == LANES and D_STATE == LANES and CHUNK == LANES


def _dot(a, b):
    return jnp.dot(a, b, preferred_element_type=F32)


def _split3(v):
    hi = v.astype(BF16)
    r1 = v - hi.astype(F32)
    mid = r1.astype(BF16)
    lo = (r1 - mid.astype(F32)).astype(BF16)
    return hi, mid, lo


def _silu_of_twice(half):
    return half + half * jnp.tanh(half)


def _silu(v):
    return _silu_of_twice(0.5 * v)


def _rms(v, w):
    ms = jnp.mean(v * v, axis=-1, keepdims=True)
    return v * lax.rsqrt(ms + EPS) * w


def _const_spec(shape):
    zeros = (0,) * len(shape)
    return pl.BlockSpec(shape, lambda *_: zeros, pipeline_mode=pl.Buffered(1))


def _adaln_kernel(c_ref, w_ref, b_ref, o_ref):
    a_hi, a_mid, _ = _split3(_silu(c_ref[...]))
    res = _dot(jnp.concatenate([a_hi, a_mid], axis=0), w_ref[...].astype(BF16))
    o_ref[...] = res[0:SUBLANES] + res[SUBLANES:2 * SUBLANES] + b_ref[...]


def _adaln(c, w_ada, b_ada):
    bsz = c.shape[0]
    n = w_ada.shape[1]
    c_pad = jnp.pad(c, ((0, SUBLANES - bsz), (0, 0)))
    mod = pl.pallas_call(
        _adaln_kernel,
        grid=(n // ADA_COLS,),
        in_specs=[
            pl.BlockSpec((SUBLANES, D_MODEL), lambda j: (0, 0)),
            pl.BlockSpec((D_MODEL, ADA_COLS), lambda j: (0, j)),
            pl.BlockSpec((1, ADA_COLS), lambda j: (0, j)),
        ],
        out_specs=pl.BlockSpec((SUBLANES, ADA_COLS), lambda j: (0, j)),
        out_shape=jax.ShapeDtypeStruct((SUBLANES, n), F32),
        compiler_params=pltpu.CompilerParams(
            dimension_semantics=("arbitrary",), vmem_limit_bytes=VMEM_LIMIT_BYTES),
        name="adaln",
    )(c_pad, w_ada, b_ada)
    return mod[:bsz]


class _BlockStream:
    def __init__(self, stage_ref, sem):
        self.stage, self.sem, self.blocks = stage_ref, sem, []
        self.n_slots = stage_ref.shape[0]

    def add(self, src, sink):
        self.blocks.append((src, sink))

    def _slot(self, i):
        return self.stage.at[i % self.n_slots, pl.ds(0, self.blocks[i][0].shape[0])]

    def _copy(self, i):
        return pltpu.make_async_copy(self.blocks[i][0], self._slot(i), self.sem.at[i % self.n_slots])

    def prime(self):
        for i in range(min(self.n_slots, len(self.blocks))):
            self._copy(i).start()

    def drain(self):
        for i in range(len(self.blocks)):
            self._copy(i).wait()
            self.blocks[i][1](self._slot(i)[...])
            if i + self.n_slots < len(self.blocks):
                self._copy(i + self.n_slots).start()


def _stream_rows_bf16(stream, w_hbm, layer, dst_ref, scale=1.0):
    rows = stream.stage.shape[1]

    def sink(r):
        def run(block):
            dst_ref[r:r + rows, :] = (block if scale == 1.0 else scale * block).astype(BF16)
        return run

    for r in range(0, dst_ref.shape[0], rows):
        stream.add(w_hbm.at[layer, pl.ds(r, rows), :], sink(r))


def _ffn_kernel(xc_ref, xn_ref, modc_ref, modn_ref, nw_ref, wg_hbm, wu_hbm, wd_hbm, fw_ref, o_ref,
                h_s, wg_ref, wu_ref, wd_ref, stage_in, stage_out, sem_in, sem_out,
                *, layer, mod_idx, final):
    def in_norm(x, mod_ref):
        sh = mod_ref[0, pl.ds(mod_idx, 1), :]
        sc = mod_ref[0, pl.ds(mod_idx + 1, 1), :]
        gain = nw_ref[...] * (1.0 + sc)
        ms = jnp.mean(x * x, axis=-1, keepdims=True)
        return (x * lax.rsqrt(ms + EPS) * gain + sh).astype(BF16)

    step = pl.program_id(0)
    cur = h_s.at[step % 2]
    nxt = h_s.at[(step + 1) % 2]

    @pl.when(step == 0)
    def _():
        s_in, s_out = _BlockStream(stage_in, sem_in), _BlockStream(stage_out, sem_out)
        _stream_rows_bf16(s_in, wg_hbm, layer, wg_ref, scale=0.5)
        _stream_rows_bf16(s_in, wu_hbm, layer, wu_ref)
        _stream_rows_bf16(s_out, wd_hbm, layer, wd_ref)
        s_in.prime()
        s_out.prime()
        s_in.drain()
        s_out.drain()
        cur[...] = in_norm(xc_ref[...], modc_ref)

    n_chunks = D_FF // MXU_COLS
    slice_rows = FFN_ROWS // NORM_SLICES
    o = None
    for k in range(n_chunks):
        if k < NORM_SLICES:
            rs = slice(k * slice_rows, (k + 1) * slice_rows)
            nxt[rs, :] = in_norm(xn_ref[rs, :], modn_ref)
        cols = slice(k * MXU_COLS, (k + 1) * MXU_COLS)
        h = cur[...]
        g_half = _dot(h, wg_ref[:, cols])
        u = _dot(h, wu_ref[:, cols])
        a_k = (_silu_of_twice(g_half) * u).astype(BF16)
        part = _dot(a_k, wd_ref[cols, :])
        o = part if o is None else o + part
    gt = modc_ref[0, pl.ds(mod_idx + 2, 1), :]
    out = xc_ref[...] + (FFN_RES * gt) * o
    if final:
        out = _rms(out, fw_ref[...])
    o_ref[...] = out


def _ffn(x, mod3, norm_w, wg, wu, wd, final_w, *, layer, mod_idx, final):
    bsz, seq, _ = x.shape
    tiles_per_seq = seq // FFN_ROWS
    steps = bsz * tiles_per_seq
    nxt = lambda s: jnp.minimum(s + 1, steps - 1)
    x2 = x.reshape(bsz * seq, D_MODEL)
    hbm = pl.BlockSpec(memory_space=pl.ANY)
    out = pl.pallas_call(
        functools.partial(_ffn_kernel, layer=layer, mod_idx=mod_idx, final=final),
        grid=(steps,),
        in_specs=[
            pl.BlockSpec((FFN_ROWS, D_MODEL), lambda s: (s, 0)),
            pl.BlockSpec((FFN_ROWS, D_MODEL), lambda s: (nxt(s), 0)),
            pl.BlockSpec((1, N_MOD, D_MODEL), lambda s: (s // tiles_per_seq, 0, 0)),
            pl.BlockSpec((1, N_MOD, D_MODEL), lambda s: (nxt(s) // tiles_per_seq, 0, 0)),
            _const_spec((1, D_MODEL)),
            hbm, hbm, hbm,
            _const_spec((1, D_MODEL)),
        ],
        out_specs=pl.BlockSpec((FFN_ROWS, D_MODEL), lambda s: (s, 0)),
        out_shape=jax.ShapeDtypeStruct((bsz * seq, D_MODEL), F32),
        scratch_shapes=[
            pltpu.VMEM((2, FFN_ROWS, D_MODEL), BF16),
            pltpu.VMEM((D_MODEL, D_FF), BF16),
            pltpu.VMEM((D_MODEL, D_FF), BF16),
            pltpu.VMEM((D_FF, D_MODEL), BF16),
            pltpu.VMEM((W_SLOTS, FFN_W_ROWS_IN, D_FF), F32),
            pltpu.VMEM((W_SLOTS, FFN_W_ROWS_OUT, D_MODEL), F32),
            pltpu.SemaphoreType.DMA((W_SLOTS,)),
            pltpu.SemaphoreType.DMA((W_SLOTS,)),
        ],
        compiler_params=pltpu.CompilerParams(
            dimension_semantics=("arbitrary",), vmem_limit_bytes=VMEM_LIMIT_BYTES),
        name="ffn_final" if final else "ffn",
    )(x2, x2, mod3, mod3, norm_w, wg, wu, wd, final_w)
    return out.reshape(bsz, seq, D_MODEL)


def _mixer_kernel(xc_ref, xn_ref, modc_ref, modn_ref, nw_ref, wint_hbm,
                  convw_ref, convb_ref, dtb_ref, alog_ref, dskip_ref, ssdnw_ref,
                  poolw_in, poolb_ref, pools_ref, wout_hbm,
                  o_ref,
                  wz_ref, wxbc_ref, wdt_ref, wu_ref, poolw_ref, wout_ref, stage, sem,
                  z_s, raw_s, dtr_s, u_s, xbc_s, st_s, diff_s, zs_s, dtc_s, *, layer, tiles_per_seq):
    step = pl.program_id(0)
    tile_in_seq = step % tiles_per_seq
    next_starts_seq = tile_in_seq == tiles_per_seq - 1
    n_chunks = MIX_ROWS // CHUNK

    def in_norm(x, mod_ref):
        sh = mod_ref[0, 3:4, :]
        sc = mod_ref[0, 4:5, :]
        gain = nw_ref[...] * (1.0 + sc)
        ms = jnp.mean(x * x, axis=-1, keepdims=True)
        return (x * lax.rsqrt(ms + EPS) * gain + sh).astype(BF16)

    def rows(c):
        return slice(c * CHUNK, (c + 1) * CHUNK)

    def conv_base(c):
        return (c % n_chunks) * (CONV_PAD + CHUNK) + CONV_PAD

    def pool_base(c):
        return (c % n_chunks) * (POOL_PAD + CHUNK) + POOL_PAD

    pending = []

    def emit(k=None):
        n = len(pending) if k is None else min(k, len(pending))
        for _ in range(n):
            pending.pop(0)()

    def proj_pieces(c, h_n):
        rs = rows(c)

        def dense(dst, w_ref, lo, hi):
            def run():
                dst[rs, lo:hi] = _dot(h_n, w_ref[:, lo:hi])
            return run

        def slabs(dst, base, w_ref, lo, hi):
            def run():
                res = _dot(h_n, w_ref[:, lo:hi])
                for j in range(lo // LANES, hi // LANES):
                    dst[j, base:base + CHUNK, :] = res[:, j * LANES - lo:(j + 1) * LANES - lo]
            return run

        out = [slabs(raw_s, conv_base(c), wxbc_ref, k, k + MXU_COLS) for k in range(0, D_XBC, MXU_COLS)]
        out += [slabs(u_s, pool_base(c), wu_ref, k, k + MXU_COLS) for k in range(0, D_POOL, MXU_COLS)]
        out += [dense(z_s, wz_ref, k, k + MXU_COLS) for k in range(0, D_SSD, MXU_COLS)]
        out.append(dense(dtr_s, wdt_ref, 0, LANES))
        return out

    @pl.when(step == 0)
    def _():
        stream = _BlockStream(stage, sem)
        w_rows, w_cols = stage.shape[1], stage.shape[2]

        def cast_into(dst, r, c0):
            def run(block):
                dst[r:r + w_rows, c0:c0 + w_cols] = block.astype(BF16)
            return run

        def transposed_into(dst, c0, scale):
            def run(block):
                dst[:, c0:c0 + block.shape[0]] = (block if scale == 1.0 else scale * block).T.astype(BF16)
            return run

        def dt_into(block):
            lane = lax.broadcasted_iota(jnp.int32, (D_MODEL, LANES), 1)
            wdt_ref[...] = jnp.where(lane < N_HEADS, block.T, 0.0).astype(BF16)

        def feature_rows(f0):
            return lambda r, n: wint_hbm.at[layer, pl.ds(f0 + r, n), :]

        for src, dst, width, scale in ((feature_rows(0), wz_ref, D_SSD, 0.5),
                                       (feature_rows(D_SSD), wxbc_ref, D_XBC, 1.0),
                                       (feature_rows(DT_COL + N_HEADS), wu_ref, D_POOL, 1.0)):
            for r in range(0, width, w_rows):
                stream.add(src(r, w_rows), transposed_into(dst, r, scale))
        stream.add(feature_rows(DT_COL)(0, LANES), dt_into)
        for r in range(0, D_SSD + D_POOL, w_rows):
            stream.add(wout_hbm.at[layer, pl.ds(r, w_rows), :], cast_into(wout_ref, r, 0))
        stream.prime()
        poolw_ref[...] = poolw_in[...]
        stream.drain()
        raw_s[:, 0:CONV_PAD, :] = jnp.zeros((D_XBC // LANES, CONV_PAD, LANES), F32)
        u_s[:, 0:POOL_PAD, :] = jnp.zeros((D_POOL // LANES, POOL_PAD, LANES), F32)
        for c in range(n_chunks):
            for run in proj_pieces(c, in_norm(xc_ref[rows(c), :], modc_ref)):
                run()

    @pl.when(tile_in_seq == 0)
    def _():
        st_s[...] = jnp.zeros(st_s.shape, F32)

    gt = modc_ref[0, 5:6, :]
    a_neg = -jnp.exp(alog_ref[...]) * LOG2_E
    conv_w_half = 0.5 * convw_ref[...]
    conv_b_half = 0.5 * convb_ref[...]
    ri = lax.broadcasted_iota(jnp.int32, (CHUNK, CHUNK), 0)
    ci = lax.broadcasted_iota(jnp.int32, (CHUNK, CHUNK), 1)
    causal = ri >= ci
    tril = jnp.where(causal, 1.0, 0.0).astype(BF16)
    low_half = ci < HEAD_DIM
    low_half_row = lax.broadcasted_iota(jnp.int32, (1, LANES), 1) < HEAD_DIM

    def pre(c, seq_tile):
        rs = rows(c)
        last = c == n_chunks - 1

        def carry(hist):
            return jnp.where(next_starts_seq, jnp.zeros_like(hist), hist) if last else hist

        base, nbase = conv_base(c), conv_base(c + 1)
        for j in range(D_XBC // LANES):
            cols = slice(j * LANES, (j + 1) * LANES)
            xc_half = conv_b_half[:, cols]
            for k in range(D_CONV):
                xc_half = xc_half + conv_w_half[k:k + 1, cols] * raw_s[j, pl.ds(base - (D_CONV - 1) + k, CHUNK), :]
            raw_s[j, nbase - CONV_PAD:nbase, :] = carry(raw_s[j, base + CHUNK - CONV_PAD:base + CHUNK, :])
            xbc_s[rs, cols] = _silu_of_twice(xc_half)
            if j % 2 == 1:
                emit(1)
        base, nbase = pool_base(c), pool_base(c + 1)
        diffs = []
        for gi, w in enumerate(POOL_WINDOWS):
            halves = []
            for j in range(gi * POOL_GROUP_DIM // LANES, (gi + 1) * POOL_GROUP_DIM // LANES):
                u_g = u_s[j, base:base + CHUNK, :]
                s = u_g
                for k in range(1, w):
                    s = s + u_s[j, pl.ds(base - k, CHUNK), :]
                if c == 0:
                    pos = seq_tile * MIX_ROWS + 1 + lax.broadcasted_iota(jnp.int32, (CHUNK, LANES), 0)
                    pooled = s / jnp.minimum(pos, w).astype(F32)
                else:
                    pooled = s * (1.0 / w)
                halves.append(pooled - u_g)
                u_s[j, nbase - POOL_PAD:nbase, :] = carry(u_s[j, base + CHUNK - POOL_PAD:base + CHUNK, :])
            diffs.append(jnp.concatenate(halves, axis=1).astype(BF16))
            emit(1)
        zs = _silu_of_twice(z_s[rs, :])
        dt_c = jax.nn.softplus(dtr_s[rs, :] + dtb_ref[...])
        emit()
        return diffs, zs, dt_c

    def ssd(c, zs, dt_c):
        rs = rows(c)
        p_hi, p_mid, p_lo = _split3(dt_c * a_neg)
        cs = _dot(tril, p_hi) + _dot(tril, p_mid) + _dot(tril, p_lo)
        cs_t = cs.T
        dt_t = dt_c.T
        w_t = dt_t * jnp.exp2(cs_t[:, CHUNK - 1:CHUNK] - cs_t)
        cs_dt_t = cs_t - jnp.log2(dt_t)
        xs_c = xbc_s[rs, 0:D_SSD]
        emit(1)
        y_pairs = []
        for g in range(N_GROUPS):
            b0 = D_SSD + g * D_STATE
            c0 = D_SSD + N_GROUPS * D_STATE + g * D_STATE
            b_t = xbc_s[rs, b0:b0 + D_STATE].T
            c_g = xbc_s[rs, c0:c0 + D_STATE]
            cb = _dot(c_g.astype(BF16), b_t.astype(BF16))
            for pr in range(HEADS_PER_GROUP // 2):
                lhs, bw_t, cd = [], [], []
                for r in (2 * pr, 2 * pr + 1):
                    hd = g * HEADS_PER_GROUP + r
                    cs_bc = jnp.broadcast_to(cs[:, hd:hd + 1], (CHUNK, CHUNK))
                    decay_dt = jnp.exp2(jnp.where(causal, cs_bc - cs_dt_t[hd:hd + 1, :], -jnp.inf))
                    scores = decay_dt * cb
                    sd = jnp.exp2(cs_bc)
                    lhs.append(jnp.concatenate([scores, c_g * sd], axis=1).astype(BF16))
                    bw_t.append((b_t * w_t[hd:hd + 1, :]).astype(BF16))
                    cd.append(sd[CHUNK - 1:CHUNK, :])
                col = (g * HEADS_PER_GROUP + 2 * pr) * HEAD_DIM
                x_p = xs_c[:, col:col + LANES]
                st = st_s[:, col:col + LANES]
                rhs = jnp.concatenate([x_p, st], axis=0).astype(BF16)
                y_pairs.append(jnp.where(low_half, _dot(lhs[0], rhs), _dot(lhs[1], rhs)))
                zero = jnp.zeros_like(x_p)
                x_split = jnp.concatenate(
                    [jnp.where(low_half, x_p, zero), jnp.where(low_half, zero, x_p)], axis=0)
                upd = _dot(jnp.concatenate(bw_t, axis=1), x_split.astype(BF16))
                st_s[:, col:col + LANES] = st * jnp.where(low_half_row, cd[0], cd[1]) + upd
                emit(1)
        y = jnp.concatenate(y_pairs, axis=1) + dskip_ref[...] * xs_c
        yz = y * zs
        gw = D_SSD // N_GROUPS
        parts = []
        for g in range(N_GROUPS):
            blk = yz[:, g * gw:(g + 1) * gw]
            ms = jnp.mean(blk * blk, axis=-1, keepdims=True)
            parts.append(blk * lax.rsqrt(ms + EPS))
        return (jnp.concatenate(parts, axis=1) * ssdnw_ref[...]).astype(BF16)

    def post_pieces(c, y_ssd, diffs):
        rs = rows(c)
        outs = [None] * len(POOL_WINDOWS)

        def pool_piece(gi):
            def run():
                lo, hi = gi * POOL_GROUP_DIM, (gi + 1) * POOL_GROUP_DIM
                o = (_dot(diffs[gi], poolw_ref[gi]) + poolb_ref[:, lo:hi]) * pools_ref[:, lo:hi]
                outs[gi] = o.astype(BF16)
            return run

        def out_piece(lo, hi):
            def run():
                y_pool = jnp.concatenate(outs, axis=1)
                out = (_dot(y_ssd, wout_ref[0:D_SSD, lo:hi])
                       + _dot(y_pool, wout_ref[D_SSD:D_SSD + D_POOL, lo:hi]))
                o_ref[rs, lo:hi] = xc_ref[rs, lo:hi] + gt[:, lo:hi] * out
            return run

        return ([pool_piece(gi) for gi in range(len(POOL_WINDOWS))]
                + [out_piece(k, k + MXU_COLS) for k in range(0, D_MODEL, MXU_COLS)])

    def hand_over(vals):
        diffs, zs, dt_c = vals
        for gi in range(len(POOL_WINDOWS)):
            diff_s[:, gi * POOL_GROUP_DIM:(gi + 1) * POOL_GROUP_DIM] = diffs[gi]
        zs_s[...] = zs
        dtc_s[...] = dt_c

    @pl.when(step == 0)
    def _():
        hand_over(pre(0, tile_in_seq))

    diffs = [diff_s[:, gi * POOL_GROUP_DIM:(gi + 1) * POOL_GROUP_DIM] for gi in range(len(POOL_WINDOWS))]
    zs, dt_c = zs_s[...], dtc_s[...]
    h_n = in_norm(xn_ref[rows(0), :], modn_ref)
    pending.extend(proj_pieces(0, h_n))
    for c in range(n_chunks):
        y_ssd = ssd(c, zs, dt_c)
        pending.extend(post_pieces(c, y_ssd, diffs))
        if c + 1 < n_chunks:
            h_n = in_norm(xn_ref[rows(c + 1), :], modn_ref)
            emit(2)
            diffs, zs, dt_c = pre(c + 1, tile_in_seq)
            pending.extend(proj_pieces(c + 1, h_n))
        else:
            hand_over(pre(0, (step + 1) % tiles_per_seq))
            emit()


def _mixer(x, mod3, norm_w, w_in_t, convw, convb, dtb, alog, dskip, ssdnw,
           poolw, poolb, pools, w_out, *, layer):
    bsz, seq, _ = x.shape
    tiles_per_seq = seq // MIX_ROWS
    n_chunks = MIX_ROWS // CHUNK
    steps = bsz * tiles_per_seq
    nxt = lambda s: jnp.minimum(s + 1, steps - 1)
    hbm = pl.BlockSpec(memory_space=pl.ANY)
    vmem_consts = [convw, convb, dtb, alog, dskip, ssdnw, poolw, poolb, pools]
    x2 = x.reshape(bsz * seq, D_MODEL)
    out = pl.pallas_call(
        functools.partial(_mixer_kernel, layer=layer, tiles_per_seq=tiles_per_seq),
        grid=(steps,),
        in_specs=[
            pl.BlockSpec((MIX_ROWS, D_MODEL), lambda s: (s, 0)),
            pl.BlockSpec((MIX_ROWS, D_MODEL), lambda s: (nxt(s), 0)),
            pl.BlockSpec((1, N_MOD, D_MODEL), lambda s: (s // tiles_per_seq, 0, 0)),
            pl.BlockSpec((1, N_MOD, D_MODEL), lambda s: (nxt(s) // tiles_per_seq, 0, 0)),
            _const_spec(norm_w.shape), hbm,
        ] + [_const_spec(a.shape) for a in vmem_consts] + [hbm],
        out_specs=pl.BlockSpec((MIX_ROWS, D_MODEL), lambda s: (s, 0)),
        out_shape=jax.ShapeDtypeStruct((bsz * seq, D_MODEL), F32),
        scratch_shapes=[
            pltpu.VMEM((D_MODEL, D_SSD), BF16),
            pltpu.VMEM((D_MODEL, D_XBC), BF16),
            pltpu.VMEM((D_MODEL, LANES), BF16),
            pltpu.VMEM((D_MODEL, D_POOL), BF16),
            pltpu.VMEM(poolw.shape, BF16),
            pltpu.VMEM((D_SSD + D_POOL, D_MODEL), BF16),
            pltpu.VMEM((W_SLOTS, MIX_W_ROWS, D_MODEL), F32),
            pltpu.SemaphoreType.DMA((W_SLOTS,)),
            pltpu.VMEM((MIX_ROWS, D_SSD), F32),
            pltpu.VMEM((D_XBC // LANES, n_chunks * (CONV_PAD + CHUNK), LANES), F32),
            pltpu.VMEM((MIX_ROWS, LANES), F32),
            pltpu.VMEM((D_POOL // LANES, n_chunks * (POOL_PAD + CHUNK), LANES), F32),
            pltpu.VMEM((MIX_ROWS, D_XBC), F32),
            pltpu.VMEM((D_STATE, D_SSD), F32),
            pltpu.VMEM((CHUNK, D_POOL), BF16),
            pltpu.VMEM((CHUNK, D_SSD), F32),
            pltpu.VMEM((CHUNK, LANES), F32),
        ],
        compiler_params=pltpu.CompilerParams(
            dimension_semantics=("arbitrary",), vmem_limit_bytes=VMEM_LIMIT_BYTES),
        name="mixer",
    )(x2, x2, mod3, mod3, norm_w, w_in_t, *vmem_consts, w_out)
    return out.reshape(bsz, seq, D_MODEL)


def _pad_lanes(v):
    return jnp.pad(v.reshape(1, -1), ((0, 0), (0, LANES - v.shape[0])))


def kernel(x, c, w_ada, b_ada, ffn1_norm, ffn1_w_gate, ffn1_w_up, ffn1_w_down, mix_norm, w_in, conv_w, conv_b, dt_bias, a_log, d_skip, ssd_norm_w, pool_w, pool_b, pool_scale, w_out, ffn2_norm, ffn2_w_gate, ffn2_w_up, ffn2_w_down, final_norm):
    bsz = x.shape[0]
    depth = w_ada.shape[0]
    row = lambda v: v.reshape(1, -1)
    for i in range(depth):
        mod3 = _adaln(c, w_ada[i], row(b_ada[i])).reshape(bsz, N_MOD, D_MODEL)
        last = i == depth - 1

        x = _ffn(x, mod3, row(ffn1_norm[i]), ffn1_w_gate, ffn1_w_up, ffn1_w_down, row(final_norm),
                 layer=i, mod_idx=0, final=False)

        x = _mixer(
            x, mod3, row(mix_norm[i]), jnp.swapaxes(w_in, 1, 2),
            conv_w[i], row(conv_b[i]), _pad_lanes(dt_bias[i]), _pad_lanes(a_log[i]),
            row(jnp.repeat(d_skip[i], HEAD_DIM)), row(ssd_norm_w[i]),
            pool_w[i].astype(BF16), row(pool_b[i]), row(pool_scale[i]), w_out, layer=i)

        x = _ffn(x, mod3, row(ffn2_norm[i]), ffn2_w_gate, ffn2_w_up, ffn2_w_down, row(final_norm),
                 layer=i, mod_idx=6, final=last)
    return x
```
